```python
import math
import jax, jax.numpy as jnp
from jax import lax
import numpy as np

D_MODEL = 1024
BATCH = 8
SEQ = 2048
DEPTH = 2
DEC_BATCH = 32
DEC_SEQ = 1
PAST_LEN = 8192
PAGE_SIZE = 128

H_A = 8
D_A = 64
H_IDX = 8
D_IDX = 64
TOPK_MAX = 256
H_B = 4
D_B = 64
H_M = 4
D_M = 128
N_MEM = 256
NUM_BUCKETS = 32
MAX_DISTANCE = 128
D_FF = 2816
N_EXPERTS = 8
TOP_K_EXPERTS = 2
N_DENSE = (DEPTH + 1) // 2
N_MOE = DEPTH // 2
QBLOCK = 128
LN_EPS = 1e-5
SUBLN_EPS = 1e-5
DEEPNORM_ALPHA = (2 * DEPTH) ** 0.25
DEEPNORM_BETA = (8 * DEPTH) ** -0.25
N_BRANCH = 3
A_WIDTH = H_A * D_A
B_WIDTH = H_B * 2 * D_B
M_WIDTH = H_M * D_M
IN_SPLITS = (A_WIDTH, A_WIDTH, A_WIDTH, H_IDX * D_IDX, D_IDX, H_IDX,
             B_WIDTH, B_WIDTH, B_WIDTH, M_WIDTH, N_BRANCH * D_MODEL)
IN_VALUE_COLS = (2, 8)
D_IN = sum(IN_SPLITS)
IN_OFFSETS = tuple(sum(IN_SPLITS[:i + 1]) for i in range(len(IN_SPLITS) - 1))

kernel_name = "hybrid_dsa_diffattn_memory_decoder_step"


def layer_norm(x, g, b):
    xf = x.astype(jnp.float32)
    mu = jnp.mean(xf, axis=-1, keepdims=True)
    var = jnp.mean(jnp.square(xf - mu), axis=-1, keepdims=True)
    return ((xf - mu) * lax.rsqrt(var + LN_EPS) * g.astype(jnp.float32) + b.astype(jnp.float32)).astype(x.dtype)


def t5_bucket(rel):
    n = jnp.maximum(rel, 0)
    max_exact = NUM_BUCKETS // 2
    nf = jnp.maximum(n, 1).astype(jnp.float32)
    large = max_exact + (jnp.log(nf / max_exact) / math.log(MAX_DISTANCE / max_exact)
                         * (NUM_BUCKETS - max_exact)).astype(jnp.int32)
    large = jnp.minimum(large, NUM_BUCKETS - 1)
    return jnp.where(n < max_exact, n, large)


def project_in(x, w):
    B, T = x.shape[0], x.shape[1]
    h = jnp.einsum('btd,de->bte', x, w)
    a_q, a_k, a_v, i_q, i_k, i_w, b_q, b_k, b_v, m_q, gates = jnp.split(h, IN_OFFSETS, axis=-1)
    return (a_q.reshape(B, T, H_A, D_A), a_k.reshape(B, T, H_A, D_A), a_v.reshape(B, T, H_A, D_A),
            i_q.reshape(B, T, H_IDX, D_IDX), i_k, i_w,
            b_q.reshape(B, T, H_B, 2, D_B), b_k.reshape(B, T, H_B, 2, D_B), b_v.reshape(B, T, H_B, 2 * D_B),
            m_q.reshape(B, T, H_M, D_M), gates.reshape(B, T, N_BRANCH, D_MODEL))


def project_mem(mem, w):
    B, M = mem.shape[0], mem.shape[1]
    kv = jnp.einsum('bmd,de->bme', mem, w)
    mk, mv = jnp.split(kv, 2, axis=-1)
    return mk.reshape(B, M, H_M, D_M), mv.reshape(B, M, H_M, D_M)


def sweep_queries(fn, *q_args):
    *arrs, qpos = q_args
    T = qpos.shape[0]
    if T <= QBLOCK or T % QBLOCK != 0:
        return fn(*arrs, qpos)
    nb = T // QBLOCK
    blocks = tuple(jnp.swapaxes(a.reshape((a.shape[0], nb, QBLOCK) + a.shape[2:]), 0, 1) for a in arrs)
    out = lax.map(lambda xs: fn(*xs[:-1], xs[-1]), blocks + (qpos.reshape(nb, QBLOCK),))
    out = jnp.swapaxes(out, 0, 1)
    return out.reshape((out.shape[0], T) + out.shape[3:])


gather_rows = jax.vmap(lambda rows, idx: rows[idx])


def paged_rows(pool, page_table):
    g = pool[page_table]
    return g.reshape((g.shape[0], g.shape[1] * g.shape[2]) + g.shape[3:])


def fetch_paged(pool, new_rows, sel, page_table):
    past = sel < PAST_LEN
    ps = jnp.minimum(sel, PAST_LEN - 1)
    phys = page_table[jnp.arange(page_table.shape[0])[:, None, None], ps // PAGE_SIZE]
    rows_past = pool[phys, ps % PAGE_SIZE]
    ns = jnp.clip(sel - PAST_LEN, 0, new_rows.shape[1] - 1)
    rows_new = gather_rows(new_rows, ns)
    mask = past.reshape(past.shape + (1,) * (rows_past.ndim - past.ndim))
    return jnp.where(mask, rows_past, rows_new)


def dsa_block(q, qi, wi, qpos, idx_keys, fetch_kv, topk, rel_bias):
    L = idx_keys.shape[1]
    kpos = jnp.arange(L, dtype=jnp.int32)
    dots = jnp.einsum('bthd,bsd->bths', qi, idx_keys).astype(jnp.float32) * (D_IDX ** -0.5)
    score = jnp.einsum('bth,bths->bts', wi.astype(jnp.float32) * (H_IDX ** -0.5), jax.nn.relu(dots))
    causal = kpos[None, :] <= qpos[:, None]
    score = jnp.where(causal[None], score, -jnp.inf)
    _, sel = lax.top_k(score, topk)
    valid = sel <= qpos[None, :, None]
    k_sel, v_sel = fetch_kv(sel)
    logits = jnp.einsum('bthd,btkhd->bthk', q, k_sel).astype(jnp.float32) * (D_A ** -0.5)
    bias = rel_bias[t5_bucket(qpos[None, :, None] - sel)][..., :H_A]
    logits = logits + jnp.swapaxes(bias, -1, -2).astype(jnp.float32)
    logits = jnp.where(valid[:, :, None, :], logits, -jnp.inf)
    p = jax.nn.softmax(logits, axis=-1).astype(v_sel.dtype)
    return jnp.einsum('bthk,btkhd->bthd', p, v_sel)


def diff_block(q, qpos, k, v, lam, rel_bias):
    S = k.shape[1]
    kpos = jnp.arange(S, dtype=jnp.int32)
    logits = jnp.einsum('bqhcd,bkhcd->bhcqk', q, k).astype(jnp.float32) * (D_B ** -0.5)
    bias = rel_bias[t5_bucket(qpos[:, None] - kpos[None, :])][..., H_A:]
    logits = logits + jnp.transpose(bias, (2, 0, 1))[None, :, None].astype(jnp.float32)
    causal = kpos[None, :] <= qpos[:, None]
    logits = jnp.where(causal, logits, -jnp.inf)
    p = jax.nn.softmax(logits, axis=-1)
    attn = (p[:, :, 0] - lam * p[:, :, 1]).astype(v.dtype)
    return jnp.einsum('bhqk,bkhe->bqhe', attn, v)


def diff_finish(o, g, lam_init):
    of = o.astype(jnp.float32)
    of = of * lax.rsqrt(jnp.mean(jnp.square(of), axis=-1, keepdims=True) + SUBLN_EPS)
    return (of * g.astype(jnp.float32) * (1.0 - lam_init)).astype(o.dtype)


def diff_lambda(lq1, lk1, lq2, lk2, lam_init):
    f = lambda a: a.astype(jnp.float32)
    return jnp.exp(jnp.sum(f(lq1) * f(lk1))) - jnp.exp(jnp.sum(f(lq2) * f(lk2))) + lam_init


def mem_attend(q, k, v):
    logits = jnp.einsum('bthd,bmhd->bhtm', q, k).astype(jnp.float32) * (D_M ** -0.5)
    p = jax.nn.softmax(logits, axis=-1).astype(v.dtype)
    return jnp.einsum('bhtm,bmhd->bthd', p, v)


def merge_branches(gates, o_a, o_b, o_m, w_ba, w_bb, w_bm, w_out):
    B, T = gates.shape[0], gates.shape[1]
    g = jax.nn.sigmoid(gates.astype(jnp.float32)).astype(o_a.dtype)
    y_a = jnp.einsum('bte,ed->btd', o_a.reshape(B, T, -1), w_ba)
    y_b = jnp.einsum('bte,ed->btd', o_b.reshape(B, T, -1), w_bb)
    y_m = jnp.einsum('bte,ed->btd', o_m.reshape(B, T, -1), w_bm)
    h = g[:, :, 0] * y_a + g[:, :, 1] * y_b + g[:, :, 2] * y_m
    return jnp.einsum('btd,de->bte', h, w_out)


def swiglu(x, wg, wu, wd):
    h = jax.nn.silu(jnp.einsum('btd,df->btf', x, wg)) * jnp.einsum('btd,df->btf', x, wu)
    return jnp.einsum('btf,fd->btd', h, wd)


def moe_swiglu(x, router_w, eg, eu, ed):
    logits = jnp.einsum('btd,de->bte', x, router_w).astype(jnp.float32)
    top_v, top_i = lax.top_k(logits, TOP_K_EXPERTS)
    w = jax.nn.softmax(top_v, axis=-1)
    gate = jnp.sum(jax.nn.one_hot(top_i, N_EXPERTS, dtype=jnp.float32) * w[..., None], axis=-2)
    gate = gate.astype(x.dtype)
    y = jnp.zeros_like(x)
    for e in range(N_EXPERTS):
        y = y + gate[..., e:e + 1] * swiglu(x, eg[e], eu[e], ed[e])
    return y


def setup_inputs(seed: int = 0) -> dict:
    key = jax.random.key(seed)
    ks = iter(jax.random.split(key, 48))

    def nrm(shape, scale):
        return jax.random.normal(next(ks), shape, jnp.float32) * scale

    n_pages = PAST_LEN // PAGE_SIZE
    n_used = DEC_BATCH * n_pages
    n_pool = n_used + n_used // 4
    perm = jax.random.permutation(next(ks), n_pool)
    page_table = perm[:n_used].reshape(DEC_BATCH, n_pages).astype(jnp.int32)

    col_scale = np.concatenate([np.full((w,), DEEPNORM_BETA if i in IN_VALUE_COLS else 1.0, np.float32)
                                for i, w in enumerate(IN_SPLITS)])
    mem_scale = np.concatenate([np.ones((M_WIDTH,), np.float32), np.full((M_WIDTH,), DEEPNORM_BETA, np.float32)])
    B_ = DEEPNORM_BETA
    return {
        'x_prompt': nrm((BATCH, SEQ, D_MODEL), 1.0),
        'x_sample': nrm((DEC_BATCH, DEC_SEQ, D_MODEL), 1.0),
        'cache_a_k': nrm((DEPTH, n_pool, PAGE_SIZE, H_A, D_A), 1.0),
        'cache_a_v': nrm((DEPTH, n_pool, PAGE_SIZE, H_A, D_A), B_),
        'cache_idx_k': nrm((DEPTH, n_pool, PAGE_SIZE, D_IDX), 1.0),
        'cache_b_k': nrm((DEPTH, n_pool, PAGE_SIZE, H_B, 2, D_B), 1.0),
        'cache_b_v': nrm((DEPTH, n_pool, PAGE_SIZE, H_B, 2 * D_B), B_),
        'cache_mem_k': nrm((DEPTH, DEC_BATCH, N_MEM, H_M, D_M), 1.0),
        'cache_mem_v': nrm((DEPTH, DEC_BATCH, N_MEM, H_M, D_M), B_),
        'page_table': page_table,
        'mem_prompt': nrm((BATCH, N_MEM, D_MODEL), 1.0),
        'w_in': nrm((DEPTH, D_MODEL, D_IN), D_MODEL ** -0.5) * jnp.asarray(col_scale),
        'w_mem_kv': nrm((DEPTH, D_MODEL, 2 * M_WIDTH), D_MODEL ** -0.5) * jnp.asarray(mem_scale),
        'lambda_q1': nrm((DEPTH, D_B), 0.1),
        'lambda_k1': nrm((DEPTH, D_B), 0.1),
        'lambda_q2': nrm((DEPTH, D_B), 0.1),
        'lambda_k2': nrm((DEPTH, D_B), 0.1),
        'subln_g': 1.0 + nrm((DEPTH, 2 * D_B), 0.02),
        'w_branch_a': nrm((DEPTH, A_WIDTH, D_MODEL), A_WIDTH ** -0.5),
        'w_branch_b': nrm((DEPTH, B_WIDTH, D_MODEL), B_WIDTH ** -0.5),
        'w_branch_m': nrm((DEPTH, M_WIDTH, D_MODEL), M_WIDTH ** -0.5),
        'w_out': nrm((DEPTH, D_MODEL, D_MODEL), D_MODEL ** -0.5) * B_,
        'rel_bias': nrm((NUM_BUCKETS, H_A + H_B), 0.5),
        'ln1_g': 1.0 + nrm((DEPTH, D_MODEL), 0.02),
        'ln1_b': nrm((DEPTH, D_MODEL), 0.02),
        'ln2_g': 1.0 + nrm((DEPTH, D_MODEL), 0.02),
        'ln2_b': nrm((DEPTH, D_MODEL), 0.02),
        'ffn_w_gate': nrm((N_DENSE, D_MODEL, D_FF), D_MODEL ** -0.5),
        'ffn_w_up': nrm((N_DENSE, D_MODEL, D_FF), D_MODEL ** -0.5),
        'ffn_w_down': nrm((N_DENSE, D_FF, D_MODEL), D_FF ** -0.5) * B_,
        'router_w': nrm((N_MOE, D_MODEL, N_EXPERTS), D_MODEL ** -0.5),
        'expert_w_gate': nrm((N_MOE, N_EXPERTS, D_MODEL, D_FF), D_MODEL ** -0.5),
        'expert_w_up': nrm((N_MOE, N_EXPERTS, D_MODEL, D_FF), D_MODEL ** -0.5),
        'expert_w_down': nrm((N_MOE, N_EXPERTS, D_FF, D_MODEL), D_FF ** -0.5) * B_,
    }


def reference(x_prompt, x_sample, cache_a_k, cache_a_v, cache_idx_k, cache_b_k, cache_b_v,
              cache_mem_k, cache_mem_v, page_table, mem_prompt,
              w_in, w_mem_kv, lambda_q1, lambda_k1, lambda_q2, lambda_k2, subln_g,
              w_branch_a, w_branch_b, w_branch_m, w_out, rel_bias,
              ln1_g, ln1_b, ln2_g, ln2_b, ffn_w_gate, ffn_w_up, ffn_w_down,
              router_w, expert_w_gate, expert_w_up, expert_w_down):
    T_p = x_prompt.shape[1]
    T_s = x_sample.shape[1]
    qpos_p = jnp.arange(T_p, dtype=jnp.int32)
    qpos_s = PAST_LEN + jnp.arange(T_s, dtype=jnp.int32)
    topk_p = min(TOPK_MAX, T_p // 4)
    topk_s = min(TOPK_MAX, (PAST_LEN + T_s) // 4)

    def channel_mixer(l, x):
        if l % 2 == 0:
            j = l // 2
            return swiglu(x, ffn_w_gate[j], ffn_w_up[j], ffn_w_down[j])
        j = l // 2
        return moe_swiglu(x, router_w[j], expert_w_gate[j], expert_w_up[j], expert_w_down[j])

    st = {n: [] for n in ('akp', 'avp', 'ikp', 'bkp', 'bvp', 'mkp', 'mvp', 'aks', 'avs', 'iks', 'bks', 'bvs')}
    x_p, x_s = x_prompt, x_sample
    for l in range(DEPTH):
        lam_init = 0.8 - 0.6 * math.exp(-0.3 * l)
        lam = diff_lambda(lambda_q1[l], lambda_k1[l], lambda_q2[l], lambda_k2[l], lam_init)

        a_q, a_k, a_v, i_q, i_k, i_w, b_q, b_k, b_v, m_q, gates = project_in(x_p, w_in[l])
        fetch_p = lambda sel, a_k=a_k, a_v=a_v: (gather_rows(a_k, sel), gather_rows(a_v, sel))
        o_a = sweep_queries(lambda q, qi, wi, qp: dsa_block(q, qi, wi, qp, i_k, fetch_p, topk_p, rel_bias),
                            a_q, i_q, i_w, qpos_p)
        o_b = sweep_queries(lambda q, qp: diff_block(q, qp, b_k, b_v, lam, rel_bias), b_q, qpos_p)
        o_b = diff_finish(o_b, subln_g[l], lam_init)
        mk, mv = project_mem(mem_prompt, w_mem_kv[l])
        o_m = mem_attend(m_q, mk, mv)
        mix = merge_branches(gates, o_a, o_b, o_m, w_branch_a[l], w_branch_b[l], w_branch_m[l], w_out[l])
        x_p = layer_norm(DEEPNORM_ALPHA * x_p + mix, ln1_g[l], ln1_b[l])
        x_p = layer_norm(DEEPNORM_ALPHA * x_p + channel_mixer(l, x_p), ln2_g[l], ln2_b[l])
        st['akp'].append(a_k); st['avp'].append(a_v); st['ikp'].append(i_k)
        st['bkp'].append(b_k); st['bvp'].append(b_v); st['mkp'].append(mk); st['mvp'].append(mv)

        a_q, a_k, a_v, i_q, i_k, i_w, b_q, b_k, b_v, m_q, gates = project_in(x_s, w_in[l])
        idx_all = jnp.concatenate([paged_rows(cache_idx_k[l], page_table), i_k], axis=1)
        pool_k, pool_v = cache_a_k[l], cache_a_v[l]
        fetch_s = lambda sel, a_k=a_k, a_v=a_v, pool_k=pool_k, pool_v=pool_v: (
            fetch_paged(pool_k, a_k, sel, page_table), fetch_paged(pool_v, a_v, sel, page_table))
        o_a = sweep_queries(lambda q, qi, wi, qp: dsa_block(q, qi, wi, qp, idx_all, fetch_s, topk_s, rel_bias),
                            a_q, i_q, i_w, qpos_s)
        bk_all = jnp.concatenate([paged_rows(cache_b_k[l], page_table), b_k], axis=1)
        bv_all = jnp.concatenate([paged_rows(cache_b_v[l], page_table), b_v], axis=1)
        o_b = sweep_queries(lambda q, qp: diff_block(q, qp, bk_all, bv_all, lam, rel_bias), b_q, qpos_s)
        o_b = diff_finish(o_b, subln_g[l], lam_init)
        o_m = mem_attend(m_q, cache_mem_k[l], cache_mem_v[l])
        mix = merge_branches(gates, o_a, o_b, o_m, w_branch_a[l], w_branch_b[l], w_branch_m[l], w_out[l])
        x_s = layer_norm(DEEPNORM_ALPHA * x_s + mix, ln1_g[l], ln1_b[l])
        x_s = layer_norm(DEEPNORM_ALPHA * x_s + channel_mixer(l, x_s), ln2_g[l], ln2_b[l])
        st['aks'].append(a_k); st['avs'].append(a_v); st['iks'].append(i_k)
        st['bks'].append(b_k); st['bvs'].append(b_v)

    y_prompt, y_sample = x_p, x_s
    new_a_k_prompt = jnp.stack(st['akp'])
    new_a_v_prompt = jnp.stack(st['avp'])
    new_idx_k_prompt = jnp.stack(st['ikp'])
    new_b_k_prompt = jnp.stack(st['bkp'])
    new_b_v_prompt = jnp.stack(st['bvp'])
    new_mem_k_prompt = jnp.stack(st['mkp'])
    new_mem_v_prompt = jnp.stack(st['mvp'])
    new_a_k_sample = jnp.stack(st['aks'])
    new_a_v_sample = jnp.stack(st['avs'])
    new_idx_k_sample = jnp.stack(st['iks'])
    new_b_k_sample = jnp.stack(st['bks'])
    new_b_v_sample = jnp.stack(st['bvs'])
    return (y_prompt, y_sample, new_a_k_prompt, new_a_v_prompt, new_idx_k_prompt, new_b_k_prompt,
            new_b_v_prompt, new_mem_k_prompt, new_mem_v_prompt, new_a_k_sample, new_a_v_sample,
            new_idx_k_sample, new_b_k_sample, new_b_v_sample)
```

```python
import functools
import math

import jax
import jax.numpy as jnp
import numpy as np
from jax import lax
from jax.experimental import pallas as pl
from jax.experimental.pallas import tpu as pltpu

F32 = jnp.float32
BF16 = jnp.bfloat16
HIGHEST = lax.Precision.HIGHEST

H_A, D_A = 8, 64
H_IDX, D_IDX = 8, 64
TOPK_MAX = 256
H_B, D_B = 4, 64
H_M, D_M = 4, 128
NUM_BUCKETS, MAX_DISTANCE = 32, 128
N_EXPERTS = 8
LN_EPS = 1e-5
SUBLN_EPS = 1e-5
A_WIDTH = H_A * D_A
B_WIDTH = H_B * 2 * D_B
M_WIDTH = H_M * D_M
HEAD_COLS = 512
PAGE = 128

NEG = -1e30
KEY_NEG_INF = -2139095041
INT_MIN = -2147483648

VMEM_LIMIT = 56 * 1024 * 1024

TQ = 256


def _cparams(sem):
    return pltpu.CompilerParams(dimension_semantics=sem, vmem_limit_bytes=VMEM_LIMIT)


def _dot(a, b, precise=False):
    if precise:
        return jnp.dot(a, b, preferred_element_type=F32, precision=HIGHEST)
    return jnp.dot(a, b, preferred_element_type=F32)


def _dot_nt(a, b):
    return lax.dot_general(a, b, (((1,), (1,)), ((), ())), preferred_element_type=F32)


def _split(x):
    hi = x.astype(BF16)
    lo = (x - hi.astype(F32)).astype(BF16)
    return hi, lo


def _dot_nt3(a, b):
    ahi, alo = a
    bhi, blo = b
    return _dot_nt(ahi, bhi) + _dot_nt(ahi, blo) + _dot_nt(alo, bhi)


def _orderable(s):
    b = lax.bitcast_convert_type(s, jnp.int32)
    return jnp.where(b < 0, b ^ jnp.int32(0x7FFFFFFF), b)


def _layer_norm(v, g, b):
    mu = jnp.mean(v, axis=-1, keepdims=True)
    d = v - mu
    var = jnp.mean(d * d, axis=-1, keepdims=True)
    return d * lax.rsqrt(var + LN_EPS) * g + b


def _mm_kernel(x_ref, w_ref, o_ref, *, precise):
    if precise:
        o_ref[...] = _dot(x_ref[...], w_ref[...], True)
    else:
        o_ref[...] = _dot(x_ref[...].astype(BF16), w_ref[...])


def matmul(x, w, tm, tn, precise=False):
    M, K = x.shape
    N = w.shape[1]
    assert M % tm == 0 and N % tn == 0
    return pl.pallas_call(
        functools.partial(_mm_kernel, precise=precise),
        grid=(N // tn, M // tm),
        in_specs=[pl.BlockSpec((tm, K), lambda j, i: (i, 0)),
                  pl.BlockSpec((K, tn), lambda j, i: (0, j))],
        out_specs=pl.BlockSpec((tm, tn), lambda j, i: (i, j)),
        out_shape=jax.ShapeDtypeStruct((M, N), F32),
        compiler_params=_cparams(("arbitrary", "arbitrary")),
    )(x, w)


def _flash(q, k_tile, v_tile, bias_tile, nk, dv):
    tq = q.shape[0]

    def body(kj, carry):
        m, l, acc = carry
        s = _dot_nt(q, k_tile(kj)) + bias_tile(kj)
        m_new = jnp.maximum(m, jnp.max(s, axis=-1, keepdims=True))
        alpha = jnp.exp(m - m_new)
        p = jnp.exp(s - m_new)
        l = alpha * l + jnp.sum(p, axis=-1, keepdims=True)
        acc = alpha * acc + _dot(p.astype(BF16), v_tile(kj))
        return m_new, l, acc

    m0 = jnp.full((tq, 1), NEG, F32)
    l0 = jnp.zeros((tq, 1), F32)
    a0 = jnp.zeros((tq, dv), F32)
    _, l, acc = lax.fori_loop(0, nk, body, (m0, l0, a0))
    return acc / l


def _attn_a_kernel(q_ref, k_ref, v_ref, iq_ref, ik_ref, iw_ref, band_ref, tri_ref,
                   o_ref, s_ref, am_ref, *, topk):
    i = pl.program_id(1)
    nk = i + 1
    tq = q_ref.shape[2]
    rows = 64

    iq = [_split(iq_ref[0, h] * (D_IDX ** -0.5)) for h in range(H_IDX)]
    wi = iw_ref[0] * (H_IDX ** -0.5)
    qpos = i * tq + lax.broadcasted_iota(jnp.int32, (tq, tq), 0)

    def score_body(kj, _):
        ikt = _split(ik_ref[0, pl.ds(pl.multiple_of(kj * tq, tq), tq), :])
        s = jnp.zeros((tq, tq), F32)
        for h in range(H_IDX):
            d = _dot_nt3(iq[h], ikt)
            s = s + wi[:, h:h + 1] * jnp.maximum(d, 0.0)
        kpos = kj * tq + lax.broadcasted_iota(jnp.int32, (tq, tq), 1)
        s = jnp.where(kpos <= qpos, s + 0.0, -jnp.inf)
        s_ref[kj] = _orderable(s)
        return 0

    lax.fori_loop(0, nk, score_body, 0)

    tri = tri_ref[...]
    for rc in range(tq // rows):
        rsl = pl.ds(rc * rows, rows)

        def count(pred_fn):
            def body(kj, c):
                t = jnp.where(pred_fn(s_ref[kj, rsl, :]), 1.0, 0.0)
                return c + t[:, :128] + t[:, 128:]
            c = lax.fori_loop(0, nk, body, jnp.zeros((rows, 128), F32))
            return jnp.sum(c, axis=-1, keepdims=True)

        def bit_body(b, t):
            bit = jnp.left_shift(jnp.int32(1), 30 - b)
            cand = t | bit
            c = count(lambda s: s >= cand)
            return jnp.where(c >= topk, cand, t)

        t0 = jnp.full((rows, 1), INT_MIN, jnp.int32)
        c0 = count(lambda s: s >= 0)
        t0 = jnp.where(c0 >= topk, jnp.zeros_like(t0), t0)
        thr = lax.fori_loop(0, 31, bit_body, t0)
        c_gt = count(lambda s: s > thr)
        need = jnp.where(thr > KEY_NEG_INF, topk - c_gt, 0.0)

        def mask_body(kj, run):
            s = s_ref[kj, rsl, :]
            tie = s == thr
            tie_f = jnp.where(tie, 1.0, 0.0)
            pre = _dot(tie_f.astype(BF16), tri) + run
            keep_tie = jnp.where(pre < need, 0.0, NEG)
            am = jnp.where(s > thr, 0.0, jnp.where(tie, keep_tie, NEG))
            am_ref[kj, rsl, :] = am
            return run + jnp.sum(tie_f, axis=-1, keepdims=True)

        lax.fori_loop(0, nk, mask_body, jnp.zeros((rows, 1), F32))

    for h in range(H_A):
        q = q_ref[0, h]

        def k_tile(kj, h=h):
            return k_ref[0, h, pl.ds(pl.multiple_of(kj * tq, tq), tq), :]

        def v_tile(kj, h=h):
            return v_ref[0, h, pl.ds(pl.multiple_of(kj * tq, tq), tq), :]

        def bias_tile(kj, h=h):
            return band_ref[h, jnp.minimum(i - kj, 2)] + am_ref[kj]

        o_ref[0, :, h * D_A:(h + 1) * D_A] = _flash(q, k_tile, v_tile, bias_tile, nk, D_A)


def attn_a_prompt(q, k, v, iq, ik, iw, band, tri, topk):
    B, H, T, D = q.shape
    nq = T // TQ
    return pl.pallas_call(
        functools.partial(_attn_a_kernel, topk=float(topk)),
        grid=(B, nq),
        in_specs=[
            pl.BlockSpec((1, H, TQ, D), lambda b, i: (b, 0, i, 0)),
            pl.BlockSpec((1, H, T, D), lambda b, i: (b, 0, 0, 0)),
            pl.BlockSpec((1, H, T, D), lambda b, i: (b, 0, 0, 0)),
            pl.BlockSpec((1, H_IDX, TQ, D_IDX), lambda b, i: (b, 0, i, 0)),
            pl.BlockSpec((1, T, D_IDX), lambda b, i: (b, 0, 0)),
            pl.BlockSpec((1, TQ, H_IDX), lambda b, i: (b, i, 0)),
            pl.BlockSpec((H, 3, TQ, TQ), lambda b, i: (0, 0, 0, 0)),
            pl.BlockSpec((TQ, TQ), lambda b, i: (0, 0)),
        ],
        out_specs=pl.BlockSpec((1, TQ, H * D), lambda b, i: (b, i, 0)),
        out_shape=jax.ShapeDtypeStruct((B, T, H * D), F32),
        scratch_shapes=[pltpu.VMEM((nq, TQ, TQ), jnp.int32),
                        pltpu.VMEM((nq, TQ, TQ), F32)],
        compiler_params=_cparams(("arbitrary", "arbitrary")),
    )(q, k, v, iq, ik, iw, band, tri)


def _attn_b_kernel(q_ref, k_ref, v_ref, band_ref, lam_ref, g_ref, o_ref, *, lam_init):
    i = pl.program_id(1)
    nk = i + 1
    tq = q_ref.shape[2]
    lv = lam_ref[...]
    lam = (jnp.exp(jnp.sum(lv[0:1] * lv[1:2], axis=-1, keepdims=True))
           - jnp.exp(jnp.sum(lv[2:3] * lv[3:4], axis=-1, keepdims=True)) + lam_init)
    g = g_ref[...]
    for h in range(H_B):
        outs = []
        for c in range(2):
            gidx = 2 * h + c
            q = q_ref[0, gidx]

            def k_tile(kj, gidx=gidx):
                return k_ref[0, gidx, pl.ds(pl.multiple_of(kj * tq, tq), tq), :]

            def v_tile(kj, h=h):
                return v_ref[0, h, pl.ds(pl.multiple_of(kj * tq, tq), tq), :]

            def bias_tile(kj, h=h):
                return band_ref[h, jnp.minimum(i - kj, 2)]

            outs.append(_flash(q, k_tile, v_tile, bias_tile, nk, 2 * D_B))
        o = outs[0] - lam * outs[1]
        o = o * lax.rsqrt(jnp.mean(o * o, axis=-1, keepdims=True) + SUBLN_EPS)
        o_ref[0, :, h * 2 * D_B:(h + 1) * 2 * D_B] = o * g * (1.0 - lam_init)


def attn_b_prompt(q, k, v, band, lamv, g, lam_init):
    B, G, T, D = q.shape
    nq = T // TQ
    return pl.pallas_call(
        functools.partial(_attn_b_kernel, lam_init=lam_init),
        grid=(B, nq),
        in_specs=[
            pl.BlockSpec((1, G, TQ, D), lambda b, i: (b, 0, i, 0)),
            pl.BlockSpec((1, G, T, D), lambda b, i: (b, 0, 0, 0)),
            pl.BlockSpec((1, H_B, T, 2 * D_B), lambda b, i: (b, 0, 0, 0)),
            pl.BlockSpec((H_B, 3, TQ, TQ), lambda b, i: (0, 0, 0, 0)),
            pl.BlockSpec((4, D_B), lambda b, i: (0, 0)),
            pl.BlockSpec((1, 2 * D_B), lambda b, i: (0, 0)),
        ],
        out_specs=pl.BlockSpec((1, TQ, B_WIDTH), lambda b, i: (b, i, 0)),
        out_shape=jax.ShapeDtypeStruct((B, T, B_WIDTH), F32),
        compiler_params=_cparams(("arbitrary", "arbitrary")),
    )(q, k, v, band, lamv, g)


def _attn_m_kernel(q_ref, k_ref, v_ref, o_ref):
    for h in range(H_M):
        s = _dot_nt(q_ref[0, h], k_ref[0, h])
        m = jnp.max(s, axis=-1, keepdims=True)
        p = jnp.exp(s - m)
        l = jnp.sum(p, axis=-1, keepdims=True)
        o_ref[0, :, h * D_M:(h + 1) * D_M] = _dot(p.astype(BF16), v_ref[0, h]) / l


def attn_m_prompt(q, k, v):
    B, H, T, D = q.shape
    NM = k.shape[2]
    tq = 512
    return pl.pallas_call(
        _attn_m_kernel,
        grid=(B, T // tq),
        in_specs=[
            pl.BlockSpec((1, H, tq, D), lambda b, i: (b, 0, i, 0)),
            pl.BlockSpec((1, H, NM, D), lambda b, i: (b, 0, 0, 0)),
            pl.BlockSpec((1, H, NM, D), lambda b, i: (b, 0, 0, 0)),
        ],
        out_specs=pl.BlockSpec((1, tq, M_WIDTH), lambda b, i: (b, i, 0)),
        out_shape=jax.ShapeDtypeStruct((B, T, M_WIDTH), F32),
        compiler_params=_cparams(("arbitrary", "arbitrary")),
    )(q, k, v)


def _merge_kernel(x_ref, oa_ref, ob_ref, om_ref, g0_ref, g1_ref, g2_ref,
                  wa_ref, wb_ref, wm_ref, wo_ref, lg_ref, lb_ref, o_ref, *, alpha, precise):
    def prep(v):
        return v if precise else v.astype(BF16)

    ya = _dot(prep(oa_ref[...]), wa_ref[...], precise)
    yb = _dot(prep(ob_ref[...]), wb_ref[...], precise)
    ym = _dot(prep(om_ref[...]), wm_ref[...], precise)
    hmix = (jax.nn.sigmoid(g0_ref[...]) * ya + jax.nn.sigmoid(g1_ref[...]) * yb
            + jax.nn.sigmoid(g2_ref[...]) * ym)
    mix = _dot(prep(hmix), wo_ref[...], precise)
    o_ref[...] = _layer_norm(alpha * x_ref[...] + mix, lg_ref[...], lb_ref[...])


def merge(x, oa, ob, om, hproj, gate_col0, wa, wb, wm, wo, lg, lb, alpha, tm, precise=False):
    M, D = x.shape
    row = lambda i: (i, 0)
    const = lambda i: (0, 0)
    gspec = [pl.BlockSpec((tm, D), (lambda i, c=c: (i, gate_col0 + c))) for c in range(3)]
    return pl.pallas_call(
        functools.partial(_merge_kernel, alpha=alpha, precise=precise),
        grid=(M // tm,),
        in_specs=[pl.BlockSpec((tm, D), row),
                  pl.BlockSpec((tm, HEAD_COLS), row), pl.BlockSpec((tm, HEAD_COLS), row),
                  pl.BlockSpec((tm, HEAD_COLS), row)] + gspec + [
                  pl.BlockSpec((HEAD_COLS, D), const), pl.BlockSpec((HEAD_COLS, D), const),
                  pl.BlockSpec((HEAD_COLS, D), const), pl.BlockSpec((D, D), const),
                  pl.BlockSpec((1, D), const), pl.BlockSpec((1, D), const)],
        out_specs=pl.BlockSpec((tm, D), row),
        out_shape=jax.ShapeDtypeStruct((M, D), F32),
        compiler_params=_cparams(("arbitrary",)),
    )(x, oa, ob, om, hproj, hproj, hproj, wa, wb, wm, wo, lg, lb)


def _ffn_kernel(x_ref, rw_ref, wg_ref, wu_ref, wd_ref, lg_ref, lb_ref, o_ref,
                acc_ref, gate_ref, *, alpha, routed, precise, n_experts):
    e = pl.program_id(1)
    f = pl.program_id(2)
    first = jnp.logical_and(e == 0, f == 0)
    last = jnp.logical_and(e == pl.num_programs(1) - 1, f == pl.num_programs(2) - 1)
    x = x_ref[...]

    @pl.when(first)
    def _():
        acc_ref[...] = jnp.zeros_like(acc_ref)
        if routed:
            logits = _dot(x, rw_ref[...], True)
            ne = logits.shape[-1]
            lane = lax.broadcasted_iota(jnp.int32, logits.shape, 1)
            logits = jnp.where(lane < n_experts, logits, -jnp.inf)
            m1 = jnp.max(logits, axis=-1, keepdims=True)
            i1 = jnp.min(jnp.where(logits == m1, lane, ne), axis=-1, keepdims=True)
            rest = jnp.where(lane == i1, -jnp.inf, logits)
            m2 = jnp.max(rest, axis=-1, keepdims=True)
            i2 = jnp.min(jnp.where(rest == m2, lane, ne), axis=-1, keepdims=True)
            e2 = jnp.exp(m2 - m1)
            w1 = 1.0 / (1.0 + e2)
            w2 = e2 / (1.0 + e2)
            gate_ref[...] = jnp.where(lane == i1, w1, 0.0) + jnp.where(lane == i2, w2, 0.0)

    xin = x if precise else x.astype(BF16)
    hg = _dot(xin, wg_ref[0], precise)
    hu = _dot(xin, wu_ref[0], precise)
    hh = hg * jax.nn.sigmoid(hg) * hu
    if routed:
        gate = gate_ref[...]
        lane = lax.broadcasted_iota(jnp.int32, gate.shape, 1)
        ge = jnp.sum(jnp.where(lane == e, gate, 0.0), axis=-1, keepdims=True)
        hh = hh * ge
    acc_ref[...] += _dot(hh if precise else hh.astype(BF16), wd_ref[0], precise)

    @pl.when(last)
    def _():
        o_ref[...] = _layer_norm(alpha * x + acc_ref[...], lg_ref[...], lb_ref[...])


def ffn(x, rw, wg, wu, wd, lg, lb, alpha, tm, tf, routed, precise=False):
    M, D = x.shape
    E, _, F = wg.shape
    assert M % tm == 0 and F % tf == 0
    return pl.pallas_call(
        functools.partial(_ffn_kernel, alpha=alpha, routed=routed, precise=precise, n_experts=E),
        grid=(M // tm, E, F // tf),
        in_specs=[pl.BlockSpec((tm, D), lambda i, e, f: (i, 0)),
                  pl.BlockSpec(rw.shape, lambda i, e, f: (0, 0)),
                  pl.BlockSpec((1, D, tf), lambda i, e, f: (e, 0, f)),
                  pl.BlockSpec((1, D, tf), lambda i, e, f: (e, 0, f)),
                  pl.BlockSpec((1, tf, D), lambda i, e, f: (e, f, 0)),
                  pl.BlockSpec((1, D), lambda i, e, f: (0, 0)),
                  pl.BlockSpec((1, D), lambda i, e, f: (0, 0))],
        out_specs=pl.BlockSpec((tm, D), lambda i, e, f: (i, 0)),
        out_shape=jax.ShapeDtypeStruct((M, D), F32),
        scratch_shapes=[pltpu.VMEM((tm, D), F32), pltpu.VMEM((tm, rw.shape[1]), F32)],
        compiler_params=_cparams(("arbitrary", "arbitrary", "arbitrary")),
    )(x, rw, wg, wu, wd, lg, lb)


IDX_PAGES_PER_STEP = 16


def _idx_s_kernel(pt_ref, iq_ref, iw_ref, iknew_ref, tri_ref, low_ref, *refs, topk):
    P = IDX_PAGES_PER_STEP
    pages = refs[:P]
    am_ref, amnew_ref, s_ref = refs[P:]
    step = pl.program_id(1)
    npg = s_ref.shape[0]
    iq = iq_ref[0] * (D_IDX ** -0.5)
    iqs = _split(iq)
    wi = iw_ref[0] * (H_IDX ** -0.5)
    for j in range(P):
        d = _dot_nt3(iqs, _split(pages[j][0]))
        sc = jnp.sum(wi * jnp.maximum(d, 0.0), axis=0, keepdims=True) + 0.0
        s_ref[pl.ds(step * P + j, 1), :] = _orderable(sc)

    @pl.when(step == pl.num_programs(1) - 1)
    def _():
        s = s_ref[...]
        dn = jnp.sum(iq * iknew_ref[0], axis=-1, keepdims=True)
        sn = jnp.sum(wi * jnp.maximum(dn, 0.0), axis=0, keepdims=True) + 0.0
        kn = _orderable(sn)

        def total(x):
            return jnp.sum(jnp.sum(x, axis=-1, keepdims=True), axis=0, keepdims=True)

        def count_ge(cand):
            return total(jnp.where(s >= cand, 1.0, 0.0)) + jnp.where(kn >= cand, 1.0, 0.0)

        def bit_body(b, t):
            cand = t | jnp.left_shift(jnp.int32(1), 30 - b)
            return jnp.where(count_ge(cand) >= topk, cand, t)

        t0 = jnp.full((1, 1), INT_MIN, jnp.int32)
        t0 = jnp.where(count_ge(jnp.zeros_like(t0)) >= topk, jnp.zeros_like(t0), t0)
        thr = lax.fori_loop(0, 31, bit_body, t0)
        c_gt = total(jnp.where(s > thr, 1.0, 0.0)) + jnp.where(kn > thr, 1.0, 0.0)
        need = topk - c_gt
        tie = s == thr
        tie_f = jnp.where(tie, 1.0, 0.0)
        in_row = _dot(tie_f.astype(BF16), tri_ref[...])
        row_tot = jnp.broadcast_to(jnp.sum(tie_f, axis=-1, keepdims=True), tie_f.shape)
        before = _dot(low_ref[...], row_tot.astype(BF16))
        pre = in_row + before
        keep_tie = jnp.where(pre < need, 0.0, NEG)
        am_ref[0] = jnp.where(s > thr, 0.0, jnp.where(tie, keep_tie, NEG))
        n_ties = total(tie_f)
        keep_new = jnp.where(n_ties < need, 0.0, NEG)
        amn = jnp.where(kn > thr, 0.0, jnp.where(kn == thr, keep_new, NEG))
        amnew_ref[0] = jnp.broadcast_to(amn, amnew_ref.shape[1:])


def idx_sample(page_table, iq, iw, ik_new, pool, topk):
    DB, npg = page_table.shape
    P = IDX_PAGES_PER_STEP
    assert npg % P == 0
    tri = jnp.asarray(np.triu(np.ones((PAGE, PAGE), np.float32), 1), BF16)
    low = jnp.asarray(np.tril(np.ones((npg, npg), np.float32), -1), BF16)
    page_specs = [pl.BlockSpec((1, PAGE, D_IDX), (lambda b, s, pt, j=j: (pt[b, s * P + j], 0, 0)))
                  for j in range(P)]
    grid_spec = pltpu.PrefetchScalarGridSpec(
        num_scalar_prefetch=1,
        grid=(DB, npg // P),
        in_specs=[pl.BlockSpec((1, H_IDX, D_IDX), lambda b, s, pt: (b, 0, 0)),
                  pl.BlockSpec((1, H_IDX, 1), lambda b, s, pt: (b, 0, 0)),
                  pl.BlockSpec((1, 1, D_IDX), lambda b, s, pt: (b, 0, 0)),
                  pl.BlockSpec((PAGE, PAGE), lambda b, s, pt: (0, 0)),
                  pl.BlockSpec((npg, npg), lambda b, s, pt: (0, 0))] + page_specs,
        out_specs=[pl.BlockSpec((1, npg, PAGE), lambda b, s, pt: (b, 0, 0)),
                   pl.BlockSpec((1, 1, PAGE), lambda b, s, pt: (b, 0, 0))],
        scratch_shapes=[pltpu.VMEM((npg, PAGE), jnp.int32)],
    )
    return pl.pallas_call(
        functools.partial(_idx_s_kernel, topk=float(topk)),
        grid_spec=grid_spec,
        out_shape=[jax.ShapeDtypeStruct((DB, npg, PAGE), F32),
                   jax.ShapeDtypeStruct((DB, 1, PAGE), F32)],
        compiler_params=_cparams(("arbitrary", "arbitrary")),
    )(page_table, iq, iw, ik_new, tri, low, *([pool] * P))


def _decode_kernel(pt_ref, q_ref, am_ref, amnew_ref, knew_ref, vnew_ref, blast_ref, bnew_ref,
                   ee_ref, en_ref, lam_ref, g_ref, *refs, P, C, has_new, use_mask, scale, lam_init):
    kp = refs[:P]
    vp = refs[P:2 * P]
    o_ref, m_ref, l_ref, acc_ref = refs[2 * P:]
    step = pl.program_id(1)
    is_last = step == pl.num_programs(1) - 1

    @pl.when(step == 0)
    def _():
        m_ref[...] = jnp.full_like(m_ref, NEG)
        l_ref[...] = jnp.zeros_like(l_ref)
        acc_ref[...] = jnp.zeros_like(acc_ref)

    q = q_ref[0] * scale
    am_t = jnp.transpose(am_ref[0]) if use_mask else None
    last_f = jnp.where(is_last, 1.0, 0.0)
    for j in range(P):
        kq = (kp[j][0] * q).astype(BF16)
        v = vp[j][0]
        am = am_t[:, j:j + 1] if use_mask else 0.0
        for c in range(C):
            lg = _dot(kq, ee_ref[c]) + am
            if j == P - 1:
                lg = lg + last_f * blast_ref[c]
            m_old = m_ref[c]
            m_new = jnp.maximum(m_old, lg)
            alpha = jnp.exp(m_old - m_new)
            p = jnp.exp(lg - m_new)
            m_ref[c] = m_new
            l_ref[c] = l_ref[c] * alpha + p
            acc_ref[c] = acc_ref[c] * alpha + p * v

    @pl.when(is_last)
    def _():
        outs = []
        for c in range(C):
            m = m_ref[c]
            mx = jnp.max(m, axis=0, keepdims=True)
            if has_new:
                kqn = jnp.broadcast_to((knew_ref[0] * q).astype(BF16), (8, q.shape[-1]))
                lgn = _dot(kqn, ee_ref[c])[0:1] + bnew_ref[c] + amnew_ref[0][:, 0:1]
                mx = jnp.maximum(mx, lgn)
            w = jnp.exp(m - mx)
            l = jnp.sum(l_ref[c] * w, axis=0, keepdims=True)
            o = jnp.sum(acc_ref[c] * w, axis=0, keepdims=True)
            if has_new:
                pn = jnp.exp(lgn - mx)
                l = l + pn
                o = o + pn * vnew_ref[0]
            outs.append(o / l)
        if C == 1:
            o_ref[0] = outs[0]
        else:
            lv = lam_ref[...]
            lam = (jnp.exp(jnp.sum(lv[0:1] * lv[1:2], axis=-1, keepdims=True))
                   - jnp.exp(jnp.sum(lv[2:3] * lv[3:4], axis=-1, keepdims=True)) + lam_init)
            o = outs[0] - lam * outs[1]
            o8 = jnp.broadcast_to(o * o, (8, o.shape[-1]))
            ms = _dot(o8, en_ref[...], True)[0:1]
            o_ref[0] = o * lax.rsqrt(ms + SUBLN_EPS) * g_ref[...] * (1.0 - lam_init)


def decode_attn(page_table, q, kpool, vpool, amask, amask_new, knew, vnew, blast, bnew,
                ee, en, lamv, g, *, P, C, has_new, use_mask, scale, lam_init=0.0):
    DB, npg = page_table.shape
    W = q.shape[-1]
    assert npg % P == 0
    kspecs = [pl.BlockSpec((1, PAGE, W), (lambda b, s, pt, j=j: (pt[b, s * P + j], 0, 0)))
              for j in range(P)]
    per_b = lambda b, s, pt: (b, 0, 0)
    c3 = lambda b, s, pt: (0, 0, 0)
    c2 = lambda b, s, pt: (0, 0)
    grid_spec = pltpu.PrefetchScalarGridSpec(
        num_scalar_prefetch=1,
        grid=(DB, npg // P),
        in_specs=[pl.BlockSpec((1, 1, W), per_b),
                  pl.BlockSpec((1, P, PAGE), lambda b, s, pt: (b, s, 0)),
                  pl.BlockSpec((1, 1, PAGE), per_b),
                  pl.BlockSpec((1, 1, W), per_b),
                  pl.BlockSpec((1, 1, W), per_b),
                  pl.BlockSpec((C, PAGE, W), c3),
                  pl.BlockSpec((C, 1, W), c3),
                  pl.BlockSpec((C, W, W), c3),
                  pl.BlockSpec((W, W), c2),
                  pl.BlockSpec((4, D_B), c2),
                  pl.BlockSpec((1, W), c2)] + kspecs + kspecs,
        out_specs=pl.BlockSpec((1, 1, W), per_b),
        scratch_shapes=[pltpu.VMEM((C, PAGE, W), F32), pltpu.VMEM((C, PAGE, W), F32),
                        pltpu.VMEM((C, PAGE, W), F32)],
    )
    return pl.pallas_call(
        functools.partial(_decode_kernel, P=P, C=C, has_new=has_new, use_mask=use_mask, scale=scale,
                          lam_init=lam_init),
        grid_spec=grid_spec,
        out_shape=jax.ShapeDtypeStruct((DB, 1, W), F32),
        compiler_params=_cparams(("arbitrary", "arbitrary")),
    )(page_table, q, amask, amask_new, knew, vnew, blast, bnew, ee, en, lamv, g,
      *([kpool] * P), *([vpool] * P))


def _t5_bucket_np(rel):
    n = np.maximum(rel, 0)
    max_exact = NUM_BUCKETS // 2
    nf = np.maximum(n, 1).astype(np.float32)
    large = max_exact + (np.log(nf / np.float32(max_exact))
                         / np.float32(math.log(MAX_DISTANCE / max_exact))
                         * np.float32(NUM_BUCKETS - max_exact)).astype(np.int32)
    large = np.minimum(large, NUM_BUCKETS - 1)
    return np.where(n < max_exact, n, large).astype(np.int32)


def _far_bucket(first_far, last_far):
    b = _t5_bucket_np(np.arange(first_far, last_far + 1))
    assert (b == b[0]).all(), "relative-position bias must be constant beyond one tile"
    return int(b[0])


def _band_tables(rel_bias, heads, t_max):
    far = _far_bucket(TQ + 1, max(t_max, TQ + 1))
    r = np.arange(TQ)[:, None] - np.arange(TQ)[None, :]
    tabs = []
    for d in range(2):
        rel = r + d * TQ
        vals = rel_bias[_t5_bucket_np(rel)][..., heads] - rel_bias[far][heads]
        vals = jnp.where((rel >= 0)[..., None], vals, NEG)
        tabs.append(jnp.transpose(vals, (2, 0, 1)))
    tabs.append(jnp.zeros_like(tabs[0]))
    return jnp.stack(tabs, axis=1).astype(F32)


def _block_ones(width, src_group, dst_group, dst_of_src):
    s = np.arange(width)[:, None] // src_group
    v = np.arange(width)[None, :] // dst_group
    return (dst_of_src(s, v)).astype(np.float32)


def kernel(x_prompt, x_sample, cache_a_k, cache_a_v, cache_idx_k, cache_b_k, cache_b_v, cache_mem_k, cache_mem_v, page_table, mem_prompt, w_in, w_mem_kv, lambda_q1, lambda_k1, lambda_q2, lambda_k2, subln_g, w_branch_a, w_branch_b, w_branch_m, w_out, rel_bias, ln1_g, ln1_b, ln2_g, ln2_b, ffn_w_gate, ffn_w_up, ffn_w_down, router_w, expert_w_gate, expert_w_up, expert_w_down):
    B, T, D = x_prompt.shape
    DB, TS, _ = x_sample.shape
    depth = w_in.shape[0]
    n_pool = cache_a_k.shape[1]
    npg = page_table.shape[1]
    past = npg * PAGE
    n_mem = mem_prompt.shape[1]
    assert TS == 1 and T % TQ == 0 and cache_a_k.shape[2] == PAGE
    alpha = (2 * depth) ** 0.25
    topk_p = min(TOPK_MAX, T // 4)
    topk_s = min(TOPK_MAX, (past + TS) // 4)
    W = HEAD_COLS

    c_ik = 3 * A_WIDTH + H_IDX * D_IDX
    c_b = c_ik + D_IDX + H_IDX
    n_main = 8 * W + 3 * D
    tail_pad = 128 - D_IDX - H_IDX

    tri = jnp.asarray(np.triu(np.ones((TQ, TQ), np.float32), 1), BF16)
    band_a = _band_tables(rel_bias, np.arange(H_A), T)
    band_b = _band_tables(rel_bias, H_A + np.arange(H_B), T)

    far_s = _far_bucket(PAGE + 1, past)
    rel_last = past - (past - PAGE + np.arange(PAGE))
    b_last = rel_bias[_t5_bucket_np(rel_last)] - rel_bias[far_s]
    b_new = (rel_bias[0] - rel_bias[far_s])[None, :]
    exp_a = lambda t: jnp.repeat(t[:, :H_A], D_A, axis=1)
    exp_b = lambda t: jnp.repeat(t[:, H_A:], 2 * D_B, axis=1)
    blast_a, bnew_a = exp_a(b_last)[None], exp_a(b_new)[None]
    blast_b = jnp.stack([exp_b(b_last)] * 2)
    bnew_b = jnp.stack([exp_b(b_new)] * 2)
    ee_a = jnp.asarray(_block_ones(W, D_A, D_A, lambda s, v: s == v), BF16)[None]
    ee_b = jnp.stack([jnp.asarray(_block_ones(W, D_B, 2 * D_B, lambda s, v, c=c: s == 2 * v + c), BF16)
                      for c in range(2)])
    ee_m = jnp.asarray(_block_ones(W, D_M, D_M, lambda s, v: s == v), BF16)[None]
    en_b = jnp.asarray(_block_ones(W, 2 * D_B, 2 * D_B, lambda s, v: s == v) / (2 * D_B), F32)
    zeros_am = jnp.zeros((DB, npg, PAGE), F32)
    zeros_amn = jnp.zeros((DB, 1, PAGE), F32)
    zeros_row = jnp.zeros((DB, 1, W), F32)
    zeros_bl = jnp.zeros((1, PAGE, W), F32)
    zeros_bn = jnp.zeros((1, 1, W), F32)
    mem_pages = n_mem // PAGE
    mem_table = jnp.arange(DB * mem_pages, dtype=jnp.int32).reshape(DB, mem_pages)

    outs = {n: [] for n in ('akp', 'avp', 'ikp', 'bkp', 'bvp', 'mkp', 'mvp',
                            'aks', 'avs', 'iks', 'bks', 'bvs')}
    x_p = x_prompt.reshape(B * T, D)
    x_s = x_sample.reshape(DB, D)
    for l in range(depth):
        lam_init = 0.8 - 0.6 * math.exp(-0.3 * l)
        lamv = jnp.stack([lambda_q1[l], lambda_k1[l], lambda_q2[l], lambda_k2[l]])
        wl = w_in[l]
        w_perm = jnp.concatenate([wl[:, :c_ik], wl[:, c_b:], wl[:, c_ik:c_b],
                                  jnp.zeros((D, tail_pad), F32)], axis=1)
        w_main_bf = w_perm[:, :n_main].astype(BF16)
        w_tail_bf = w_perm[:, n_main:].astype(BF16)
        wa_l, wb_l, wm_l, wo_l = w_branch_a[l], w_branch_b[l], w_branch_m[l], w_out[l]
        lg1, lb1 = ln1_g[l][None], ln1_b[l][None]
        lg2, lb2 = ln2_g[l][None], ln2_b[l][None]
        g_sub = subln_g[l][None]
        if l % 2 == 0:
            j = l // 2
            rw = jnp.zeros((D, 128), F32)
            wg, wu, wd = ffn_w_gate[j][None], ffn_w_up[j][None], ffn_w_down[j][None]
            routed = False
        else:
            j = l // 2
            rw = jnp.pad(router_w[j], ((0, 0), (0, 128 - N_EXPERTS)))
            wg, wu, wd = expert_w_gate[j], expert_w_up[j], expert_w_down[j]
            routed = True

        h = matmul(x_p, w_main_bf, 512, 1024)
        tail = matmul(x_p, w_tail_bf, 512, 128)
        h3 = h.reshape(B, T, n_main)

        def heads(c0, nh, dh, dt):
            return h3[:, :, c0:c0 + nh * dh].reshape(B, T, nh, dh).transpose(0, 2, 1, 3).astype(dt)

        a_k = h3[:, :, W:2 * W]
        a_v = h3[:, :, 2 * W:3 * W]
        b_k = h3[:, :, 5 * W:6 * W]
        b_v = h3[:, :, 6 * W:7 * W]
        i_k = tail[:, :D_IDX].reshape(B, T, D_IDX)
        i_w = tail[:, D_IDX:D_IDX + H_IDX].reshape(B, T, H_IDX)
        qa = (heads(0, H_A, D_A, F32) * (D_A ** -0.5)).astype(BF16)
        o_a = attn_a_prompt(qa, heads(W, H_A, D_A, BF16), heads(2 * W, H_A, D_A, BF16),
                            heads(3 * W, H_IDX, D_IDX, F32), i_k, i_w, band_a, tri, topk_p)
        qb = (heads(4 * W, 2 * H_B, D_B, F32) * (D_B ** -0.5)).astype(BF16)
        o_b = attn_b_prompt(qb, heads(5 * W, 2 * H_B, D_B, BF16), heads(6 * W, H_B, 2 * D_B, BF16),
                            band_b, lamv, g_sub, lam_init)
        kv = matmul(mem_prompt.reshape(B * n_mem, D), w_mem_kv[l].astype(BF16), 512, 1024)
        mk = kv[:, :M_WIDTH].reshape(B, n_mem, H_M, D_M)
        mv = kv[:, M_WIDTH:].reshape(B, n_mem, H_M, D_M)
        qm = (heads(7 * W, H_M, D_M, F32) * (D_M ** -0.5)).astype(BF16)
        o_m = attn_m_prompt(qm, mk.transpose(0, 2, 1, 3).astype(BF16),
                            mv.transpose(0, 2, 1, 3).astype(BF16))
        x_p = merge(x_p, o_a.reshape(B * T, W), o_b.reshape(B * T, W), o_m.reshape(B * T, W),
                    h, 8 * W // D, wa_l.astype(BF16), wb_l.astype(BF16), wm_l.astype(BF16),
                    wo_l.astype(BF16), lg1, lb1, alpha, 512)
        x_p = ffn(x_p, rw, wg.astype(BF16), wu.astype(BF16), wd.astype(BF16), lg2, lb2,
                  alpha, 1024 if B * T % 1024 == 0 else 512, 256, routed)
        outs['akp'].append(a_k.reshape(B, T, H_A, D_A))
        outs['avp'].append(a_v.reshape(B, T, H_A, D_A))
        outs['ikp'].append(i_k)
        outs['bkp'].append(b_k.reshape(B, T, H_B, 2, D_B))
        outs['bvp'].append(b_v.reshape(B, T, H_B, 2 * D_B))
        outs['mkp'].append(mk)
        outs['mvp'].append(mv)

        hs = matmul(x_s, w_perm[:, :n_main], DB, 1024, precise=True)
        ts = matmul(x_s, w_perm[:, n_main:], DB, 128, precise=True)
        sa_q, sa_k, sa_v, s_iq = (hs[:, c * W:(c + 1) * W] for c in range(4))
        sb_q, sb_k, sb_v, sm_q = (hs[:, c * W:(c + 1) * W] for c in range(4, 8))
        s_ik = ts[:, :D_IDX]
        s_iw = ts[:, D_IDX:D_IDX + H_IDX]
        am, amn = idx_sample(page_table, s_iq.reshape(DB, H_IDX, D_IDX), s_iw.reshape(DB, H_IDX, 1),
                             s_ik.reshape(DB, 1, D_IDX), cache_idx_k[l], topk_s)
        r3 = lambda t: t.reshape(DB, 1, W)
        o_a = decode_attn(page_table, r3(sa_q), cache_a_k[l].reshape(n_pool, PAGE, W),
                          cache_a_v[l].reshape(n_pool, PAGE, W), am, amn, r3(sa_k), r3(sa_v),
                          blast_a, bnew_a, ee_a, en_b, lamv, g_sub_w(g_sub), P=8, C=1,
                          has_new=True, use_mask=True, scale=D_A ** -0.5)
        o_b = decode_attn(page_table, r3(sb_q), cache_b_k[l].reshape(n_pool, PAGE, W),
                          cache_b_v[l].reshape(n_pool, PAGE, W), zeros_am, zeros_amn, r3(sb_k),
                          r3(sb_v), blast_b, bnew_b, ee_b, en_b, lamv, g_sub_w(g_sub), P=8, C=2,
                          has_new=True, use_mask=False, scale=D_B ** -0.5, lam_init=lam_init)
        o_m = decode_attn(mem_table, r3(sm_q), cache_mem_k[l].reshape(DB * mem_pages, PAGE, W),
                          cache_mem_v[l].reshape(DB * mem_pages, PAGE, W),
                          zeros_am[:, :mem_pages], zeros_amn, zeros_row, zeros_row,
                          zeros_bl, zeros_bn, ee_m, en_b, lamv, g_sub_w(g_sub), P=mem_pages, C=1,
                          has_new=False, use_mask=False, scale=D_M ** -0.5)
        x_s = merge(x_s, o_a.reshape(DB, W), o_b.reshape(DB, W), o_m.reshape(DB, W),
                    hs, 8 * W // D, wa_l, wb_l, wm_l, wo_l, lg1, lb1, alpha, DB, precise=True)
        x_s = ffn(x_s, rw, wg, wu, wd, lg2, lb2, alpha, DB, 256, routed, precise=True)
        outs['aks'].append(sa_k.reshape(DB, TS, H_A, D_A))
        outs['avs'].append(sa_v.reshape(DB, TS, H_A, D_A))
        outs['iks'].append(s_ik.reshape(DB, TS, D_IDX))
        outs['bks'].append(sb_k.reshape(DB, TS, H_B, 2, D_B))
        outs['bvs'].append(sb_v.reshape(DB, TS, H_B, 2 * D_B))

    st = {n: jnp.stack(v) for n, v in outs.items()}
    return (x_p.reshape(B, T, D), x_s.reshape(DB, TS, D),
            st['akp'], st['avp'], st['ikp'], st['bkp'], st['bvp'], st['mkp'], st['mvp'],
            st['aks'], st['avs'], st['iks'], st['bks'], st['bvs'])


def g_sub_w(g_sub):
    return jnp.tile(g_sub, (1, H_B))
```

```python
import functools
import math

import jax
import jax.numpy as jnp
import numpy as np
from jax import lax
from jax.experimental import pallas as pl
from jax.experimental.pallas import tpu as pltpu

F32 = jnp.float32
BF16 = jnp.bfloat16
HIGHEST = lax.Precision.HIGHEST

H_A, D_A = 8, 64
H_IDX, D_IDX = 8, 64
TOPK_MAX = 256
H_B, D_B = 4, 64
H_M, D_M = 4, 128
NUM_BUCKETS, MAX_DISTANCE = 32, 128
N_EXPERTS = 8
LN_EPS = 1e-5
SUBLN_EPS = 1e-5
A_WIDTH = H_A * D_A
B_WIDTH = H_B * 2 * D_B
M_WIDTH = H_M * D_M
HEAD_COLS = 512
PAGE = 128
N_GROUPS = 8

NEG = -1e30
KEY_NEG_INF = -2139095041
INT_MIN = -2147483648

VMEM_LIMIT = 56 * 1024 * 1024

TQ = 256
IDX_SPLIT = 3 * D_IDX


def _cparams(sem):
    return pltpu.CompilerParams(dimension_semantics=sem, vmem_limit_bytes=VMEM_LIMIT)


def _dot(a, b, precise=False):
    if precise:
        return jnp.dot(a, b, preferred_element_type=F32, precision=HIGHEST)
    return jnp.dot(a, b, preferred_element_type=F32)


def _dot_nt(a, b):
    return lax.dot_general(a, b, (((1,), (1,)), ((), ())), preferred_element_type=F32)


def _orderable(s):
    b = lax.bitcast_convert_type(s, jnp.int32)
    return jnp.where(b < 0, b ^ jnp.int32(0x7FFFFFFF), b)


def _layer_norm(v, g, b):
    mu = jnp.mean(v, axis=-1, keepdims=True)
    d = v - mu
    var = jnp.mean(d * d, axis=-1, keepdims=True)
    return d * lax.rsqrt(var + LN_EPS) * g + b


def _kth_largest(count_ge, shape, topk):
    t0 = jnp.full(shape, INT_MIN, jnp.int32)
    t0 = jnp.where(count_ge(jnp.zeros(shape, jnp.int32)) >= topk, jnp.zeros(shape, jnp.int32), t0)

    def bit_body(b, t):
        cand = t | jnp.left_shift(jnp.int32(1), 30 - b)
        return jnp.where(count_ge(cand) >= topk, cand, t)

    return lax.fori_loop(0, 31, bit_body, t0)


def _lambda(lv, lam_init):
    return (jnp.exp(jnp.sum(lv[0:1] * lv[1:2], axis=-1, keepdims=True))
            - jnp.exp(jnp.sum(lv[2:3] * lv[3:4], axis=-1, keepdims=True)) + lam_init)


def _mm_kernel(x_ref, w_ref, o_ref, *, precise):
    if precise:
        o_ref[...] = _dot(x_ref[...], w_ref[...], True)
    else:
        o_ref[...] = _dot(x_ref[...].astype(BF16), w_ref[...])


def matmul(x, w, tm, tn, precise=False, name="matmul"):
    M, K = x.shape
    N = w.shape[1]
    assert M % tm == 0 and N % tn == 0
    return pl.pallas_call(
        functools.partial(_mm_kernel, precise=precise),
        grid=(N // tn, M // tm),
        in_specs=[pl.BlockSpec((tm, K), lambda j, i: (i, 0)),
                  pl.BlockSpec((K, tn), lambda j, i: (0, j))],
        out_specs=pl.BlockSpec((tm, tn), lambda j, i: (i, j)),
        out_shape=jax.ShapeDtypeStruct((M, N), F32),
        compiler_params=_cparams(("arbitrary", "arbitrary")),
        name=name,
    )(x, w)


def _set_row(full, g, row):
    r = lax.broadcasted_iota(jnp.int32, full.shape, 0)
    return jnp.where(r == g, row, full)


def _attend_tile(kj, band_d, ml, *, n_groups, dv, qT_ref, k_ref, vT_ref, band_ref, am, acc_ref,
                 s_ref, p_ref, tq):
    m_all, l_all = ml
    ksl = pl.ds(pl.multiple_of(kj * tq, tq), tq)
    gpb = n_groups // band_ref.shape[0]
    gpv = (n_groups * dv) // vT_ref.shape[2]
    dq = qT_ref.shape[1] // n_groups
    m_out, l_out = m_all, l_all
    for g in range(n_groups):
        s_ref[g] = _dot(k_ref[0, g, ksl, :], qT_ref[0, g * dq:(g + 1) * dq, :])
    alphas = []
    for g in range(n_groups):
        s = s_ref[g]
        if am is not None:
            s = s + am
        if band_d is not None:
            s = s + band_ref[g // gpb, band_d]
        m_old = m_all[g:g + 1, :]
        m_new = jnp.maximum(m_old, jnp.max(s, axis=0, keepdims=True))
        alpha = jnp.exp(m_old - m_new)
        p = jnp.exp(s - m_new)
        p_ref[g] = p.astype(BF16)
        alphas.append(alpha)
        m_out = _set_row(m_out, g, m_new)
        l_out = _set_row(l_out, g, alpha * l_all[g:g + 1, :] + jnp.sum(p, axis=0, keepdims=True))
    for g in range(n_groups):
        vrows = slice((g // gpv) * dv, (g // gpv + 1) * dv)
        arows = slice(g * dv, (g + 1) * dv)
        acc_ref[arows, :] = alphas[g] * acc_ref[arows, :] + _dot(vT_ref[0, kj, vrows, :], p_ref[g])
    return m_out, l_out


def _attend_causal(i, tile_fn, n_groups, tq):
    ml = (jnp.full((n_groups, tq), NEG, F32), jnp.zeros((n_groups, tq), F32))
    ml = lax.fori_loop(0, jnp.maximum(i - 1, 0), lambda kj, c: tile_fn(kj, None, c), ml)
    ml = lax.cond(i >= 1, lambda c: tile_fn(i - 1, 1, c), lambda c: c, ml)
    return tile_fn(i, 0, ml)


def _attn_a_kernel(qT_ref, k_ref, vT_ref, iqT_ref, ik_ref, wT_ref, band_ref, low_ref,
                   o_ref, key_ref, am_ref, acc_ref, s_ref, p_ref, *, topk):
    i = pl.program_id(1)
    nk = i + 1
    tq = qT_ref.shape[2]
    qpos = i * tq + lax.broadcasted_iota(jnp.int32, (tq, tq), 1)
    krow = lax.broadcasted_iota(jnp.int32, (tq, tq), 0)

    def score_body(kj, c):
        ikt = ik_ref[0, pl.ds(pl.multiple_of(kj * tq, tq), tq), :]
        s = jnp.zeros((tq, tq), F32)
        for h in range(H_IDX):
            s = s + wT_ref[0, h:h + 1, :] * jnp.maximum(_dot(ikt, iqT_ref[0, h]), 0.0)
        s = jnp.where(kj * tq + krow <= qpos, s, -jnp.inf)
        key_ref[kj] = _orderable(s)
        return c

    lax.fori_loop(0, nk, score_body, 0)

    def count(pred):
        def body(kj, c):
            t = jnp.where(pred(key_ref[kj]), 1.0, 0.0)
            return c + jnp.sum(t.reshape(tq // 8, 8, tq), axis=0)
        c = lax.fori_loop(0, nk, body, jnp.zeros((8, tq), F32))
        return jnp.sum(c, axis=0, keepdims=True)

    thr = _kth_largest(lambda cand: count(lambda s: s >= cand), (1, tq), topk)
    c_gt = count(lambda s: s > thr)
    need = jnp.where(thr > KEY_NEG_INF, topk - c_gt, 0.0)

    def mask_body(kj, run):
        s = key_ref[kj]
        tie = s == thr
        tie_f = jnp.where(tie, 1.0, 0.0)
        pre = _dot(low_ref[...], tie_f.astype(BF16)) + run
        keep_tie = jnp.where(pre < need, 0.0, NEG)
        am_ref[kj] = jnp.where(s > thr, 0.0, jnp.where(tie, keep_tie, NEG))
        return run + jnp.sum(tie_f, axis=0, keepdims=True)

    lax.fori_loop(0, nk, mask_body, jnp.zeros((1, tq), F32))

    acc_ref[...] = jnp.zeros_like(acc_ref)

    def tile_fn(kj, band_d, ml):
        return _attend_tile(kj, band_d, ml, n_groups=H_A, dv=D_A, qT_ref=qT_ref, k_ref=k_ref,
                            vT_ref=vT_ref, band_ref=band_ref, am=am_ref[kj], acc_ref=acc_ref,
                            s_ref=s_ref, p_ref=p_ref, tq=tq)

    _, l_all = _attend_causal(i, tile_fn, H_A, tq)
    inv = 1.0 / l_all
    oT = jnp.concatenate([acc_ref[h * D_A:(h + 1) * D_A, :] * inv[h:h + 1, :] for h in range(H_A)],
                         axis=0)
    o_ref[0] = oT.T


def attn_a_prompt(qT, k, vT, iqT, ik, wT, band, low, topk):
    B, H, T, D = k.shape
    nq = T // TQ
    W = H * D
    return pl.pallas_call(
        functools.partial(_attn_a_kernel, topk=float(topk)),
        grid=(B, nq),
        in_specs=[
            pl.BlockSpec((1, W, TQ), lambda b, i: (b, 0, i)),
            pl.BlockSpec((1, H, T, D), lambda b, i: (b, 0, 0, 0)),
            pl.BlockSpec((1, nq, W, TQ), lambda b, i: (b, 0, 0, 0)),
            pl.BlockSpec((1, H_IDX, IDX_SPLIT, TQ), lambda b, i: (b, 0, 0, i)),
            pl.BlockSpec((1, T, IDX_SPLIT), lambda b, i: (b, 0, 0)),
            pl.BlockSpec((1, H_IDX, TQ), lambda b, i: (b, 0, i)),
            pl.BlockSpec((H, 2, TQ, TQ), lambda b, i: (0, 0, 0, 0)),
            pl.BlockSpec((TQ, TQ), lambda b, i: (0, 0)),
        ],
        out_specs=pl.BlockSpec((1, TQ, W), lambda b, i: (b, i, 0)),
        out_shape=jax.ShapeDtypeStruct((B, T, W), F32),
        scratch_shapes=[pltpu.VMEM((nq, TQ, TQ), jnp.int32),
                        pltpu.VMEM((nq, TQ, TQ), F32),
                        pltpu.VMEM((W, TQ), F32),
                        pltpu.VMEM((H, TQ, TQ), F32), pltpu.VMEM((H, TQ, TQ), BF16)],
        compiler_params=_cparams(("arbitrary", "arbitrary")),
        name="attn_a_prompt",
    )(qT, k, vT, iqT, ik, wT, band, low)


def _attn_b_kernel(qT_ref, k_ref, vT_ref, band_ref, lam_ref, g_ref, o_ref,
                   acc_ref, s_ref, p_ref, *, lam_init):
    i = pl.program_id(1)
    tq = qT_ref.shape[2]
    dv = 2 * D_B
    acc_ref[...] = jnp.zeros_like(acc_ref)

    def tile_fn(kj, band_d, ml):
        return _attend_tile(kj, band_d, ml, n_groups=2 * H_B, dv=dv, qT_ref=qT_ref, k_ref=k_ref,
                            vT_ref=vT_ref, band_ref=band_ref, am=None, acc_ref=acc_ref,
                            s_ref=s_ref, p_ref=p_ref, tq=tq)

    _, l_all = _attend_causal(i, tile_fn, 2 * H_B, tq)
    lam = _lambda(lam_ref[...], lam_init)
    inv = 1.0 / l_all
    parts = []
    for h in range(H_B):
        g0, g1 = 2 * h, 2 * h + 1
        o = (acc_ref[g0 * dv:(g0 + 1) * dv, :] * inv[g0:g0 + 1, :]
             - lam * (acc_ref[g1 * dv:(g1 + 1) * dv, :] * inv[g1:g1 + 1, :]))
        o = o * lax.rsqrt(jnp.mean(o * o, axis=0, keepdims=True) + SUBLN_EPS)
        parts.append(o * g_ref[...] * (1.0 - lam_init))
    o_ref[0] = jnp.concatenate(parts, axis=0).T


def attn_b_prompt(qT, k, vT, band, lamv, gcol, lam_init):
    B, G, T, D = k.shape
    nq = T // TQ
    W = G * D
    return pl.pallas_call(
        functools.partial(_attn_b_kernel, lam_init=lam_init),
        grid=(B, nq),
        in_specs=[
            pl.BlockSpec((1, W, TQ), lambda b, i: (b, 0, i)),
            pl.BlockSpec((1, G, T, D), lambda b, i: (b, 0, 0, 0)),
            pl.BlockSpec((1, nq, W, TQ), lambda b, i: (b, 0, 0, 0)),
            pl.BlockSpec((H_B, 2, TQ, TQ), lambda b, i: (0, 0, 0, 0)),
            pl.BlockSpec((4, D_B), lambda b, i: (0, 0)),
            pl.BlockSpec((2 * D_B, 1), lambda b, i: (0, 0)),
        ],
        out_specs=pl.BlockSpec((1, TQ, W), lambda b, i: (b, i, 0)),
        out_shape=jax.ShapeDtypeStruct((B, T, W), F32),
        scratch_shapes=[pltpu.VMEM((G * 2 * D_B, TQ), F32),
                        pltpu.VMEM((G, TQ, TQ), F32), pltpu.VMEM((G, TQ, TQ), BF16)],
        compiler_params=_cparams(("arbitrary", "arbitrary")),
        name="attn_b_prompt",
    )(qT, k, vT, band, lamv, gcol)


def _attn_m_kernel(q_ref, k_ref, v_ref, o_ref):
    for h in range(H_M):
        s = _dot_nt(q_ref[0, h], k_ref[0, h])
        m = jnp.max(s, axis=-1, keepdims=True)
        p = jnp.exp(s - m)
        l = jnp.sum(p, axis=-1, keepdims=True)
        o_ref[0, :, h * D_M:(h + 1) * D_M] = _dot(p.astype(BF16), v_ref[0, h]) / l


def attn_m_prompt(q, k, v):
    B, H, T, D = q.shape
    NM = k.shape[2]
    tq = 512
    return pl.pallas_call(
        _attn_m_kernel,
        grid=(B, T // tq),
        in_specs=[
            pl.BlockSpec((1, H, tq, D), lambda b, i: (b, 0, i, 0)),
            pl.BlockSpec((1, H, NM, D), lambda b, i: (b, 0, 0, 0)),
            pl.BlockSpec((1, H, NM, D), lambda b, i: (b, 0, 0, 0)),
        ],
        out_specs=pl.BlockSpec((1, tq, M_WIDTH), lambda b, i: (b, i, 0)),
        out_shape=jax.ShapeDtypeStruct((B, T, M_WIDTH), F32),
        compiler_params=_cparams(("arbitrary", "arbitrary")),
        name="attn_m_prompt",
    )(q, k, v)


def _merge_kernel(x_ref, oa_ref, ob_ref, om_ref, g0_ref, g1_ref, g2_ref,
                  wa_ref, wb_ref, wm_ref, wo_ref, lg_ref, lb_ref, o_ref, *, alpha, precise):
    def prep(v):
        return v if precise else v.astype(BF16)

    ya = _dot(prep(oa_ref[...]), wa_ref[...], precise)
    yb = _dot(prep(ob_ref[...]), wb_ref[...], precise)
    ym = _dot(prep(om_ref[...]), wm_ref[...], precise)
    hmix = (jax.nn.sigmoid(g0_ref[...]) * ya + jax.nn.sigmoid(g1_ref[...]) * yb
            + jax.nn.sigmoid(g2_ref[...]) * ym)
    mix = _dot(prep(hmix), wo_ref[...], precise)
    o_ref[...] = _layer_norm(alpha * x_ref[...] + mix, lg_ref[...], lb_ref[...])


def merge(x, oa, ob, om, hproj, gate_col0, wa, wb, wm, wo, lg, lb, alpha, tm, precise=False,
          name="merge"):
    M, D = x.shape
    row = lambda i: (i, 0)
    const = lambda i: (0, 0)
    gspec = [pl.BlockSpec((tm, D), (lambda i, c=c: (i, gate_col0 + c))) for c in range(3)]
    return pl.pallas_call(
        functools.partial(_merge_kernel, alpha=alpha, precise=precise),
        grid=(M // tm,),
        in_specs=[pl.BlockSpec((tm, D), row),
                  pl.BlockSpec((tm, HEAD_COLS), row), pl.BlockSpec((tm, HEAD_COLS), row),
                  pl.BlockSpec((tm, HEAD_COLS), row)] + gspec + [
                  pl.BlockSpec((HEAD_COLS, D), const), pl.BlockSpec((HEAD_COLS, D), const),
                  pl.BlockSpec((HEAD_COLS, D), const), pl.BlockSpec((D, D), const),
                  pl.BlockSpec((1, D), const), pl.BlockSpec((1, D), const)],
        out_specs=pl.BlockSpec((tm, D), row),
        out_shape=jax.ShapeDtypeStruct((M, D), F32),
        compiler_params=_cparams(("arbitrary",)),
        name=name,
    )(x, oa, ob, om, hproj, hproj, hproj, wa, wb, wm, wo, lg, lb)


def _ffn_kernel(x_ref, rw_ref, wg_ref, wu_ref, wd_ref, lg_ref, lb_ref, o_ref,
                acc_ref, gate_ref, *, alpha, routed, precise, n_experts):
    e = pl.program_id(1)
    f = pl.program_id(2)
    first = jnp.logical_and(e == 0, f == 0)
    last = jnp.logical_and(e == pl.num_programs(1) - 1, f == pl.num_programs(2) - 1)
    x = x_ref[...]

    @pl.when(first)
    def _():
        acc_ref[...] = jnp.zeros_like(acc_ref)
        if routed:
            logits = _dot(x, rw_ref[...], True)
            ne = logits.shape[-1]
            lane = lax.broadcasted_iota(jnp.int32, logits.shape, 1)
            logits = jnp.where(lane < n_experts, logits, -jnp.inf)
            m1 = jnp.max(logits, axis=-1, keepdims=True)
            i1 = jnp.min(jnp.where(logits == m1, lane, ne), axis=-1, keepdims=True)
            rest = jnp.where(lane == i1, -jnp.inf, logits)
            m2 = jnp.max(rest, axis=-1, keepdims=True)
            i2 = jnp.min(jnp.where(rest == m2, lane, ne), axis=-1, keepdims=True)
            e2 = jnp.exp(m2 - m1)
            w1 = 1.0 / (1.0 + e2)
            w2 = e2 / (1.0 + e2)
            gate_ref[...] = jnp.where(lane == i1, w1, 0.0) + jnp.where(lane == i2, w2, 0.0)

    xin = x if precise else x.astype(BF16)
    hg = _dot(xin, wg_ref[0], precise)
    hu = _dot(xin, wu_ref[0], precise)
    hh = hg * jax.nn.sigmoid(hg) * hu
    if routed:
        gate = gate_ref[...]
        lane = lax.broadcasted_iota(jnp.int32, gate.shape, 1)
        hh = hh * jnp.sum(jnp.where(lane == e, gate, 0.0), axis=-1, keepdims=True)
    acc_ref[...] += _dot(hh if precise else hh.astype(BF16), wd_ref[0], precise)

    @pl.when(last)
    def _():
        o_ref[...] = _layer_norm(alpha * x + acc_ref[...], lg_ref[...], lb_ref[...])


def ffn(x, rw, wg, wu, wd, lg, lb, alpha, tm, tf, routed, precise=False, name="ffn"):
    M, D = x.shape
    E, _, F = wg.shape
    assert M % tm == 0 and F % tf == 0
    return pl.pallas_call(
        functools.partial(_ffn_kernel, alpha=alpha, routed=routed, precise=precise, n_experts=E),
        grid=(M // tm, E, F // tf),
        in_specs=[pl.BlockSpec((tm, D), lambda i, e, f: (i, 0)),
                  pl.BlockSpec(rw.shape, lambda i, e, f: (0, 0)),
                  pl.BlockSpec((1, D, tf), lambda i, e, f: (e, 0, f)),
                  pl.BlockSpec((1, D, tf), lambda i, e, f: (e, 0, f)),
                  pl.BlockSpec((1, tf, D), lambda i, e, f: (e, f, 0)),
                  pl.BlockSpec((1, D), lambda i, e, f: (0, 0)),
                  pl.BlockSpec((1, D), lambda i, e, f: (0, 0))],
        out_specs=pl.BlockSpec((tm, D), lambda i, e, f: (i, 0)),
        out_shape=jax.ShapeDtypeStruct((M, D), F32),
        scratch_shapes=[pltpu.VMEM((tm, D), F32), pltpu.VMEM((tm, rw.shape[1]), F32)],
        compiler_params=_cparams(("arbitrary", "arbitrary", "arbitrary")),
        name=name,
    )(x, rw, wg, wu, wd, lg, lb)


IDX_PAGES_PER_STEP = 16
ATTN_PAGES_PER_STEP = 8


def _page_specs(layer, n, rows, P):
    return [pl.BlockSpec((1, 1, rows, PAGE), (lambda b, s, pt, j=j: (layer, pt[b, s * P + j], 0, 0)))
            for j in range(n)]


def _idx_s_kernel(pt_ref, iqT_ref, iw_ref, iknew_ref, tri_ref, low_ref, *refs, topk):
    P = IDX_PAGES_PER_STEP
    pages = refs[:P]
    am_ref, amnew_ref, s_ref = refs[P:]
    step = pl.program_id(1)
    iqT = iqT_ref[0] * (D_IDX ** -0.5)
    wi = iw_ref[0] * (H_IDX ** -0.5)
    for j in range(P):
        kt = pages[j][0, 0]
        sc = jnp.zeros((1, PAGE), F32)
        for h in range(H_IDX):
            d = jnp.sum(kt * iqT[:, h:h + 1], axis=0, keepdims=True)
            sc = sc + wi[:, h:h + 1] * jnp.maximum(d, 0.0)
        s_ref[pl.ds(step * P + j, 1), :] = _orderable(sc + 0.0)

    @pl.when(step == pl.num_programs(1) - 1)
    def _():
        s = s_ref[...]
        dn = jnp.sum(iqT * iknew_ref[0], axis=0, keepdims=True)
        sn = jnp.sum(wi * jnp.maximum(dn, 0.0), axis=-1, keepdims=True) + 0.0
        kn = _orderable(sn)

        def total(x):
            return jnp.sum(jnp.sum(x, axis=-1, keepdims=True), axis=0, keepdims=True)

        def count_ge(cand):
            return total(jnp.where(s >= cand, 1.0, 0.0)) + jnp.where(kn >= cand, 1.0, 0.0)

        thr = _kth_largest(count_ge, (1, 1), topk)
        c_gt = total(jnp.where(s > thr, 1.0, 0.0)) + jnp.where(kn > thr, 1.0, 0.0)
        need = topk - c_gt
        tie = s == thr
        tie_f = jnp.where(tie, 1.0, 0.0)
        in_row = _dot(tie_f.astype(BF16), tri_ref[...])
        row_tot = jnp.broadcast_to(jnp.sum(tie_f, axis=-1, keepdims=True), tie_f.shape)
        before = _dot(low_ref[...], row_tot.astype(BF16))
        keep_tie = jnp.where(in_row + before < need, 0.0, NEG)
        am_ref[0] = jnp.where(s > thr, 0.0, jnp.where(tie, keep_tie, NEG))
        keep_new = jnp.where(total(tie_f) < need, 0.0, NEG)
        amn = jnp.where(kn > thr, 0.0, jnp.where(kn == thr, keep_new, NEG))
        amnew_ref[0] = jnp.broadcast_to(amn, amnew_ref.shape[1:])


def idx_sample(page_table, layer, iqT, iw, ik_new, poolT, topk):
    DB, npg = page_table.shape
    P = IDX_PAGES_PER_STEP
    assert npg % P == 0
    tri = jnp.asarray(np.triu(np.ones((PAGE, PAGE), np.float32), 1), BF16)
    low = jnp.asarray(np.tril(np.ones((npg, npg), np.float32), -1), BF16)
    per_b = lambda b, s, pt: (b, 0, 0)
    grid_spec = pltpu.PrefetchScalarGridSpec(
        num_scalar_prefetch=1,
        grid=(DB, npg // P),
        in_specs=[pl.BlockSpec((1, D_IDX, H_IDX), per_b),
                  pl.BlockSpec((1, 1, H_IDX), per_b),
                  pl.BlockSpec((1, D_IDX, 1), per_b),
                  pl.BlockSpec((PAGE, PAGE), lambda b, s, pt: (0, 0)),
                  pl.BlockSpec((npg, npg), lambda b, s, pt: (0, 0))]
                 + _page_specs(layer, P, D_IDX, P),
        out_specs=[pl.BlockSpec((1, npg, PAGE), per_b),
                   pl.BlockSpec((1, 1, PAGE), per_b)],
        scratch_shapes=[pltpu.VMEM((npg, PAGE), jnp.int32)],
    )
    return pl.pallas_call(
        functools.partial(_idx_s_kernel, topk=float(topk)),
        grid_spec=grid_spec,
        out_shape=[jax.ShapeDtypeStruct((DB, npg, PAGE), F32),
                   jax.ShapeDtypeStruct((DB, 1, PAGE), F32)],
        compiler_params=_cparams(("arbitrary", "arbitrary")),
        name="idx_sample",
    )(page_table, iqT, iw, ik_new, tri, low, *([poolT] * P))


def _expand_rows(x, rep):
    g, n = x.shape
    return jnp.broadcast_to(x[:, None, :], (g, rep, n)).reshape(g * rep, n)


def _group_logits(kt, qb):
    w, n = kt.shape
    return jnp.sum((kt * qb).reshape(N_GROUPS, w // N_GROUPS, n), axis=1)


def _dec_a_kernel(pt_ref, q_ref, am_ref, amnew_ref, knew_ref, vnew_ref, blast_ref, bnew_ref,
                  *refs, scale):
    P = ATTN_PAGES_PER_STEP
    kp = refs[:P]
    vp = refs[P:2 * P]
    o_ref, qb_ref, m_ref, l_ref, acc_ref = refs[2 * P:]
    step = pl.program_id(1)
    is_last = step == pl.num_programs(1) - 1
    W = qb_ref.shape[0]
    rep = W // N_GROUPS

    @pl.when(step == 0)
    def _():
        qb_ref[...] = jnp.broadcast_to(q_ref[0] * scale, qb_ref.shape)
        m_ref[...] = jnp.full_like(m_ref, NEG)
        l_ref[...] = jnp.zeros_like(l_ref)
        acc_ref[...] = jnp.zeros_like(acc_ref)

    last_f = jnp.where(is_last, 1.0, 0.0)
    for j in range(P):
        lg = _group_logits(kp[j][0, 0], qb_ref[...]) + am_ref[0, j:j + 1, :]
        if j == P - 1:
            lg = lg + last_f * blast_ref[...]
        m_old = m_ref[...]
        m_new = jnp.maximum(m_old, lg)
        alpha = jnp.exp(m_old - m_new)
        p = jnp.exp(lg - m_new)
        m_ref[...] = m_new
        l_ref[...] = l_ref[...] * alpha + p
        acc_ref[...] = acc_ref[...] * _expand_rows(alpha, rep) + _expand_rows(p, rep) * vp[j][0, 0]

    @pl.when(is_last)
    def _():
        q = q_ref[0] * scale
        lgn = (jnp.sum((knew_ref[0] * q).reshape(N_GROUPS, rep, 1), axis=1)
               + bnew_ref[...] + amnew_ref[0][:, 0:1])
        m = m_ref[...]
        mx = jnp.maximum(jnp.max(m, axis=-1, keepdims=True), lgn)
        w = jnp.exp(m - mx)
        pn = jnp.exp(lgn - mx)
        l = jnp.sum(l_ref[...] * w, axis=-1, keepdims=True) + pn
        o = (jnp.sum(acc_ref[...] * _expand_rows(w, rep), axis=-1, keepdims=True)
             + _expand_rows(pn, rep) * vnew_ref[0])
        o_ref[0] = o / _expand_rows(l, rep)


def decode_a(page_table, layer, qcol, am, amnew, knew, vnew, blast, bnew, kT, vT, scale):
    DB, npg = page_table.shape
    W = qcol.shape[1]
    P = ATTN_PAGES_PER_STEP
    assert npg % P == 0
    per_b = lambda b, s, pt: (b, 0, 0)
    c2 = lambda b, s, pt: (0, 0)
    grid_spec = pltpu.PrefetchScalarGridSpec(
        num_scalar_prefetch=1,
        grid=(DB, npg // P),
        in_specs=[pl.BlockSpec((1, W, 1), per_b),
                  pl.BlockSpec((1, P, PAGE), lambda b, s, pt: (b, s, 0)),
                  pl.BlockSpec((1, 1, PAGE), per_b),
                  pl.BlockSpec((1, W, 1), per_b),
                  pl.BlockSpec((1, W, 1), per_b),
                  pl.BlockSpec((N_GROUPS, PAGE), c2),
                  pl.BlockSpec((N_GROUPS, 1), c2)]
                 + _page_specs(layer, P, W, P) + _page_specs(layer, P, W, P),
        out_specs=pl.BlockSpec((1, W, 1), per_b),
        scratch_shapes=[pltpu.VMEM((W, PAGE), F32), pltpu.VMEM((N_GROUPS, PAGE), F32),
                        pltpu.VMEM((N_GROUPS, PAGE), F32), pltpu.VMEM((W, PAGE), F32)],
    )
    return pl.pallas_call(
        functools.partial(_dec_a_kernel, scale=scale),
        grid_spec=grid_spec,
        out_shape=jax.ShapeDtypeStruct((DB, W, 1), F32),
        compiler_params=_cparams(("arbitrary", "arbitrary")),
        name="decode_a",
    )(page_table, qcol, am, amnew, knew, vnew, blast, bnew, *([kT] * P), *([vT] * P))


def _dec_b_kernel(pt_ref, q_ref, knew_ref, vnew_ref, blast_ref, bnew_ref, r_ref, pm_ref,
                  lam_ref, g_ref, *refs, scale, lam_init):
    P = ATTN_PAGES_PER_STEP
    kp = refs[:P]
    vp = refs[P:2 * P]
    o_ref, qb_ref, m_ref, l_ref, acc_ref = refs[2 * P:]
    step = pl.program_id(1)
    is_last = step == pl.num_programs(1) - 1
    W = qb_ref.shape[0]
    rep = W // N_GROUPS

    @pl.when(step == 0)
    def _():
        qb_ref[...] = jnp.broadcast_to(q_ref[0] * scale, qb_ref.shape)
        m_ref[...] = jnp.full_like(m_ref, NEG)
        l_ref[...] = jnp.zeros_like(l_ref)
        acc_ref[...] = jnp.zeros_like(acc_ref)

    last_f = jnp.where(is_last, 1.0, 0.0)
    for j in range(P):
        lg = _group_logits(kp[j][0, 0], qb_ref[...])
        if j == P - 1:
            lg = lg + last_f * blast_ref[...]
        m_old = m_ref[...]
        m_new = jnp.maximum(m_old, jnp.max(lg, axis=-1, keepdims=True))
        alpha = jnp.exp(m_old - m_new)
        p = jnp.exp(lg - m_new)
        m_ref[...] = m_new
        l_ref[...] = l_ref[...] * alpha + jnp.sum(p, axis=-1, keepdims=True)
        pr = (_dot(p.astype(BF16), r_ref[...]) * pm_ref[...]).astype(BF16)
        acc_ref[...] = acc_ref[...] * alpha + _dot(pr, vp[j][0, 0].astype(BF16))

    @pl.when(is_last)
    def _():
        q = q_ref[0] * scale
        lgn = jnp.sum((knew_ref[0] * q).reshape(N_GROUPS, rep, 1), axis=1) + bnew_ref[...]
        m = m_ref[...]
        mx = jnp.maximum(m, lgn)
        a2 = jnp.exp(m - mx)
        pn = jnp.exp(lgn - mx)
        l = l_ref[...] * a2 + pn
        on = (acc_ref[...] * a2 + pn * vnew_ref[0]) / l
        lam = _lambda(lam_ref[...], lam_init)
        r = lax.broadcasted_iota(jnp.int32, (N_GROUPS, N_GROUPS), 0)
        c = lax.broadcasted_iota(jnp.int32, (N_GROUPS, N_GROUPS), 1)
        comb = jnp.where(c == 2 * r, 1.0, 0.0) - lam * jnp.where(c == 2 * r + 1, 1.0, 0.0)
        o = _dot(comb, on, True)
        o = o * lax.rsqrt(jnp.mean(o * o, axis=-1, keepdims=True) + SUBLN_EPS)
        o_ref[0] = o * g_ref[...] * (1.0 - lam_init)


def decode_b(page_table, layer, qcol, knew, vnew_g, blast, bnew, rmat, pmask, lamv, g, kT, vrows,
             scale, lam_init):
    DB, npg = page_table.shape
    W = qcol.shape[1]
    P = ATTN_PAGES_PER_STEP
    dv = 2 * D_B
    assert npg % P == 0
    per_b = lambda b, s, pt: (b, 0, 0)
    c2 = lambda b, s, pt: (0, 0)
    grid_spec = pltpu.PrefetchScalarGridSpec(
        num_scalar_prefetch=1,
        grid=(DB, npg // P),
        in_specs=[pl.BlockSpec((1, W, 1), per_b),
                  pl.BlockSpec((1, W, 1), per_b),
                  pl.BlockSpec((1, N_GROUPS, dv), per_b),
                  pl.BlockSpec((N_GROUPS, PAGE), c2),
                  pl.BlockSpec((N_GROUPS, 1), c2),
                  pl.BlockSpec((PAGE, W), c2),
                  pl.BlockSpec((N_GROUPS, W), c2),
                  pl.BlockSpec((4, D_B), c2),
                  pl.BlockSpec((1, dv), c2)]
                 + _page_specs(layer, P, W, P) + _page_specs(layer, P, W, P),
        out_specs=pl.BlockSpec((1, N_GROUPS, dv), per_b),
        scratch_shapes=[pltpu.VMEM((W, PAGE), F32), pltpu.VMEM((N_GROUPS, 1), F32),
                        pltpu.VMEM((N_GROUPS, 1), F32), pltpu.VMEM((N_GROUPS, dv), F32)],
    )
    return pl.pallas_call(
        functools.partial(_dec_b_kernel, scale=scale, lam_init=lam_init),
        grid_spec=grid_spec,
        out_shape=jax.ShapeDtypeStruct((DB, N_GROUPS, dv), F32),
        compiler_params=_cparams(("arbitrary", "arbitrary")),
        name="decode_b",
    )(page_table, qcol, knew, vnew_g, blast, bnew, rmat, pmask, lamv, g,
      *([kT] * P), *([vrows] * P))


def _dec_m_kernel(q_ref, k_ref, v_ref, o_ref):
    n = k_ref.shape[2] // 8
    q = q_ref[0]
    k3 = k_ref[0, 0].reshape(n, 8, D_M)
    lg = jnp.sum(k3 * q[None], axis=-1, keepdims=True)
    m8 = jnp.max(lg, axis=0)
    m4 = jnp.maximum(m8[:H_M], m8[H_M:])
    p = jnp.exp(lg - jnp.concatenate([m4, m4], axis=0)[None])
    l8 = jnp.sum(p, axis=0)
    o8 = jnp.sum(p * v_ref[0, 0].reshape(n, 8, D_M), axis=0)
    o_ref[0] = (o8[:H_M] + o8[H_M:]) / (l8[:H_M] + l8[H_M:])


def decode_m(layer, q8, mem_k, mem_v):
    DB = q8.shape[0]
    rows = mem_k.shape[2]
    return pl.pallas_call(
        _dec_m_kernel,
        grid=(DB,),
        in_specs=[pl.BlockSpec((1, 8, D_M), lambda b: (b, 0, 0)),
                  pl.BlockSpec((1, 1, rows, D_M), lambda b: (layer, b, 0, 0)),
                  pl.BlockSpec((1, 1, rows, D_M), lambda b: (layer, b, 0, 0))],
        out_specs=pl.BlockSpec((1, H_M, D_M), lambda b: (b, 0, 0)),
        out_shape=jax.ShapeDtypeStruct((DB, H_M, D_M), F32),
        compiler_params=_cparams(("arbitrary",)),
        name="decode_m",
    )(q8, mem_k, mem_v)


def _t5_bucket_np(rel):
    n = np.maximum(rel, 0)
    max_exact = NUM_BUCKETS // 2
    nf = np.maximum(n, 1).astype(np.float32)
    large = max_exact + (np.log(nf / np.float32(max_exact))
                         / np.float32(math.log(MAX_DISTANCE / max_exact))
                         * np.float32(NUM_BUCKETS - max_exact)).astype(np.int32)
    large = np.minimum(large, NUM_BUCKETS - 1)
    return np.where(n < max_exact, n, large).astype(np.int32)


def _far_bucket(first_far, last_far):
    b = _t5_bucket_np(np.arange(first_far, last_far + 1))
    assert (b == b[0]).all(), "relative-position bias must be constant beyond one tile"
    return int(b[0])


def _band_tables(rel_bias, heads, t_max):
    far = _far_bucket(TQ + 1, max(t_max, TQ + 1))
    r = np.arange(TQ)[None, :] - np.arange(TQ)[:, None]
    tabs = []
    for d in range(2):
        rel = r + d * TQ
        vals = rel_bias[_t5_bucket_np(rel)][..., heads] - rel_bias[far][heads]
        vals = jnp.where((rel >= 0)[..., None], vals, NEG)
        tabs.append(jnp.transpose(vals, (2, 0, 1)))
    return jnp.stack(tabs, axis=1).astype(F32)


def _split3(x, order):
    hi = x.astype(BF16)
    lo = (x - hi.astype(F32)).astype(BF16)
    return jnp.concatenate([(hi, lo)[o] for o in order], axis=-1)


def kernel(x_prompt, x_sample, cache_a_k, cache_a_v, cache_idx_k, cache_b_k, cache_b_v, cache_mem_k, cache_mem_v, page_table, mem_prompt, w_in, w_mem_kv, lambda_q1, lambda_k1, lambda_q2, lambda_k2, subln_g, w_branch_a, w_branch_b, w_branch_m, w_out, rel_bias, ln1_g, ln1_b, ln2_g, ln2_b, ffn_w_gate, ffn_w_up, ffn_w_down, router_w, expert_w_gate, expert_w_up, expert_w_down):
    B, T, D = x_prompt.shape
    DB, TS, _ = x_sample.shape
    depth = w_in.shape[0]
    n_pool = cache_a_k.shape[1]
    npg = page_table.shape[1]
    past = npg * PAGE
    n_mem = mem_prompt.shape[1]
    assert TS == 1 and T % TQ == 0 and cache_a_k.shape[2] == PAGE
    alpha = (2 * depth) ** 0.25
    topk_p = min(TOPK_MAX, T // 4)
    topk_s = min(TOPK_MAX, (past + TS) // 4)
    W = HEAD_COLS
    nq = T // TQ

    c_ik = 3 * A_WIDTH + H_IDX * D_IDX
    c_b = c_ik + D_IDX + H_IDX
    n_main = 8 * W + 3 * D
    tail_pad = 128 - D_IDX - H_IDX

    low = jnp.asarray(np.tril(np.ones((TQ, TQ), np.float32), -1), BF16)
    band_a = _band_tables(rel_bias, np.arange(H_A), T)
    band_b = _band_tables(rel_bias, H_A + np.arange(H_B), T)

    far_s = _far_bucket(PAGE + 1, past)
    b_last = (rel_bias[_t5_bucket_np(PAGE - np.arange(PAGE))] - rel_bias[far_s]).T
    b_new = (rel_bias[0] - rel_bias[far_s])[:, None]
    blast_a, bnew_a = b_last[:H_A], b_new[:H_A]
    blast_b, bnew_b = jnp.repeat(b_last[H_A:], 2, axis=0), jnp.repeat(b_new[H_A:], 2, axis=0)
    lane = np.arange(W)
    rmat = jnp.asarray(lane[None, :] // H_B == np.arange(PAGE)[:, None], BF16)
    pmask = jnp.asarray(lane[None, :] % H_B == np.arange(N_GROUPS)[:, None] // 2, F32)

    idxT = jnp.transpose(cache_idx_k, (0, 1, 3, 2))
    akT = jnp.transpose(cache_a_k, (0, 1, 3, 4, 2)).reshape(depth, n_pool, W, PAGE)
    avT = jnp.transpose(cache_a_v, (0, 1, 3, 4, 2)).reshape(depth, n_pool, W, PAGE)
    bkT = jnp.transpose(cache_b_k, (0, 1, 3, 4, 5, 2)).reshape(depth, n_pool, W, PAGE)
    bv_rows = cache_b_v.reshape(depth, n_pool, PAGE * H_B, 2 * D_B)
    memk_rows = cache_mem_k.reshape(depth, DB, n_mem * H_M, D_M)
    memv_rows = cache_mem_v.reshape(depth, DB, n_mem * H_M, D_M)

    outs = {n: [] for n in ('akp', 'avp', 'ikp', 'bkp', 'bvp', 'mkp', 'mvp',
                            'aks', 'avs', 'iks', 'bks', 'bvs')}
    x_p = x_prompt.reshape(B * T, D)
    x_s = x_sample.reshape(DB, D)
    for l in range(depth):
        lam_init = 0.8 - 0.6 * math.exp(-0.3 * l)
        lamv = jnp.stack([lambda_q1[l], lambda_k1[l], lambda_q2[l], lambda_k2[l]])
        wl = w_in[l]
        w_perm = jnp.concatenate([wl[:, :c_ik], wl[:, c_b:], wl[:, c_ik:c_b],
                                  jnp.zeros((D, tail_pad), F32)], axis=1)
        w_main_bf = w_perm[:, :n_main].astype(BF16)
        w_tail_bf = w_perm[:, n_main:].astype(BF16)
        wa_l, wb_l, wm_l, wo_l = w_branch_a[l], w_branch_b[l], w_branch_m[l], w_out[l]
        lg1, lb1 = ln1_g[l][None], ln1_b[l][None]
        lg2, lb2 = ln2_g[l][None], ln2_b[l][None]
        g_sub = subln_g[l][None]
        j = l // 2
        if l % 2 == 0:
            rw = jnp.zeros((D, 128), F32)
            wg, wu, wd = ffn_w_gate[j][None], ffn_w_up[j][None], ffn_w_down[j][None]
            routed = False
        else:
            rw = jnp.pad(router_w[j], ((0, 0), (0, 128 - N_EXPERTS)))
            wg, wu, wd = expert_w_gate[j], expert_w_up[j], expert_w_down[j]
            routed = True

        h = matmul(x_p, w_main_bf, 512, 1024, name="proj_prompt")
        tail = matmul(x_p, w_tail_bf, 512, 128, name="proj_tail")
        h3 = h.reshape(B, T, n_main)
        col = lambda c: h3[:, :, c * W:(c + 1) * W]

        def heads(c, nh, dh, dt):
            return col(c).reshape(B, T, nh, dh).transpose(0, 2, 1, 3).astype(dt)

        def q_t(c, scale):
            return jnp.transpose(col(c) * scale, (0, 2, 1)).astype(BF16)

        def v_t(c):
            return col(c).reshape(B, nq, TQ, W).transpose(0, 1, 3, 2).astype(BF16)

        a_k, a_v, b_k, b_v = col(1), col(2), col(5), col(6)
        i_k = tail[:, :D_IDX].reshape(B, T, D_IDX)
        i_w = tail[:, D_IDX:D_IDX + H_IDX].reshape(B, T, H_IDX)
        iq = col(3).reshape(B, T, H_IDX, D_IDX) * (D_IDX ** -0.5)
        iqT = jnp.transpose(_split3(iq, (0, 0, 1)), (0, 2, 3, 1))
        ik3 = _split3(i_k, (0, 1, 0))
        wT = jnp.transpose(i_w * (H_IDX ** -0.5), (0, 2, 1))
        o_a = attn_a_prompt(q_t(0, D_A ** -0.5), heads(1, H_A, D_A, BF16), v_t(2), iqT, ik3, wT,
                            band_a, low, topk_p)
        o_b = attn_b_prompt(q_t(4, D_B ** -0.5), heads(5, 2 * H_B, D_B, BF16), v_t(6),
                            band_b, lamv, g_sub.T, lam_init)
        kv = matmul(mem_prompt.reshape(B * n_mem, D), w_mem_kv[l].astype(BF16), 512, 1024,
                    name="proj_mem")
        mk = kv[:, :M_WIDTH].reshape(B, n_mem, H_M, D_M)
        mv = kv[:, M_WIDTH:].reshape(B, n_mem, H_M, D_M)
        qm = (heads(7, H_M, D_M, F32) * (D_M ** -0.5)).astype(BF16)
        o_m = attn_m_prompt(qm, mk.transpose(0, 2, 1, 3).astype(BF16),
                            mv.transpose(0, 2, 1, 3).astype(BF16))
        x_p = merge(x_p, o_a.reshape(B * T, W), o_b.reshape(B * T, W), o_m.reshape(B * T, W),
                    h, 8 * W // D, wa_l.astype(BF16), wb_l.astype(BF16), wm_l.astype(BF16),
                    wo_l.astype(BF16), lg1, lb1, alpha, 512, name="merge_prompt")
        x_p = ffn(x_p, rw, wg.astype(BF16), wu.astype(BF16), wd.astype(BF16), lg2, lb2,
                  alpha, 1024 if B * T % 1024 == 0 else 512, 256, routed, name="ffn_prompt")
        outs['akp'].append(a_k.reshape(B, T, H_A, D_A))
        outs['avp'].append(a_v.reshape(B, T, H_A, D_A))
        outs['ikp'].append(i_k)
        outs['bkp'].append(b_k.reshape(B, T, H_B, 2, D_B))
        outs['bvp'].append(b_v.reshape(B, T, H_B, 2 * D_B))
        outs['mkp'].append(mk)
        outs['mvp'].append(mv)

        hs = matmul(x_s, w_perm[:, :n_main], DB, 1024, precise=True, name="proj_sample")
        ts = matmul(x_s, w_perm[:, n_main:], DB, 128, precise=True, name="proj_tail_sample")
        sa_q, sa_k, sa_v, s_iq = (hs[:, c * W:(c + 1) * W] for c in range(4))
        sb_q, sb_k, sb_v, sm_q = (hs[:, c * W:(c + 1) * W] for c in range(4, 8))
        s_ik = ts[:, :D_IDX]
        s_iw = ts[:, D_IDX:D_IDX + H_IDX]
        colv = lambda t: t.reshape(DB, -1, 1)
        am, amn = idx_sample(page_table, l, jnp.transpose(s_iq.reshape(DB, H_IDX, D_IDX), (0, 2, 1)),
                             s_iw.reshape(DB, 1, H_IDX), colv(s_ik), idxT, topk_s)
        o_a = decode_a(page_table, l, colv(sa_q), am, amn, colv(sa_k), colv(sa_v),
                       blast_a, bnew_a, akT, avT, D_A ** -0.5).reshape(DB, W)
        vnew_g = jnp.repeat(sb_v.reshape(DB, H_B, 2 * D_B), 2, axis=1)
        o_b = decode_b(page_table, l, colv(sb_q), colv(sb_k), vnew_g, blast_b, bnew_b, rmat, pmask,
                       lamv, g_sub, bkT, bv_rows, D_B ** -0.5, lam_init)[:, :H_B].reshape(DB, W)
        q8 = jnp.tile(sm_q.reshape(DB, H_M, D_M) * (D_M ** -0.5), (1, 2, 1))
        o_m = decode_m(l, q8, memk_rows, memv_rows).reshape(DB, W)
        x_s = merge(x_s, o_a, o_b, o_m, hs, 8 * W // D, wa_l, wb_l, wm_l, wo_l, lg1, lb1, alpha,
                    DB, precise=True, name="merge_sample")
        x_s = ffn(x_s, rw, wg, wu, wd, lg2, lb2, alpha, DB, 256, routed, precise=True,
                  name="ffn_sample")
        outs['aks'].append(sa_k.reshape(DB, TS, H_A, D_A))
        outs['avs'].append(sa_v.reshape(DB, TS, H_A, D_A))
        outs['iks'].append(s_ik.reshape(DB, TS, D_IDX))
        outs['bks'].append(sb_k.reshape(DB, TS, H_B, 2, D_B))
        outs['bvs'].append(sb_v.reshape(DB, TS, H_B, 2 * D_B))

    st = {n: jnp.stack(v) for n, v in outs.items()}
    return (x_p.reshape(B, T, D), x_s.reshape(DB, TS, D),
            st['akp'], st['avp'], st['ikp'], st['bkp'], st['bvp'], st['mkp'], st['mvp'],
            st['aks'], st['avs'], st['iks'], st['bks'], st['bvs'])
```

```python
import functools
import math

import jax
import jax.numpy as jnp
import numpy as np
from jax import lax
from jax.experimental import pallas as pl
from jax.experimental.pallas import tpu as pltpu
from jax.experimental.pallas import tpu_sc as plsc

F32 = jnp.float32
BF16 = jnp.bfloat16
HIGHEST = lax.Precision.HIGHEST

H_A, D_A = 8, 64
H_IDX, D_IDX = 8, 64
TOPK_MAX = 256
H_B, D_B = 4, 64
H_M, D_M = 4, 128
NUM_BUCKETS, MAX_DISTANCE = 32, 128
N_EXPERTS = 8
LN_EPS = 1e-5
SUBLN_EPS = 1e-5
A_WIDTH = H_A * D_A
B_WIDTH = H_B * 2 * D_B
M_WIDTH = H_M * D_M
HEAD_COLS = 512
PAGE = 128
N_GROUPS = 8

NEG = -1e30
KEY_NEG_INF = -2139095041
INT_MIN = -2147483648

VMEM_LIMIT = 56 * 1024 * 1024

TQ = 256
IDX_SPLIT = 3 * D_IDX


def _cparams(sem):
    return pltpu.CompilerParams(dimension_semantics=sem, vmem_limit_bytes=VMEM_LIMIT)


def _dot(a, b, precise=False):
    if precise:
        return jnp.dot(a, b, preferred_element_type=F32, precision=HIGHEST)
    return jnp.dot(a, b, preferred_element_type=F32)


def _dot_nt(a, b):
    return lax.dot_general(a, b, (((1,), (1,)), ((), ())), preferred_element_type=F32)


def _orderable(s):
    b = lax.bitcast_convert_type(s, jnp.int32)
    return jnp.where(b < 0, b ^ jnp.int32(0x7FFFFFFF), b)


def _layer_norm(v, g, b):
    mu = jnp.mean(v, axis=-1, keepdims=True)
    d = v - mu
    var = jnp.mean(d * d, axis=-1, keepdims=True)
    return d * lax.rsqrt(var + LN_EPS) * g + b


def _kth_largest(count_ge, shape, topk):
    t0 = jnp.full(shape, INT_MIN, jnp.int32)
    t0 = jnp.where(count_ge(jnp.zeros(shape, jnp.int32)) >= topk, jnp.zeros(shape, jnp.int32), t0)

    def bit_body(b, t):
        cand = t | jnp.left_shift(jnp.int32(1), 30 - b)
        return jnp.where(count_ge(cand) >= topk, cand, t)

    return lax.fori_loop(0, 31, bit_body, t0)


def _lambda(lv, lam_init):
    return (jnp.exp(jnp.sum(lv[0:1] * lv[1:2], axis=-1, keepdims=True))
            - jnp.exp(jnp.sum(lv[2:3] * lv[3:4], axis=-1, keepdims=True)) + lam_init)


def _mm_kernel(x_ref, w_ref, o_ref, *, precise):
    if precise:
        o_ref[...] = _dot(x_ref[...], w_ref[...], True)
    else:
        o_ref[...] = _dot(x_ref[...].astype(BF16), w_ref[...])


def matmul(x, w, tm, tn, precise=False, name="matmul"):
    M, K = x.shape
    N = w.shape[1]
    assert M % tm == 0 and N % tn == 0
    return pl.pallas_call(
        functools.partial(_mm_kernel, precise=precise),
        grid=(N // tn, M // tm),
        in_specs=[pl.BlockSpec((tm, K), lambda j, i: (i, 0)),
                  pl.BlockSpec((K, tn), lambda j, i: (0, j))],
        out_specs=pl.BlockSpec((tm, tn), lambda j, i: (i, j)),
        out_shape=jax.ShapeDtypeStruct((M, N), F32),
        compiler_params=_cparams(("arbitrary", "arbitrary")),
        name=name,
    )(x, w)


def _set_row(full, g, row):
    r = lax.broadcasted_iota(jnp.int32, full.shape, 0)
    return jnp.where(r == g, row, full)


def _attend_tile(kj, band_d, ml, *, n_groups, dv, qT_ref, k_ref, vT_ref, band_ref, am, acc_ref,
                 s_ref, p_ref, tq):
    m_all, l_all = ml
    ksl = pl.ds(pl.multiple_of(kj * tq, tq), tq)
    gpb = n_groups // band_ref.shape[0]
    gpv = (n_groups * dv) // vT_ref.shape[2]
    dq = qT_ref.shape[1] // n_groups
    m_out, l_out = m_all, l_all
    for g in range(n_groups):
        s_ref[g] = _dot(k_ref[0, g, ksl, :], qT_ref[0, g * dq:(g + 1) * dq, :])
    alphas = []
    for g in range(n_groups):
        s = s_ref[g]
        if am is not None:
            s = s + am
        if band_d is not None:
            s = s + band_ref[g // gpb, band_d]
        m_old = m_all[g:g + 1, :]
        m_new = jnp.maximum(m_old, jnp.max(s, axis=0, keepdims=True))
        alpha = jnp.exp(m_old - m_new)
        p = jnp.exp(s - m_new)
        p_ref[g] = p.astype(BF16)
        alphas.append(alpha)
        m_out = _set_row(m_out, g, m_new)
        l_out = _set_row(l_out, g, alpha * l_all[g:g + 1, :] + jnp.sum(p, axis=0, keepdims=True))
    for g in range(n_groups):
        vrows = slice((g // gpv) * dv, (g // gpv + 1) * dv)
        arows = slice(g * dv, (g + 1) * dv)
        acc_ref[arows, :] = alphas[g] * acc_ref[arows, :] + _dot(vT_ref[0, kj, vrows, :], p_ref[g])
    return m_out, l_out


def _attend_causal(i, tile_fn, n_groups, tq):
    ml = (jnp.full((n_groups, tq), NEG, F32), jnp.zeros((n_groups, tq), F32))
    ml = lax.fori_loop(0, jnp.maximum(i - 1, 0), lambda kj, c: tile_fn(kj, None, c), ml)
    ml = lax.cond(i >= 1, lambda c: tile_fn(i - 1, 1, c), lambda c: c, ml)
    return tile_fn(i, 0, ml)


def _attn_a_kernel(qT_ref, k_ref, vT_ref, iqT_ref, ik_ref, wT_ref, band_ref, low_ref,
                   o_ref, key_ref, am_ref, acc_ref, s_ref, p_ref, *, topk):
    i = pl.program_id(1)
    nk = i + 1
    tq = qT_ref.shape[2]
    qpos = i * tq + lax.broadcasted_iota(jnp.int32, (tq, tq), 1)
    krow = lax.broadcasted_iota(jnp.int32, (tq, tq), 0)

    def score_body(kj, c):
        ikt = ik_ref[0, pl.ds(pl.multiple_of(kj * tq, tq), tq), :]
        s = jnp.zeros((tq, tq), F32)
        for h in range(H_IDX):
            s = s + wT_ref[0, h:h + 1, :] * jnp.maximum(_dot(ikt, iqT_ref[0, h]), 0.0)
        s = jnp.where(kj * tq + krow <= qpos, s, -jnp.inf)
        key_ref[kj] = _orderable(s)
        return c

    lax.fori_loop(0, nk, score_body, 0)

    def count(pred):
        def body(kj, c):
            t = jnp.where(pred(key_ref[kj]), 1.0, 0.0)
            return c + jnp.sum(t.reshape(tq // 8, 8, tq), axis=0)
        c = lax.fori_loop(0, nk, body, jnp.zeros((8, tq), F32))
        return jnp.sum(c, axis=0, keepdims=True)

    thr = _kth_largest(lambda cand: count(lambda s: s >= cand), (1, tq), topk)
    c_gt = count(lambda s: s > thr)
    need = jnp.where(thr > KEY_NEG_INF, topk - c_gt, 0.0)

    def mask_body(kj, run):
        s = key_ref[kj]
        tie = s == thr
        tie_f = jnp.where(tie, 1.0, 0.0)
        pre = _dot(low_ref[...], tie_f.astype(BF16)) + run
        keep_tie = jnp.where(pre < need, 0.0, NEG)
        am_ref[kj] = jnp.where(s > thr, 0.0, jnp.where(tie, keep_tie, NEG))
        return run + jnp.sum(tie_f, axis=0, keepdims=True)

    lax.fori_loop(0, nk, mask_body, jnp.zeros((1, tq), F32))

    acc_ref[...] = jnp.zeros_like(acc_ref)

    def tile_fn(kj, band_d, ml):
        return _attend_tile(kj, band_d, ml, n_groups=H_A, dv=D_A, qT_ref=qT_ref, k_ref=k_ref,
                            vT_ref=vT_ref, band_ref=band_ref, am=am_ref[kj], acc_ref=acc_ref,
                            s_ref=s_ref, p_ref=p_ref, tq=tq)

    _, l_all = _attend_causal(i, tile_fn, H_A, tq)
    inv = 1.0 / l_all
    oT = jnp.concatenate([acc_ref[h * D_A:(h + 1) * D_A, :] * inv[h:h + 1, :] for h in range(H_A)],
                         axis=0)
    o_ref[0] = oT.T


def attn_a_prompt(qT, k, vT, iqT, ik, wT, band, low, topk):
    B, H, T, D = k.shape
    nq = T // TQ
    W = H * D
    return pl.pallas_call(
        functools.partial(_attn_a_kernel, topk=float(topk)),
        grid=(B, nq),
        in_specs=[
            pl.BlockSpec((1, W, TQ), lambda b, i: (b, 0, i)),
            pl.BlockSpec((1, H, T, D), lambda b, i: (b, 0, 0, 0)),
            pl.BlockSpec((1, nq, W, TQ), lambda b, i: (b, 0, 0, 0)),
            pl.BlockSpec((1, H_IDX, IDX_SPLIT, TQ), lambda b, i: (b, 0, 0, i)),
            pl.BlockSpec((1, T, IDX_SPLIT), lambda b, i: (b, 0, 0)),
            pl.BlockSpec((1, H_IDX, TQ), lambda b, i: (b, 0, i)),
            pl.BlockSpec((H, 2, TQ, TQ), lambda b, i: (0, 0, 0, 0)),
            pl.BlockSpec((TQ, TQ), lambda b, i: (0, 0)),
        ],
        out_specs=pl.BlockSpec((1, TQ, W), lambda b, i: (b, i, 0)),
        out_shape=jax.ShapeDtypeStruct((B, T, W), F32),
        scratch_shapes=[pltpu.VMEM((nq, TQ, TQ), jnp.int32),
                        pltpu.VMEM((nq, TQ, TQ), F32),
                        pltpu.VMEM((W, TQ), F32),
                        pltpu.VMEM((H, TQ, TQ), F32), pltpu.VMEM((H, TQ, TQ), BF16)],
        compiler_params=_cparams(("arbitrary", "arbitrary")),
        name="attn_a_prompt",
    )(qT, k, vT, iqT, ik, wT, band, low)


def _attn_b_kernel(qT_ref, k_ref, vT_ref, band_ref, lam_ref, g_ref, o_ref,
                   acc_ref, s_ref, p_ref, *, lam_init):
    i = pl.program_id(1)
    tq = qT_ref.shape[2]
    dv = 2 * D_B
    acc_ref[...] = jnp.zeros_like(acc_ref)

    def tile_fn(kj, band_d, ml):
        return _attend_tile(kj, band_d, ml, n_groups=2 * H_B, dv=dv, qT_ref=qT_ref, k_ref=k_ref,
                            vT_ref=vT_ref, band_ref=band_ref, am=None, acc_ref=acc_ref,
                            s_ref=s_ref, p_ref=p_ref, tq=tq)

    _, l_all = _attend_causal(i, tile_fn, 2 * H_B, tq)
    lam = _lambda(lam_ref[...], lam_init)
    inv = 1.0 / l_all
    parts = []
    for h in range(H_B):
        g0, g1 = 2 * h, 2 * h + 1
        o = (acc_ref[g0 * dv:(g0 + 1) * dv, :] * inv[g0:g0 + 1, :]
             - lam * (acc_ref[g1 * dv:(g1 + 1) * dv, :] * inv[g1:g1 + 1, :]))
        o = o * lax.rsqrt(jnp.mean(o * o, axis=0, keepdims=True) + SUBLN_EPS)
        parts.append(o * g_ref[...] * (1.0 - lam_init))
    o_ref[0] = jnp.concatenate(parts, axis=0).T


def attn_b_prompt(qT, k, vT, band, lamv, gcol, lam_init):
    B, G, T, D = k.shape
    nq = T // TQ
    W = G * D
    return pl.pallas_call(
        functools.partial(_attn_b_kernel, lam_init=lam_init),
        grid=(B, nq),
        in_specs=[
            pl.BlockSpec((1, W, TQ), lambda b, i: (b, 0, i)),
            pl.BlockSpec((1, G, T, D), lambda b, i: (b, 0, 0, 0)),
            pl.BlockSpec((1, nq, W, TQ), lambda b, i: (b, 0, 0, 0)),
            pl.BlockSpec((H_B, 2, TQ, TQ), lambda b, i: (0, 0, 0, 0)),
            pl.BlockSpec((4, D_B), lambda b, i: (0, 0)),
            pl.BlockSpec((2 * D_B, 1), lambda b, i: (0, 0)),
        ],
        out_specs=pl.BlockSpec((1, TQ, W), lambda b, i: (b, i, 0)),
        out_shape=jax.ShapeDtypeStruct((B, T, W), F32),
        scratch_shapes=[pltpu.VMEM((G * 2 * D_B, TQ), F32),
                        pltpu.VMEM((G, TQ, TQ), F32), pltpu.VMEM((G, TQ, TQ), BF16)],
        compiler_params=_cparams(("arbitrary", "arbitrary")),
        name="attn_b_prompt",
    )(qT, k, vT, band, lamv, gcol)


def _attn_m_kernel(q_ref, k_ref, v_ref, o_ref):
    for h in range(H_M):
        s = _dot_nt(q_ref[0, h], k_ref[0, h])
        m = jnp.max(s, axis=-1, keepdims=True)
        p = jnp.exp(s - m)
        l = jnp.sum(p, axis=-1, keepdims=True)
        o_ref[0, :, h * D_M:(h + 1) * D_M] = _dot(p.astype(BF16), v_ref[0, h]) / l


def attn_m_prompt(q, k, v):
    B, H, T, D = q.shape
    NM = k.shape[2]
    tq = 512
    return pl.pallas_call(
        _attn_m_kernel,
        grid=(B, T // tq),
        in_specs=[
            pl.BlockSpec((1, H, tq, D), lambda b, i: (b, 0, i, 0)),
            pl.BlockSpec((1, H, NM, D), lambda b, i: (b, 0, 0, 0)),
            pl.BlockSpec((1, H, NM, D), lambda b, i: (b, 0, 0, 0)),
        ],
        out_specs=pl.BlockSpec((1, tq, M_WIDTH), lambda b, i: (b, i, 0)),
        out_shape=jax.ShapeDtypeStruct((B, T, M_WIDTH), F32),
        compiler_params=_cparams(("arbitrary", "arbitrary")),
        name="attn_m_prompt",
    )(q, k, v)


def _merge_kernel(x_ref, oa_ref, ob_ref, om_ref, g0_ref, g1_ref, g2_ref,
                  wa_ref, wb_ref, wm_ref, wo_ref, lg_ref, lb_ref, o_ref, *, alpha, precise):
    def prep(v):
        return v if precise else v.astype(BF16)

    ya = _dot(prep(oa_ref[...]), wa_ref[...], precise)
    yb = _dot(prep(ob_ref[...]), wb_ref[...], precise)
    ym = _dot(prep(om_ref[...]), wm_ref[...], precise)
    hmix = (jax.nn.sigmoid(g0_ref[...]) * ya + jax.nn.sigmoid(g1_ref[...]) * yb
            + jax.nn.sigmoid(g2_ref[...]) * ym)
    mix = _dot(prep(hmix), wo_ref[...], precise)
    o_ref[...] = _layer_norm(alpha * x_ref[...] + mix, lg_ref[...], lb_ref[...])


def merge(x, oa, ob, om, hproj, gate_col0, wa, wb, wm, wo, lg, lb, alpha, tm, precise=False,
          name="merge"):
    M, D = x.shape
    row = lambda i: (i, 0)
    const = lambda i: (0, 0)
    gspec = [pl.BlockSpec((tm, D), (lambda i, c=c: (i, gate_col0 + c))) for c in range(3)]
    return pl.pallas_call(
        functools.partial(_merge_kernel, alpha=alpha, precise=precise),
        grid=(M // tm,),
        in_specs=[pl.BlockSpec((tm, D), row),
                  pl.BlockSpec((tm, HEAD_COLS), row), pl.BlockSpec((tm, HEAD_COLS), row),
                  pl.BlockSpec((tm, HEAD_COLS), row)] + gspec + [
                  pl.BlockSpec((HEAD_COLS, D), const), pl.BlockSpec((HEAD_COLS, D), const),
                  pl.BlockSpec((HEAD_COLS, D), const), pl.BlockSpec((D, D), const),
                  pl.BlockSpec((1, D), const), pl.BlockSpec((1, D), const)],
        out_specs=pl.BlockSpec((tm, D), row),
        out_shape=jax.ShapeDtypeStruct((M, D), F32),
        compiler_params=_cparams(("arbitrary",)),
        name=name,
    )(x, oa, ob, om, hproj, hproj, hproj, wa, wb, wm, wo, lg, lb)


def _ffn_kernel(x_ref, rw_ref, wg_ref, wu_ref, wd_ref, lg_ref, lb_ref, o_ref,
                acc_ref, gate_ref, *, alpha, routed, precise, n_experts):
    e = pl.program_id(1)
    f = pl.program_id(2)
    first = jnp.logical_and(e == 0, f == 0)
    last = jnp.logical_and(e == pl.num_programs(1) - 1, f == pl.num_programs(2) - 1)
    x = x_ref[...]

    @pl.when(first)
    def _():
        acc_ref[...] = jnp.zeros_like(acc_ref)
        if routed:
            logits = _dot(x, rw_ref[...], True)
            i1, i2, w1, w2 = _top2(logits, n_experts)
            lane = lax.broadcasted_iota(jnp.int32, logits.shape, 1)
            gate_ref[...] = jnp.where(lane == i1, w1, 0.0) + jnp.where(lane == i2, w2, 0.0)

    xin = x if precise else x.astype(BF16)
    hg = _dot(xin, wg_ref[0], precise)
    hu = _dot(xin, wu_ref[0], precise)
    hh = hg * jax.nn.sigmoid(hg) * hu
    if routed:
        gate = gate_ref[...]
        lane = lax.broadcasted_iota(jnp.int32, gate.shape, 1)
        hh = hh * jnp.sum(jnp.where(lane == e, gate, 0.0), axis=-1, keepdims=True)
    acc_ref[...] += _dot(hh if precise else hh.astype(BF16), wd_ref[0], precise)

    @pl.when(last)
    def _():
        o_ref[...] = _layer_norm(alpha * x + acc_ref[...], lg_ref[...], lb_ref[...])


def ffn(x, rw, wg, wu, wd, lg, lb, alpha, tm, tf, routed, precise=False, name="ffn"):
    M, D = x.shape
    E, _, F = wg.shape
    assert M % tm == 0 and F % tf == 0
    return pl.pallas_call(
        functools.partial(_ffn_kernel, alpha=alpha, routed=routed, precise=precise, n_experts=E),
        grid=(M // tm, E, F // tf),
        in_specs=[pl.BlockSpec((tm, D), lambda i, e, f: (i, 0)),
                  pl.BlockSpec(rw.shape, lambda i, e, f: (0, 0)),
                  pl.BlockSpec((1, D, tf), lambda i, e, f: (e, 0, f)),
                  pl.BlockSpec((1, D, tf), lambda i, e, f: (e, 0, f)),
                  pl.BlockSpec((1, tf, D), lambda i, e, f: (e, f, 0)),
                  pl.BlockSpec((1, D), lambda i, e, f: (0, 0)),
                  pl.BlockSpec((1, D), lambda i, e, f: (0, 0))],
        out_specs=pl.BlockSpec((tm, D), lambda i, e, f: (i, 0)),
        out_shape=jax.ShapeDtypeStruct((M, D), F32),
        scratch_shapes=[pltpu.VMEM((tm, D), F32), pltpu.VMEM((tm, rw.shape[1]), F32)],
        compiler_params=_cparams(("arbitrary", "arbitrary", "arbitrary")),
        name=name,
    )(x, rw, wg, wu, wd, lg, lb)


MOE_ROW_TILE = 512
SC_WINDOW = 128
SC_ROW = 128


def _top2(logits, n_experts):
    ne = logits.shape[-1]
    lane = lax.broadcasted_iota(jnp.int32, logits.shape, 1)
    logits = jnp.where(lane < n_experts, logits, -jnp.inf)
    m1 = jnp.max(logits, axis=-1, keepdims=True)
    i1 = jnp.min(jnp.where(logits == m1, lane, ne), axis=-1, keepdims=True)
    rest = jnp.where(lane == i1, -jnp.inf, logits)
    m2 = jnp.max(rest, axis=-1, keepdims=True)
    i2 = jnp.min(jnp.where(rest == m2, lane, ne), axis=-1, keepdims=True)
    e2 = jnp.exp(m2 - m1)
    return i1, i2, 1.0 / (1.0 + e2), e2 / (1.0 + e2)


def _route_kernel(x_ref, rw_ref, low_ref, e_ref, r_ref, w_ref, cnt_ref, run_ref, *, n_experts):
    @pl.when(pl.program_id(0) == 0)
    def _():
        run_ref[...] = jnp.zeros_like(run_ref)

    logits = _dot(x_ref[...], rw_ref[...], True)
    i1, i2, w1, w2 = _top2(logits, n_experts)
    lane = lax.broadcasted_iota(jnp.int32, logits.shape, 1)
    oh1 = jnp.where(lane == i1, 1.0, 0.0)
    oh2 = jnp.where(lane == i2, 1.0, 0.0)
    oh = oh1 + oh2
    pre = _dot(low_ref[...], oh.astype(BF16)) + run_ref[...]
    r1 = jnp.sum(oh1 * pre, axis=-1, keepdims=True)
    r2 = jnp.sum(oh2 * pre, axis=-1, keepdims=True)
    run_ref[...] = run_ref[...] + jnp.sum(oh, axis=0, keepdims=True)
    first = lax.broadcasted_iota(jnp.int32, e_ref.shape, 1) == 0
    e_ref[...] = jnp.where(first, i1, i2)
    r_ref[...] = jnp.where(first, r1, r2).astype(jnp.int32)
    w_ref[...] = jnp.where(first, w1, w2)
    cnt_ref[...] = run_ref[...]


def moe_route(x, rw, n_experts, tm=512):
    M, D = x.shape
    low = jnp.asarray(np.tril(np.ones((tm, tm), np.float32), -1), BF16)
    pair = pl.BlockSpec((tm, 2), lambda i: (i, 0))
    return pl.pallas_call(
        functools.partial(_route_kernel, n_experts=n_experts),
        grid=(M // tm,),
        in_specs=[pl.BlockSpec((tm, D), lambda i: (i, 0)),
                  pl.BlockSpec(rw.shape, lambda i: (0, 0)),
                  pl.BlockSpec((tm, tm), lambda i: (0, 0))],
        out_specs=[pair, pair, pair, pl.BlockSpec((1, 128), lambda i: (0, 0))],
        out_shape=[jax.ShapeDtypeStruct((M, 2), jnp.int32), jax.ShapeDtypeStruct((M, 2), jnp.int32),
                   jax.ShapeDtypeStruct((M, 2), F32), jax.ShapeDtypeStruct((1, 128), F32)],
        scratch_shapes=[pltpu.VMEM((1, 128), F32)],
        compiler_params=_cparams(("arbitrary",)),
        name="moe_route",
    )(x, rw, low)


def _gmm_kernel(te_ref, nu_ref, x_ref, wg_ref, wu_ref, wd_ref, o_ref, acc_ref):
    i = pl.program_id(0)
    f = pl.program_id(1)

    @pl.when(i < nu_ref[0])
    def _():
        @pl.when(f == 0)
        def _():
            acc_ref[...] = jnp.zeros_like(acc_ref)

        x = x_ref[...]
        hg = _dot(x, wg_ref[0])
        hu = _dot(x, wu_ref[0])
        hh = hg * jax.nn.sigmoid(hg) * hu
        acc_ref[...] += _dot(hh.astype(BF16), wd_ref[0])

        @pl.when(f == pl.num_programs(1) - 1)
        def _():
            o_ref[...] = acc_ref[...]


def moe_gmm(tile_expert, n_used, xs, wg, wu, wd, tf):
    R, D = xs.shape
    F = wg.shape[2]
    tr = MOE_ROW_TILE
    grid_spec = pltpu.PrefetchScalarGridSpec(
        num_scalar_prefetch=2,
        grid=(R // tr, F // tf),
        in_specs=[pl.BlockSpec((tr, D), lambda i, f, te, nu: (i, 0)),
                  pl.BlockSpec((1, D, tf), lambda i, f, te, nu: (te[i], 0, f)),
                  pl.BlockSpec((1, D, tf), lambda i, f, te, nu: (te[i], 0, f)),
                  pl.BlockSpec((1, tf, D), lambda i, f, te, nu: (te[i], f, 0))],
        out_specs=pl.BlockSpec((tr, D), lambda i, f, te, nu: (i, 0)),
        scratch_shapes=[pltpu.VMEM((tr, D), F32)],
    )
    return pl.pallas_call(
        _gmm_kernel,
        grid_spec=grid_spec,
        out_shape=jax.ShapeDtypeStruct((R, D), F32),
        compiler_params=_cparams(("arbitrary", "arbitrary")),
        name="moe_gmm",
    )(tile_expert, n_used, xs, wg, wu, wd)


def _combine_kernel(x_ref, o1_ref, o2_ref, w_ref, lg_ref, lb_ref, o_ref, *, alpha):
    w = w_ref[...]
    y = w[:, 0:1] * o1_ref[...] + w[:, 1:2] * o2_ref[...]
    o_ref[...] = _layer_norm(alpha * x_ref[...] + y, lg_ref[...], lb_ref[...])


def moe_combine(x, og, w12, lg, lb, alpha, tm=512):
    M, D = x.shape
    nb = M // tm
    return pl.pallas_call(
        functools.partial(_combine_kernel, alpha=alpha),
        grid=(nb,),
        in_specs=[pl.BlockSpec((tm, D), lambda i: (i, 0)),
                  pl.BlockSpec((tm, D), lambda i: (i, 0)),
                  pl.BlockSpec((tm, D), lambda i: (i + nb, 0)),
                  pl.BlockSpec((tm, 2), lambda i: (i, 0)),
                  pl.BlockSpec((1, D), lambda i: (0, 0)),
                  pl.BlockSpec((1, D), lambda i: (0, 0))],
        out_specs=pl.BlockSpec((tm, D), lambda i: (i, 0)),
        out_shape=jax.ShapeDtypeStruct((M, D), F32),
        compiler_params=_cparams(("arbitrary",)),
        name="moe_combine",
    )(x, og, og, w12, lg, lb)


def _sc_mesh():
    return plsc.VectorSubcoreMesh(core_axis_name="c", subcore_axis_name="s")


def sc_scatter_rows(src, idx, n_out_rows):
    S = src.shape[0]
    M = idx.shape[0]
    nb = S // SC_WINDOW
    assert S % SC_WINDOW == 0 and M % S == 0

    @pl.kernel(out_type=jax.ShapeDtypeStruct((n_out_rows, SC_ROW), src.dtype), mesh=_sc_mesh(),
               scratch_types=[])
    def scatter_kernel(x_hbm, i_hbm, o_hbm):
        def body(x_vmem, i_vmem):
            pltpu.sync_copy(x_vmem, o_hbm.at[i_vmem.at[0]])

        pltpu.emit_pipeline(
            body,
            grid=(M // SC_WINDOW,),
            in_specs=[pl.BlockSpec((SC_WINDOW, SC_ROW), lambda i: (lax.rem(i, nb), 0)),
                      pl.BlockSpec((1, SC_WINDOW), lambda i: (0, i))],
            out_specs=[],
            core_axis_name=("c", "s"),
            dimension_semantics=(pltpu.PARALLEL,),
        )(x_hbm, i_hbm)

    return scatter_kernel(src, idx.reshape(1, M))


def sc_gather_rows(table, idx):
    M = idx.shape[0]
    assert M % SC_WINDOW == 0

    @pl.kernel(out_type=jax.ShapeDtypeStruct((M, SC_ROW), table.dtype), mesh=_sc_mesh(),
               scratch_types=[])
    def gather_kernel(t_hbm, i_hbm, o_hbm):
        def body(i_vmem, o_vmem):
            pltpu.sync_copy(t_hbm.at[i_vmem.at[0]], o_vmem)

        pltpu.emit_pipeline(
            body,
            grid=(M // SC_WINDOW,),
            in_specs=[pl.BlockSpec((1, SC_WINDOW), lambda i: (0, i))],
            out_specs=[pl.BlockSpec((SC_WINDOW, SC_ROW), lambda i: (i, 0))],
            core_axis_name=("c", "s"),
            dimension_semantics=(pltpu.PARALLEL,),
        )(i_hbm, o_hbm)

    return gather_kernel(table, idx.reshape(1, M))


def moe_sparse(x, rw, wg, wu, wd, lg, lb, alpha, n_experts):
    N, D = x.shape
    tr = MOE_ROW_TILE
    e12, r12, w12, counts = moe_route(x, rw, n_experts)
    cnt = counts[0, :n_experts].astype(jnp.int32)
    cnt_pad = (cnt + tr - 1) // tr * tr
    off_end = jnp.cumsum(cnt_pad)
    off = off_end - cnt_pad
    R = 2 * N + n_experts * tr
    eids = jnp.arange(n_experts, dtype=jnp.int32)
    pos = jnp.sum(jnp.where(e12[..., None] == eids, off, 0), axis=-1) + r12
    tile_start = jnp.arange(R // tr, dtype=jnp.int32) * tr
    tile_expert = jnp.minimum(jnp.sum(tile_start[:, None] >= off_end[None, :], axis=1),
                              n_experts - 1).astype(jnp.int32)
    n_used = (off_end[-1:] // tr).astype(jnp.int32)
    pos_flat = pos.T.reshape(2 * N)
    wpr = D // 2 // SC_ROW
    xw = lax.bitcast_convert_type(x.astype(BF16).reshape(N, D // 2, 2), jnp.int32)
    idx_s = (pos_flat[:, None] * wpr + jnp.arange(wpr, dtype=jnp.int32)).reshape(-1)
    xs = sc_scatter_rows(xw.reshape(N * wpr, SC_ROW), idx_s, R * wpr)
    xs = lax.bitcast_convert_type(xs.reshape(R, D // 2), BF16).reshape(R, D)
    o_sorted = moe_gmm(tile_expert, n_used, xs, wg, wu, wd, 256)
    fpr = D // SC_ROW
    idx_g = (pos_flat[:, None] * fpr + jnp.arange(fpr, dtype=jnp.int32)).reshape(-1)
    og = sc_gather_rows(o_sorted.reshape(R * fpr, SC_ROW), idx_g).reshape(2 * N, D)
    return moe_combine(x, og, w12, lg, lb, alpha)


IDX_PAGES_PER_STEP = 32
ATTN_PAGES_PER_STEP = 16


def _page_specs(layer, n, rows, P):
    return [pl.BlockSpec((1, 1, rows, PAGE), (lambda b, s, pt, j=j: (layer, pt[b, s * P + j], 0, 0)))
            for j in range(n)]


def _idx_s_kernel(pt_ref, iqT_ref, iw_ref, iknew_ref, tri_ref, low_ref, *refs, topk, P):
    pages = refs[:P]
    am_ref, amnew_ref, s_ref = refs[P:]
    step = pl.program_id(1)
    iqT = iqT_ref[0] * (D_IDX ** -0.5)
    wi = iw_ref[0] * (H_IDX ** -0.5)
    for j in range(P):
        kt = pages[j][0, 0]
        sc = jnp.zeros((1, PAGE), F32)
        for h in range(H_IDX):
            d = jnp.sum(kt * iqT[:, h:h + 1], axis=0, keepdims=True)
            sc = sc + wi[:, h:h + 1] * jnp.maximum(d, 0.0)
        s_ref[pl.ds(step * P + j, 1), :] = _orderable(sc + 0.0)

    @pl.when(step == pl.num_programs(1) - 1)
    def _():
        s = s_ref[...]
        dn = jnp.sum(iqT * iknew_ref[0], axis=0, keepdims=True)
        sn = jnp.sum(wi * jnp.maximum(dn, 0.0), axis=-1, keepdims=True) + 0.0
        kn = _orderable(sn)

        def total(x):
            return jnp.sum(jnp.sum(x, axis=-1, keepdims=True), axis=0, keepdims=True)

        def count_ge(cand):
            return total(jnp.where(s >= cand, 1.0, 0.0)) + jnp.where(kn >= cand, 1.0, 0.0)

        thr = _kth_largest(count_ge, (1, 1), topk)
        c_gt = total(jnp.where(s > thr, 1.0, 0.0)) + jnp.where(kn > thr, 1.0, 0.0)
        need = topk - c_gt
        tie = s == thr
        tie_f = jnp.where(tie, 1.0, 0.0)
        in_row = _dot(tie_f.astype(BF16), tri_ref[...])
        row_tot = jnp.broadcast_to(jnp.sum(tie_f, axis=-1, keepdims=True), tie_f.shape)
        before = _dot(low_ref[...], row_tot.astype(BF16))
        keep_tie = jnp.where(in_row + before < need, 0.0, NEG)
        am_ref[0] = jnp.where(s > thr, 0.0, jnp.where(tie, keep_tie, NEG))
        keep_new = jnp.where(total(tie_f) < need, 0.0, NEG)
        amn = jnp.where(kn > thr, 0.0, jnp.where(kn == thr, keep_new, NEG))
        amnew_ref[0] = jnp.broadcast_to(amn, amnew_ref.shape[1:])


def idx_sample(page_table, layer, iqT, iw, ik_new, poolT, topk):
    DB, npg = page_table.shape
    P = min(IDX_PAGES_PER_STEP, npg)
    assert npg % P == 0
    tri = jnp.asarray(np.triu(np.ones((PAGE, PAGE), np.float32), 1), BF16)
    low = jnp.asarray(np.tril(np.ones((npg, npg), np.float32), -1), BF16)
    per_b = lambda b, s, pt: (b, 0, 0)
    grid_spec = pltpu.PrefetchScalarGridSpec(
        num_scalar_prefetch=1,
        grid=(DB, npg // P),
        in_specs=[pl.BlockSpec((1, D_IDX, H_IDX), per_b),
                  pl.BlockSpec((1, 1, H_IDX), per_b),
                  pl.BlockSpec((1, D_IDX, 1), per_b),
                  pl.BlockSpec((PAGE, PAGE), lambda b, s, pt: (0, 0)),
                  pl.BlockSpec((npg, npg), lambda b, s, pt: (0, 0))]
                 + _page_specs(layer, P, D_IDX, P),
        out_specs=[pl.BlockSpec((1, npg, PAGE), per_b),
                   pl.BlockSpec((1, 1, PAGE), per_b)],
        scratch_shapes=[pltpu.VMEM((npg, PAGE), jnp.int32)],
    )
    return pl.pallas_call(
        functools.partial(_idx_s_kernel, topk=float(topk), P=P),
        grid_spec=grid_spec,
        out_shape=[jax.ShapeDtypeStruct((DB, npg, PAGE), F32),
                   jax.ShapeDtypeStruct((DB, 1, PAGE), F32)],
        compiler_params=_cparams(("arbitrary", "arbitrary")),
        name="idx_sample",
    )(page_table, iqT, iw, ik_new, tri, low, *([poolT] * P))


def _expand_rows(x, rep):
    g, n = x.shape
    return jnp.broadcast_to(x[:, None, :], (g, rep, n)).reshape(g * rep, n)


def _group_logits(kt, qb):
    w, n = kt.shape
    return jnp.sum((kt * qb).reshape(N_GROUPS, w // N_GROUPS, n), axis=1)


def _dec_a_kernel(pt_ref, q_ref, am_ref, amnew_ref, knew_ref, vnew_ref, blast_ref, bnew_ref,
                  *refs, scale, P):
    kp = refs[:P]
    vp = refs[P:2 * P]
    o_ref, qb_ref, m_ref, l_ref, acc_ref = refs[2 * P:]
    step = pl.program_id(1)
    is_last = step == pl.num_programs(1) - 1
    W = qb_ref.shape[0]
    rep = W // N_GROUPS

    @pl.when(step == 0)
    def _():
        qb_ref[...] = jnp.broadcast_to(q_ref[0] * scale, qb_ref.shape)
        m_ref[...] = jnp.full_like(m_ref, NEG)
        l_ref[...] = jnp.zeros_like(l_ref)
        acc_ref[...] = jnp.zeros_like(acc_ref)

    last_f = jnp.where(is_last, 1.0, 0.0)
    for j in range(P):
        lg = _group_logits(kp[j][0, 0], qb_ref[...]) + am_ref[0, j:j + 1, :]
        if j == P - 1:
            lg = lg + last_f * blast_ref[...]
        m_old = m_ref[...]
        m_new = jnp.maximum(m_old, lg)
        alpha = jnp.exp(m_old - m_new)
        p = jnp.exp(lg - m_new)
        m_ref[...] = m_new
        l_ref[...] = l_ref[...] * alpha + p
        acc_ref[...] = acc_ref[...] * _expand_rows(alpha, rep) + _expand_rows(p, rep) * vp[j][0, 0]

    @pl.when(is_last)
    def _():
        q = q_ref[0] * scale
        lgn = (jnp.sum((knew_ref[0] * q).reshape(N_GROUPS, rep, 1), axis=1)
               + bnew_ref[...] + amnew_ref[0][:, 0:1])
        m = m_ref[...]
        mx = jnp.maximum(jnp.max(m, axis=-1, keepdims=True), lgn)
        w = jnp.exp(m - mx)
        pn = jnp.exp(lgn - mx)
        l = jnp.sum(l_ref[...] * w, axis=-1, keepdims=True) + pn
        o = (jnp.sum(acc_ref[...] * _expand_rows(w, rep), axis=-1, keepdims=True)
             + _expand_rows(pn, rep) * vnew_ref[0])
        o_ref[0] = o / _expand_rows(l, rep)


def decode_a(page_table, layer, qcol, am, amnew, knew, vnew, blast, bnew, kT, vT, scale):
    DB, npg = page_table.shape
    W = qcol.shape[1]
    P = min(ATTN_PAGES_PER_STEP, npg)
    assert npg % P == 0
    per_b = lambda b, s, pt: (b, 0, 0)
    c2 = lambda b, s, pt: (0, 0)
    grid_spec = pltpu.PrefetchScalarGridSpec(
        num_scalar_prefetch=1,
        grid=(DB, npg // P),
        in_specs=[pl.BlockSpec((1, W, 1), per_b),
                  pl.BlockSpec((1, P, PAGE), lambda b, s, pt: (b, s, 0)),
                  pl.BlockSpec((1, 1, PAGE), per_b),
                  pl.BlockSpec((1, W, 1), per_b),
                  pl.BlockSpec((1, W, 1), per_b),
                  pl.BlockSpec((N_GROUPS, PAGE), c2),
                  pl.BlockSpec((N_GROUPS, 1), c2)]
                 + _page_specs(layer, P, W, P) + _page_specs(layer, P, W, P),
        out_specs=pl.BlockSpec((1, W, 1), per_b),
        scratch_shapes=[pltpu.VMEM((W, PAGE), F32), pltpu.VMEM((N_GROUPS, PAGE), F32),
                        pltpu.VMEM((N_GROUPS, PAGE), F32), pltpu.VMEM((W, PAGE), F32)],
    )
    return pl.pallas_call(
        functools.partial(_dec_a_kernel, scale=scale, P=P),
        grid_spec=grid_spec,
        out_shape=jax.ShapeDtypeStruct((DB, W, 1), F32),
        compiler_params=_cparams(("arbitrary", "arbitrary")),
        name="decode_a",
    )(page_table, qcol, am, amnew, knew, vnew, blast, bnew, *([kT] * P), *([vT] * P))


def _dec_b_kernel(pt_ref, q_ref, knew_ref, vnew_ref, blast_ref, bnew_ref, r_ref, pm_ref,
                  lam_ref, g_ref, *refs, scale, lam_init, P):
    kp = refs[:P]
    vp = refs[P:2 * P]
    o_ref, qb_ref, m_ref, l_ref, acc_ref = refs[2 * P:]
    step = pl.program_id(1)
    is_last = step == pl.num_programs(1) - 1
    W = qb_ref.shape[0]
    rep = W // N_GROUPS

    @pl.when(step == 0)
    def _():
        qb_ref[...] = jnp.broadcast_to(q_ref[0] * scale, qb_ref.shape)
        m_ref[...] = jnp.full_like(m_ref, NEG)
        l_ref[...] = jnp.zeros_like(l_ref)
        acc_ref[...] = jnp.zeros_like(acc_ref)

    last_f = jnp.where(is_last, 1.0, 0.0)
    lgs = [_group_logits(kp[j][0, 0], qb_ref[...]) for j in range(P)]
    lgs[P - 1] = lgs[P - 1] + last_f * blast_ref[...]
    m_old = m_ref[...]
    m_new = m_old
    for lg in lgs:
        m_new = jnp.maximum(m_new, jnp.max(lg, axis=-1, keepdims=True))
    alpha = jnp.exp(m_old - m_new)
    ps = [jnp.exp(lg - m_new) for lg in lgs]
    m_ref[...] = m_new
    l_ref[...] = l_ref[...] * alpha + sum(jnp.sum(p, axis=-1, keepdims=True) for p in ps)
    pr = _dot(jnp.concatenate(ps, axis=0).astype(BF16), r_ref[...])
    o = jnp.zeros(acc_ref.shape, F32)
    for j in range(P):
        prj = (pr[j * N_GROUPS:(j + 1) * N_GROUPS] * pm_ref[...]).astype(BF16)
        o = o + _dot(prj, vp[j][0, 0].astype(BF16))
    acc_ref[...] = acc_ref[...] * alpha + o

    @pl.when(is_last)
    def _():
        q = q_ref[0] * scale
        lgn = jnp.sum((knew_ref[0] * q).reshape(N_GROUPS, rep, 1), axis=1) + bnew_ref[...]
        m = m_ref[...]
        mx = jnp.maximum(m, lgn)
        a2 = jnp.exp(m - mx)
        pn = jnp.exp(lgn - mx)
        l = l_ref[...] * a2 + pn
        on = (acc_ref[...] * a2 + pn * vnew_ref[0]) / l
        lam = _lambda(lam_ref[...], lam_init)
        r = lax.broadcasted_iota(jnp.int32, (N_GROUPS, N_GROUPS), 0)
        c = lax.broadcasted_iota(jnp.int32, (N_GROUPS, N_GROUPS), 1)
        comb = jnp.where(c == 2 * r, 1.0, 0.0) - lam * jnp.where(c == 2 * r + 1, 1.0, 0.0)
        o = _dot(comb, on, True)
        o = o * lax.rsqrt(jnp.mean(o * o, axis=-1, keepdims=True) + SUBLN_EPS)
        o_ref[0] = o * g_ref[...] * (1.0 - lam_init)


def decode_b(page_table, layer, qcol, knew, vnew_g, blast, bnew, rmat, pmask, lamv, g, kT, vrows,
             scale, lam_init):
    DB, npg = page_table.shape
    W = qcol.shape[1]
    P = min(ATTN_PAGES_PER_STEP, npg)
    dv = 2 * D_B
    assert npg % P == 0
    per_b = lambda b, s, pt: (b, 0, 0)
    c2 = lambda b, s, pt: (0, 0)
    grid_spec = pltpu.PrefetchScalarGridSpec(
        num_scalar_prefetch=1,
        grid=(DB, npg // P),
        in_specs=[pl.BlockSpec((1, W, 1), per_b),
                  pl.BlockSpec((1, W, 1), per_b),
                  pl.BlockSpec((1, N_GROUPS, dv), per_b),
                  pl.BlockSpec((N_GROUPS, PAGE), c2),
                  pl.BlockSpec((N_GROUPS, 1), c2),
                  pl.BlockSpec((PAGE, W), c2),
                  pl.BlockSpec((N_GROUPS, W), c2),
                  pl.BlockSpec((4, D_B), c2),
                  pl.BlockSpec((1, dv), c2)]
                 + _page_specs(layer, P, W, P) + _page_specs(layer, P, W, P),
        out_specs=pl.BlockSpec((1, N_GROUPS, dv), per_b),
        scratch_shapes=[pltpu.VMEM((W, PAGE), F32), pltpu.VMEM((N_GROUPS, 1), F32),
                        pltpu.VMEM((N_GROUPS, 1), F32), pltpu.VMEM((N_GROUPS, dv), F32)],
    )
    return pl.pallas_call(
        functools.partial(_dec_b_kernel, scale=scale, lam_init=lam_init, P=P),
        grid_spec=grid_spec,
        out_shape=jax.ShapeDtypeStruct((DB, N_GROUPS, dv), F32),
        compiler_params=_cparams(("arbitrary", "arbitrary")),
        name="decode_b",
    )(page_table, qcol, knew, vnew_g, blast, bnew, rmat, pmask, lamv, g,
      *([kT] * P), *([vrows] * P))


def _dec_m_kernel(q_ref, k_ref, v_ref, o_ref):
    n = k_ref.shape[2] // 8
    q = q_ref[0]
    k3 = k_ref[0, 0].reshape(n, 8, D_M)
    lg = jnp.sum(k3 * q[None], axis=-1, keepdims=True)
    m8 = jnp.max(lg, axis=0)
    m4 = jnp.maximum(m8[:H_M], m8[H_M:])
    p = jnp.exp(lg - jnp.concatenate([m4, m4], axis=0)[None])
    l8 = jnp.sum(p, axis=0)
    o8 = jnp.sum(p * v_ref[0, 0].reshape(n, 8, D_M), axis=0)
    o_ref[0] = (o8[:H_M] + o8[H_M:]) / (l8[:H_M] + l8[H_M:])


def decode_m(layer, q8, mem_k, mem_v):
    DB = q8.shape[0]
    rows = mem_k.shape[2]
    return pl.pallas_call(
        _dec_m_kernel,
        grid=(DB,),
        in_specs=[pl.BlockSpec((1, 8, D_M), lambda b: (b, 0, 0)),
                  pl.BlockSpec((1, 1, rows, D_M), lambda b: (layer, b, 0, 0)),
                  pl.BlockSpec((1, 1, rows, D_M), lambda b: (layer, b, 0, 0))],
        out_specs=pl.BlockSpec((1, H_M, D_M), lambda b: (b, 0, 0)),
        out_shape=jax.ShapeDtypeStruct((DB, H_M, D_M), F32),
        compiler_params=_cparams(("arbitrary",)),
        name="decode_m",
    )(q8, mem_k, mem_v)


def _t5_bucket_np(rel):
    n = np.maximum(rel, 0)
    max_exact = NUM_BUCKETS // 2
    nf = np.maximum(n, 1).astype(np.float32)
    large = max_exact + (np.log(nf / np.float32(max_exact))
                         / np.float32(math.log(MAX_DISTANCE / max_exact))
                         * np.float32(NUM_BUCKETS - max_exact)).astype(np.int32)
    large = np.minimum(large, NUM_BUCKETS - 1)
    return np.where(n < max_exact, n, large).astype(np.int32)


def _far_bucket(first_far, last_far):
    b = _t5_bucket_np(np.arange(first_far, last_far + 1))
    assert (b == b[0]).all(), "relative-position bias must be constant beyond one tile"
    return int(b[0])


def _bias_lookup(rel_bias, buckets):
    onehot = jax.nn.one_hot(jnp.asarray(buckets, jnp.int32), NUM_BUCKETS, dtype=F32)
    return jnp.einsum('...b,bh->...h', onehot, rel_bias, precision=HIGHEST)


def _band_tables(rel_bias, heads, t_max):
    far = _far_bucket(TQ + 1, max(t_max, TQ + 1))
    r = np.arange(TQ)[None, :] - np.arange(TQ)[:, None]
    tabs = []
    for d in range(2):
        rel = r + d * TQ
        vals = _bias_lookup(rel_bias, _t5_bucket_np(rel))[..., heads] - rel_bias[far][heads]
        vals = jnp.where((rel >= 0)[..., None], vals, NEG)
        tabs.append(jnp.transpose(vals, (2, 0, 1)))
    return jnp.stack(tabs, axis=1).astype(F32)


def _split3(x, order):
    hi = x.astype(BF16)
    lo = (x - hi.astype(F32)).astype(BF16)
    return jnp.concatenate([(hi, lo)[o] for o in order], axis=-1)


def kernel(x_prompt, x_sample, cache_a_k, cache_a_v, cache_idx_k, cache_b_k, cache_b_v, cache_mem_k, cache_mem_v, page_table, mem_prompt, w_in, w_mem_kv, lambda_q1, lambda_k1, lambda_q2, lambda_k2, subln_g, w_branch_a, w_branch_b, w_branch_m, w_out, rel_bias, ln1_g, ln1_b, ln2_g, ln2_b, ffn_w_gate, ffn_w_up, ffn_w_down, router_w, expert_w_gate, expert_w_up, expert_w_down):
    B, T, D = x_prompt.shape
    DB, TS, _ = x_sample.shape
    depth = w_in.shape[0]
    n_pool = cache_a_k.shape[1]
    npg = page_table.shape[1]
    past = npg * PAGE
    n_mem = mem_prompt.shape[1]
    assert TS == 1 and T % TQ == 0 and cache_a_k.shape[2] == PAGE
    alpha = (2 * depth) ** 0.25
    topk_p = min(TOPK_MAX, T // 4)
    topk_s = min(TOPK_MAX, (past + TS) // 4)
    W = HEAD_COLS
    nq = T // TQ

    c_ik = 3 * A_WIDTH + H_IDX * D_IDX
    c_b = c_ik + D_IDX + H_IDX
    n_main = 8 * W + 3 * D
    tail_pad = 128 - D_IDX - H_IDX

    low = jnp.asarray(np.tril(np.ones((TQ, TQ), np.float32), -1), BF16)
    band_a = _band_tables(rel_bias, np.arange(H_A), T)
    band_b = _band_tables(rel_bias, H_A + np.arange(H_B), T)

    far_s = _far_bucket(PAGE + 1, past)
    b_last = (_bias_lookup(rel_bias, _t5_bucket_np(PAGE - np.arange(PAGE))) - rel_bias[far_s]).T
    b_new = (rel_bias[0] - rel_bias[far_s])[:, None]
    blast_a, bnew_a = b_last[:H_A], b_new[:H_A]
    blast_b, bnew_b = jnp.repeat(b_last[H_A:], 2, axis=0), jnp.repeat(b_new[H_A:], 2, axis=0)
    lane = np.arange(W)
    rmat = jnp.asarray(lane[None, :] // H_B == np.arange(PAGE)[:, None], BF16)
    pmask = jnp.asarray(lane[None, :] % H_B == np.arange(N_GROUPS)[:, None] // 2, F32)

    idxT = jnp.transpose(cache_idx_k, (0, 1, 3, 2))
    akT = jnp.transpose(cache_a_k, (0, 1, 3, 4, 2)).reshape(depth, n_pool, W, PAGE)
    avT = jnp.transpose(cache_a_v, (0, 1, 3, 4, 2)).reshape(depth, n_pool, W, PAGE)
    bkT = jnp.transpose(cache_b_k, (0, 1, 3, 4, 5, 2)).reshape(depth, n_pool, W, PAGE)
    bv_rows = cache_b_v.reshape(depth, n_pool, PAGE * H_B, 2 * D_B)
    memk_rows = cache_mem_k.reshape(depth, DB, n_mem * H_M, D_M)
    memv_rows = cache_mem_v.reshape(depth, DB, n_mem * H_M, D_M)

    outs = {n: [] for n in ('akp', 'avp', 'ikp', 'bkp', 'bvp', 'mkp', 'mvp',
                            'aks', 'avs', 'iks', 'bks', 'bvs')}
    x_p = x_prompt.reshape(B * T, D)
    x_s = x_sample.reshape(DB, D)
    for l in range(depth):
        lam_init = 0.8 - 0.6 * math.exp(-0.3 * l)
        lamv = jnp.stack([lambda_q1[l], lambda_k1[l], lambda_q2[l], lambda_k2[l]])
        wl = w_in[l]
        w_perm = jnp.concatenate([wl[:, :c_ik], wl[:, c_b:], wl[:, c_ik:c_b],
                                  jnp.zeros((D, tail_pad), F32)], axis=1)
        w_main_bf = w_perm[:, :n_main].astype(BF16)
        w_tail_bf = w_perm[:, n_main:].astype(BF16)
        wa_l, wb_l, wm_l, wo_l = w_branch_a[l], w_branch_b[l], w_branch_m[l], w_out[l]
        lg1, lb1 = ln1_g[l][None], ln1_b[l][None]
        lg2, lb2 = ln2_g[l][None], ln2_b[l][None]
        g_sub = subln_g[l][None]
        j = l // 2
        if l % 2 == 0:
            rw = jnp.zeros((D, 128), F32)
            wg, wu, wd = ffn_w_gate[j][None], ffn_w_up[j][None], ffn_w_down[j][None]
            routed = False
        else:
            rw = jnp.pad(router_w[j], ((0, 0), (0, 128 - N_EXPERTS)))
            wg, wu, wd = expert_w_gate[j], expert_w_up[j], expert_w_down[j]
            routed = True

        h = matmul(x_p, w_main_bf, 512, 1024, name="proj_prompt")
        tail = matmul(x_p, w_tail_bf, 512, 128, name="proj_tail")
        h3 = h.reshape(B, T, n_main)
        col = lambda c: h3[:, :, c * W:(c + 1) * W]

        def heads(c, nh, dh, dt):
            return col(c).reshape(B, T, nh, dh).transpose(0, 2, 1, 3).astype(dt)

        def q_t(c, scale):
            return jnp.transpose(col(c) * scale, (0, 2, 1)).astype(BF16)

        def v_t(c):
            return col(c).reshape(B, nq, TQ, W).transpose(0, 1, 3, 2).astype(BF16)

        a_k, a_v, b_k, b_v = col(1), col(2), col(5), col(6)
        i_k = tail[:, :D_IDX].reshape(B, T, D_IDX)
        i_w = tail[:, D_IDX:D_IDX + H_IDX].reshape(B, T, H_IDX)
        iq = col(3).reshape(B, T, H_IDX, D_IDX) * (D_IDX ** -0.5)
        iqT = jnp.transpose(_split3(iq, (0, 0, 1)), (0, 2, 3, 1))
        ik3 = _split3(i_k, (0, 1, 0))
        wT = jnp.transpose(i_w * (H_IDX ** -0.5), (0, 2, 1))
        o_a = attn_a_prompt(q_t(0, D_A ** -0.5), heads(1, H_A, D_A, BF16), v_t(2), iqT, ik3, wT,
                            band_a, low, topk_p)
        o_b = attn_b_prompt(q_t(4, D_B ** -0.5), heads(5, 2 * H_B, D_B, BF16), v_t(6),
                            band_b, lamv, g_sub.T, lam_init)
        kv = matmul(mem_prompt.reshape(B * n_mem, D), w_mem_kv[l].astype(BF16), 512, 1024,
                    name="proj_mem")
        mk = kv[:, :M_WIDTH].reshape(B, n_mem, H_M, D_M)
        mv = kv[:, M_WIDTH:].reshape(B, n_mem, H_M, D_M)
        qm = (heads(7, H_M, D_M, F32) * (D_M ** -0.5)).astype(BF16)
        o_m = attn_m_prompt(qm, mk.transpose(0, 2, 1, 3).astype(BF16),
                            mv.transpose(0, 2, 1, 3).astype(BF16))
        x_p = merge(x_p, o_a.reshape(B * T, W), o_b.reshape(B * T, W), o_m.reshape(B * T, W),
                    h, 8 * W // D, wa_l.astype(BF16), wb_l.astype(BF16), wm_l.astype(BF16),
                    wo_l.astype(BF16), lg1, lb1, alpha, 512, name="merge_prompt")
        x_p_mid = x_p
        outs['akp'].append(a_k.reshape(B, T, H_A, D_A))
        outs['avp'].append(a_v.reshape(B, T, H_A, D_A))
        outs['ikp'].append(i_k)
        outs['bkp'].append(b_k.reshape(B, T, H_B, 2, D_B))
        outs['bvp'].append(b_v.reshape(B, T, H_B, 2 * D_B))
        outs['mkp'].append(mk)
        outs['mvp'].append(mv)

        hs = matmul(x_s, w_perm[:, :n_main], DB, 1024, precise=True, name="proj_sample")
        ts = matmul(x_s, w_perm[:, n_main:], DB, 128, precise=True, name="proj_tail_sample")
        sa_q, sa_k, sa_v, s_iq = (hs[:, c * W:(c + 1) * W] for c in range(4))
        sb_q, sb_k, sb_v, sm_q = (hs[:, c * W:(c + 1) * W] for c in range(4, 8))
        s_ik = ts[:, :D_IDX]
        s_iw = ts[:, D_IDX:D_IDX + H_IDX]
        colv = lambda t: t.reshape(DB, -1, 1)
        am, amn = idx_sample(page_table, l, jnp.transpose(s_iq.reshape(DB, H_IDX, D_IDX), (0, 2, 1)),
                             s_iw.reshape(DB, 1, H_IDX), colv(s_ik), idxT, topk_s)
        o_a = decode_a(page_table, l, colv(sa_q), am, amn, colv(sa_k), colv(sa_v),
                       blast_a, bnew_a, akT, avT, D_A ** -0.5).reshape(DB, W)
        vnew_g = jnp.repeat(sb_v.reshape(DB, H_B, 2 * D_B), 2, axis=1)
        o_b = decode_b(page_table, l, colv(sb_q), colv(sb_k), vnew_g, blast_b, bnew_b, rmat, pmask,
                       lamv, g_sub, bkT, bv_rows, D_B ** -0.5, lam_init)[:, :H_B].reshape(DB, W)
        q8 = jnp.tile(sm_q.reshape(DB, H_M, D_M) * (D_M ** -0.5), (1, 2, 1))
        o_m = decode_m(l, q8, memk_rows, memv_rows).reshape(DB, W)
        x_s = merge(x_s, o_a, o_b, o_m, hs, 8 * W // D, wa_l, wb_l, wm_l, wo_l, lg1, lb1, alpha,
                    DB, precise=True, name="merge_sample")
        x_s = ffn(x_s, rw, wg, wu, wd, lg2, lb2, alpha, DB, 256, routed, precise=True,
                  name="ffn_sample")
        wg_bf, wu_bf, wd_bf = wg.astype(BF16), wu.astype(BF16), wd.astype(BF16)
        if routed:
            x_p = moe_sparse(x_p_mid, rw, wg_bf, wu_bf, wd_bf, lg2, lb2, alpha, N_EXPERTS)
        else:
            x_p = ffn(x_p_mid, rw, wg_bf, wu_bf, wd_bf, lg2, lb2, alpha,
                      1024 if B * T % 1024 == 0 else 512, 256, False, name="ffn_prompt")
        outs['aks'].append(sa_k.reshape(DB, TS, H_A, D_A))
        outs['avs'].append(sa_v.reshape(DB, TS, H_A, D_A))
        outs['iks'].append(s_ik.reshape(DB, TS, D_IDX))
        outs['bks'].append(sb_k.reshape(DB, TS, H_B, 2, D_B))
        outs['bvs'].append(sb_v.reshape(DB, TS, H_B, 2 * D_B))

    st = {n: jnp.stack(v) for n, v in outs.items()}
    return (x_p.reshape(B, T, D), x_s.reshape(DB, TS, D),
            st['akp'], st['avp'], st['ikp'], st['bkp'], st['bvp'], st['mkp'], st['mvp'],
            st['aks'], st['avs'], st['iks'], st['bks'], st['bvs'])
```

```python
import functools
import math

import jax
import jax.numpy as jnp
import numpy as np
from jax import lax
from jax.experimental import pallas as pl
from jax.experimental.pallas import tpu as pltpu
from jax.experimental.pallas import tpu_sc as plsc

F32 = jnp.float32
BF16 = jnp.bfloat16
HIGHEST = lax.Precision.HIGHEST

H_A, D_A = 8, 64
H_IDX, D_IDX = 8, 64
TOPK_MAX = 256
H_B, D_B = 4, 64
H_M, D_M = 4, 128
NUM_BUCKETS, MAX_DISTANCE = 32, 128
N_EXPERTS = 8
LN_EPS = 1e-5
SUBLN_EPS = 1e-5
A_WIDTH = H_A * D_A
B_WIDTH = H_B * 2 * D_B
M_WIDTH = H_M * D_M
HEAD_COLS = 512
PAGE = 128
N_GROUPS = 8

NEG = -1e30
KEY_NEG_INF = -2139095041
INT_MIN = -2147483648

VMEM_LIMIT = 56 * 1024 * 1024

TQ = 256
IDX_SPLIT = 3 * D_IDX


def _cparams(sem):
    return pltpu.CompilerParams(dimension_semantics=sem, vmem_limit_bytes=VMEM_LIMIT)


def _dot(a, b, precise=False):
    if precise:
        return jnp.dot(a, b, preferred_element_type=F32, precision=HIGHEST)
    return jnp.dot(a, b, preferred_element_type=F32)


def _dot_nt(a, b):
    return lax.dot_general(a, b, (((1,), (1,)), ((), ())), preferred_element_type=F32)


def _orderable(s):
    b = lax.bitcast_convert_type(s, jnp.int32)
    return jnp.where(b < 0, b ^ jnp.int32(0x7FFFFFFF), b)


def _layer_norm(v, g, b):
    mu = jnp.mean(v, axis=-1, keepdims=True)
    d = v - mu
    var = jnp.mean(d * d, axis=-1, keepdims=True)
    return d * lax.rsqrt(var + LN_EPS) * g + b


def _kth_largest(count_ge, shape, topk):
    t0 = jnp.full(shape, INT_MIN, jnp.int32)
    t0 = jnp.where(count_ge(jnp.zeros(shape, jnp.int32)) >= topk, jnp.zeros(shape, jnp.int32), t0)

    def bit_body(b, t):
        cand = t | jnp.left_shift(jnp.int32(1), 30 - b)
        return jnp.where(count_ge(cand) >= topk, cand, t)

    return lax.fori_loop(0, 31, bit_body, t0)


def _lambda(lv, lam_init):
    return (jnp.exp(jnp.sum(lv[0:1] * lv[1:2], axis=-1, keepdims=True))
            - jnp.exp(jnp.sum(lv[2:3] * lv[3:4], axis=-1, keepdims=True)) + lam_init)


def _mm_kernel(x_ref, w_ref, o_ref):
    o_ref[...] = _dot(x_ref[...].astype(BF16), w_ref[...])


def matmul(x, w, tm, tn, name="matmul"):
    M, K = x.shape
    N = w.shape[1]
    assert M % tm == 0 and N % tn == 0
    return pl.pallas_call(
        _mm_kernel,
        grid=(N // tn, M // tm),
        in_specs=[pl.BlockSpec((tm, K), lambda j, i: (i, 0)),
                  pl.BlockSpec((K, tn), lambda j, i: (0, j))],
        out_specs=pl.BlockSpec((tm, tn), lambda j, i: (i, j)),
        out_shape=jax.ShapeDtypeStruct((M, N), F32),
        compiler_params=_cparams(("arbitrary", "arbitrary")),
        name=name,
    )(x, w)


def _set_row(full, g, row):
    r = lax.broadcasted_iota(jnp.int32, full.shape, 0)
    return jnp.where(r == g, row, full)


def _attend_tile(kj, band_d, ml, *, n_groups, dv, qT_ref, k_ref, vT_ref, band_ref, am, acc_ref,
                 s_ref, p_ref, tq):
    m_all, l_all = ml
    ksl = pl.ds(pl.multiple_of(kj * tq, tq), tq)
    gpb = n_groups // band_ref.shape[0]
    gpv = (n_groups * dv) // vT_ref.shape[2]
    dq = qT_ref.shape[1] // n_groups
    m_out, l_out = m_all, l_all
    for g in range(n_groups):
        s_ref[g] = _dot(k_ref[0, g, ksl, :], qT_ref[0, g * dq:(g + 1) * dq, :])
    alphas = []
    for g in range(n_groups):
        s = s_ref[g]
        if am is not None:
            s = s + am
        if band_d is not None:
            s = s + band_ref[g // gpb, band_d]
        m_old = m_all[g:g + 1, :]
        m_new = jnp.maximum(m_old, jnp.max(s, axis=0, keepdims=True))
        alpha = jnp.exp(m_old - m_new)
        p = jnp.exp(s - m_new)
        p_ref[g] = p.astype(BF16)
        alphas.append(alpha)
        m_out = _set_row(m_out, g, m_new)
        l_out = _set_row(l_out, g, alpha * l_all[g:g + 1, :] + jnp.sum(p, axis=0, keepdims=True))
    for g in range(n_groups):
        vrows = slice((g // gpv) * dv, (g // gpv + 1) * dv)
        arows = slice(g * dv, (g + 1) * dv)
        acc_ref[arows, :] = alphas[g] * acc_ref[arows, :] + _dot(vT_ref[0, kj, vrows, :], p_ref[g])
    return m_out, l_out


def _attend_causal(i, tile_fn, n_groups, tq):
    ml = (jnp.full((n_groups, tq), NEG, F32), jnp.zeros((n_groups, tq), F32))
    ml = lax.fori_loop(0, jnp.maximum(i - 1, 0), lambda kj, c: tile_fn(kj, None, c), ml)
    ml = lax.cond(i >= 1, lambda c: tile_fn(i - 1, 1, c), lambda c: c, ml)
    return tile_fn(i, 0, ml)


def _attn_a_kernel(qT_ref, k_ref, vT_ref, iqT_ref, ik_ref, wT_ref, band_ref, low_ref,
                   o_ref, key_ref, am_ref, acc_ref, s_ref, p_ref, *, topk):
    i = pl.program_id(1)
    nk = i + 1
    tq = qT_ref.shape[2]
    qpos = i * tq + lax.broadcasted_iota(jnp.int32, (tq, tq), 1)
    krow = lax.broadcasted_iota(jnp.int32, (tq, tq), 0)

    def score_body(kj, c):
        ikt = ik_ref[0, pl.ds(pl.multiple_of(kj * tq, tq), tq), :]
        s = jnp.zeros((tq, tq), F32)
        for h in range(H_IDX):
            s = s + wT_ref[0, h:h + 1, :] * jnp.maximum(_dot(ikt, iqT_ref[0, h]), 0.0)
        s = jnp.where(kj * tq + krow <= qpos, s, -jnp.inf)
        key_ref[kj] = _orderable(s)
        return c

    lax.fori_loop(0, nk, score_body, 0)

    def count(pred):
        def body(kj, c):
            t = jnp.where(pred(key_ref[kj]), 1.0, 0.0)
            return c + jnp.sum(t.reshape(tq // 8, 8, tq), axis=0)
        c = lax.fori_loop(0, nk, body, jnp.zeros((8, tq), F32))
        return jnp.sum(c, axis=0, keepdims=True)

    thr = _kth_largest(lambda cand: count(lambda s: s >= cand), (1, tq), topk)
    c_gt = count(lambda s: s > thr)
    need = jnp.where(thr > KEY_NEG_INF, topk - c_gt, 0.0)

    def mask_body(kj, run):
        s = key_ref[kj]
        tie = s == thr
        tie_f = jnp.where(tie, 1.0, 0.0)
        pre = _dot(low_ref[...], tie_f.astype(BF16)) + run
        keep_tie = jnp.where(pre < need, 0.0, NEG)
        am_ref[kj] = jnp.where(s > thr, 0.0, jnp.where(tie, keep_tie, NEG))
        return run + jnp.sum(tie_f, axis=0, keepdims=True)

    lax.fori_loop(0, nk, mask_body, jnp.zeros((1, tq), F32))

    acc_ref[...] = jnp.zeros_like(acc_ref)

    def tile_fn(kj, band_d, ml):
        return _attend_tile(kj, band_d, ml, n_groups=H_A, dv=D_A, qT_ref=qT_ref, k_ref=k_ref,
                            vT_ref=vT_ref, band_ref=band_ref, am=am_ref[kj], acc_ref=acc_ref,
                            s_ref=s_ref, p_ref=p_ref, tq=tq)

    _, l_all = _attend_causal(i, tile_fn, H_A, tq)
    inv = 1.0 / l_all
    oT = jnp.concatenate([acc_ref[h * D_A:(h + 1) * D_A, :] * inv[h:h + 1, :] for h in range(H_A)],
                         axis=0)
    o_ref[0] = oT.T


def attn_a_prompt(qT, k, vT, iqT, ik, wT, band, low, topk):
    B, H, T, D = k.shape
    nq = T // TQ
    W = H * D
    return pl.pallas_call(
        functools.partial(_attn_a_kernel, topk=float(topk)),
        grid=(B, nq),
        in_specs=[
            pl.BlockSpec((1, W, TQ), lambda b, i: (b, 0, i)),
            pl.BlockSpec((1, H, T, D), lambda b, i: (b, 0, 0, 0)),
            pl.BlockSpec((1, nq, W, TQ), lambda b, i: (b, 0, 0, 0)),
            pl.BlockSpec((1, H_IDX, IDX_SPLIT, TQ), lambda b, i: (b, 0, 0, i)),
            pl.BlockSpec((1, T, IDX_SPLIT), lambda b, i: (b, 0, 0)),
            pl.BlockSpec((1, H_IDX, TQ), lambda b, i: (b, 0, i)),
            pl.BlockSpec((H, 2, TQ, TQ), lambda b, i: (0, 0, 0, 0)),
            pl.BlockSpec((TQ, TQ), lambda b, i: (0, 0)),
        ],
        out_specs=pl.BlockSpec((1, TQ, W), lambda b, i: (b, i, 0)),
        out_shape=jax.ShapeDtypeStruct((B, T, W), F32),
        scratch_shapes=[pltpu.VMEM((nq, TQ, TQ), jnp.int32),
                        pltpu.VMEM((nq, TQ, TQ), F32),
                        pltpu.VMEM((W, TQ), F32),
                        pltpu.VMEM((H, TQ, TQ), F32), pltpu.VMEM((H, TQ, TQ), BF16)],
        compiler_params=_cparams(("arbitrary", "arbitrary")),
        name="attn_a_prompt",
    )(qT, k, vT, iqT, ik, wT, band, low)


def _attn_b_kernel(qT_ref, k_ref, vT_ref, band_ref, lam_ref, g_ref, o_ref,
                   acc_ref, s_ref, p_ref, *, lam_init):
    i = pl.program_id(1)
    tq = qT_ref.shape[2]
    dv = 2 * D_B
    acc_ref[...] = jnp.zeros_like(acc_ref)

    def tile_fn(kj, band_d, ml):
        return _attend_tile(kj, band_d, ml, n_groups=2 * H_B, dv=dv, qT_ref=qT_ref, k_ref=k_ref,
                            vT_ref=vT_ref, band_ref=band_ref, am=None, acc_ref=acc_ref,
                            s_ref=s_ref, p_ref=p_ref, tq=tq)

    _, l_all = _attend_causal(i, tile_fn, 2 * H_B, tq)
    lam = _lambda(lam_ref[...], lam_init)
    inv = 1.0 / l_all
    parts = []
    for h in range(H_B):
        g0, g1 = 2 * h, 2 * h + 1
        o = (acc_ref[g0 * dv:(g0 + 1) * dv, :] * inv[g0:g0 + 1, :]
             - lam * (acc_ref[g1 * dv:(g1 + 1) * dv, :] * inv[g1:g1 + 1, :]))
        o = o * lax.rsqrt(jnp.mean(o * o, axis=0, keepdims=True) + SUBLN_EPS)
        parts.append(o * g_ref[...] * (1.0 - lam_init))
    o_ref[0] = jnp.concatenate(parts, axis=0).T


def attn_b_prompt(qT, k, vT, band, lamv, gcol, lam_init):
    B, G, T, D = k.shape
    nq = T // TQ
    W = G * D
    return pl.pallas_call(
        functools.partial(_attn_b_kernel, lam_init=lam_init),
        grid=(B, nq),
        in_specs=[
            pl.BlockSpec((1, W, TQ), lambda b, i: (b, 0, i)),
            pl.BlockSpec((1, G, T, D), lambda b, i: (b, 0, 0, 0)),
            pl.BlockSpec((1, nq, W, TQ), lambda b, i: (b, 0, 0, 0)),
            pl.BlockSpec((H_B, 2, TQ, TQ), lambda b, i: (0, 0, 0, 0)),
            pl.BlockSpec((4, D_B), lambda b, i: (0, 0)),
            pl.BlockSpec((2 * D_B, 1), lambda b, i: (0, 0)),
        ],
        out_specs=pl.BlockSpec((1, TQ, W), lambda b, i: (b, i, 0)),
        out_shape=jax.ShapeDtypeStruct((B, T, W), F32),
        scratch_shapes=[pltpu.VMEM((G * 2 * D_B, TQ), F32),
                        pltpu.VMEM((G, TQ, TQ), F32), pltpu.VMEM((G, TQ, TQ), BF16)],
        compiler_params=_cparams(("arbitrary", "arbitrary")),
        name="attn_b_prompt",
    )(qT, k, vT, band, lamv, gcol)


def _attn_m_kernel(q_ref, k_ref, v_ref, o_ref):
    for h in range(H_M):
        s = _dot_nt(q_ref[0, h], k_ref[0, h])
        m = jnp.max(s, axis=-1, keepdims=True)
        p = jnp.exp(s - m)
        l = jnp.sum(p, axis=-1, keepdims=True)
        o_ref[0, :, h * D_M:(h + 1) * D_M] = _dot(p.astype(BF16), v_ref[0, h]) / l


def attn_m_prompt(q, k, v):
    B, H, T, D = q.shape
    NM = k.shape[2]
    tq = 512
    return pl.pallas_call(
        _attn_m_kernel,
        grid=(B, T // tq),
        in_specs=[
            pl.BlockSpec((1, H, tq, D), lambda b, i: (b, 0, i, 0)),
            pl.BlockSpec((1, H, NM, D), lambda b, i: (b, 0, 0, 0)),
            pl.BlockSpec((1, H, NM, D), lambda b, i: (b, 0, 0, 0)),
        ],
        out_specs=pl.BlockSpec((1, tq, M_WIDTH), lambda b, i: (b, i, 0)),
        out_shape=jax.ShapeDtypeStruct((B, T, M_WIDTH), F32),
        compiler_params=_cparams(("arbitrary", "arbitrary")),
        name="attn_m_prompt",
    )(q, k, v)


def _merge_kernel(x_ref, oa_ref, ob_ref, om_ref, g0_ref, g1_ref, g2_ref,
                  wa_ref, wb_ref, wm_ref, wo_ref, lg_ref, lb_ref, o_ref, *, alpha):
    ya = _dot(oa_ref[...].astype(BF16), wa_ref[...])
    yb = _dot(ob_ref[...].astype(BF16), wb_ref[...])
    ym = _dot(om_ref[...].astype(BF16), wm_ref[...])
    hmix = (jax.nn.sigmoid(g0_ref[...]) * ya + jax.nn.sigmoid(g1_ref[...]) * yb
            + jax.nn.sigmoid(g2_ref[...]) * ym)
    mix = _dot(hmix.astype(BF16), wo_ref[...])
    o_ref[...] = _layer_norm(alpha * x_ref[...] + mix, lg_ref[...], lb_ref[...])


def merge(x, oa, ob, om, hproj, gate_col0, wa, wb, wm, wo, lg, lb, alpha, tm, name="merge"):
    M, D = x.shape
    row = lambda i: (i, 0)
    const = lambda i: (0, 0)
    gspec = [pl.BlockSpec((tm, D), (lambda i, c=c: (i, gate_col0 + c))) for c in range(3)]
    return pl.pallas_call(
        functools.partial(_merge_kernel, alpha=alpha),
        grid=(M // tm,),
        in_specs=[pl.BlockSpec((tm, D), row),
                  pl.BlockSpec((tm, HEAD_COLS), row), pl.BlockSpec((tm, HEAD_COLS), row),
                  pl.BlockSpec((tm, HEAD_COLS), row)] + gspec + [
                  pl.BlockSpec((HEAD_COLS, D), const), pl.BlockSpec((HEAD_COLS, D), const),
                  pl.BlockSpec((HEAD_COLS, D), const), pl.BlockSpec((D, D), const),
                  pl.BlockSpec((1, D), const), pl.BlockSpec((1, D), const)],
        out_specs=pl.BlockSpec((tm, D), row),
        out_shape=jax.ShapeDtypeStruct((M, D), F32),
        compiler_params=_cparams(("arbitrary",)),
        name=name,
    )(x, oa, ob, om, hproj, hproj, hproj, wa, wb, wm, wo, lg, lb)


def _ffn_kernel(x_ref, rw_ref, wg_ref, wu_ref, wd_ref, lg_ref, lb_ref, o_ref,
                acc_ref, gate_ref, *, alpha, routed, n_experts):
    e = pl.program_id(1)
    f = pl.program_id(2)
    first = jnp.logical_and(e == 0, f == 0)
    last = jnp.logical_and(e == pl.num_programs(1) - 1, f == pl.num_programs(2) - 1)
    x = x_ref[...]

    @pl.when(first)
    def _():
        acc_ref[...] = jnp.zeros_like(acc_ref)
        if routed:
            logits = _dot(x.astype(BF16), rw_ref[...])
            i1, i2, w1, w2 = _top2(logits, n_experts)
            lane = lax.broadcasted_iota(jnp.int32, logits.shape, 1)
            gate_ref[...] = jnp.where(lane == i1, w1, 0.0) + jnp.where(lane == i2, w2, 0.0)

    xin = x.astype(BF16)
    hg = _dot(xin, wg_ref[0])
    hu = _dot(xin, wu_ref[0])
    hh = hg * jax.nn.sigmoid(hg) * hu
    part = _dot(hh.astype(BF16), wd_ref[0])
    if routed:
        gate = gate_ref[...]
        lane = lax.broadcasted_iota(jnp.int32, gate.shape, 1)
        part = part * jnp.sum(jnp.where(lane == e, gate, 0.0), axis=-1, keepdims=True)
    acc_ref[...] += part

    @pl.when(last)
    def _():
        o_ref[...] = _layer_norm(alpha * x + acc_ref[...], lg_ref[...], lb_ref[...])


def ffn(x, rw, wg, wu, wd, lg, lb, alpha, tm, tf, routed, name="ffn"):
    M, D = x.shape
    E, _, F = wg.shape
    assert M % tm == 0 and F % tf == 0
    return pl.pallas_call(
        functools.partial(_ffn_kernel, alpha=alpha, routed=routed, n_experts=E),
        grid=(M // tm, E, F // tf),
        in_specs=[pl.BlockSpec((tm, D), lambda i, e, f: (i, 0)),
                  pl.BlockSpec(rw.shape, lambda i, e, f: (0, 0)),
                  pl.BlockSpec((1, D, tf), lambda i, e, f: (e, 0, f)),
                  pl.BlockSpec((1, D, tf), lambda i, e, f: (e, 0, f)),
                  pl.BlockSpec((1, tf, D), lambda i, e, f: (e, f, 0)),
                  pl.BlockSpec((1, D), lambda i, e, f: (0, 0)),
                  pl.BlockSpec((1, D), lambda i, e, f: (0, 0))],
        out_specs=pl.BlockSpec((tm, D), lambda i, e, f: (i, 0)),
        out_shape=jax.ShapeDtypeStruct((M, D), F32),
        scratch_shapes=[pltpu.VMEM((tm, D), F32), pltpu.VMEM((tm, rw.shape[1]), F32)],
        compiler_params=_cparams(("arbitrary", "arbitrary", "arbitrary")),
        name=name,
    )(x, rw, wg, wu, wd, lg, lb)


MOE_ROW_TILE = 512
SC_WINDOW = 128
SC_ROW = 128


def _top2(logits, n_experts):
    ne = logits.shape[-1]
    lane = lax.broadcasted_iota(jnp.int32, logits.shape, 1)
    logits = jnp.where(lane < n_experts, logits, -jnp.inf)
    m1 = jnp.max(logits, axis=-1, keepdims=True)
    i1 = jnp.min(jnp.where(logits == m1, lane, ne), axis=-1, keepdims=True)
    rest = jnp.where(lane == i1, -jnp.inf, logits)
    m2 = jnp.max(rest, axis=-1, keepdims=True)
    i2 = jnp.min(jnp.where(rest == m2, lane, ne), axis=-1, keepdims=True)
    e2 = jnp.exp(m2 - m1)
    return i1, i2, 1.0 / (1.0 + e2), e2 / (1.0 + e2)


def _route_kernel(x_ref, rw_ref, low_ref, e_ref, r_ref, w_ref, cnt_ref, run_ref, *, n_experts):
    @pl.when(pl.program_id(0) == 0)
    def _():
        run_ref[...] = jnp.zeros_like(run_ref)

    logits = _dot(x_ref[...].astype(BF16), rw_ref[...])
    i1, i2, w1, w2 = _top2(logits, n_experts)
    lane = lax.broadcasted_iota(jnp.int32, logits.shape, 1)
    oh1 = jnp.where(lane == i1, 1.0, 0.0)
    oh2 = jnp.where(lane == i2, 1.0, 0.0)
    oh = oh1 + oh2
    pre = _dot(low_ref[...], oh.astype(BF16)) + run_ref[...]
    r1 = jnp.sum(oh1 * pre, axis=-1, keepdims=True)
    r2 = jnp.sum(oh2 * pre, axis=-1, keepdims=True)
    run_ref[...] = run_ref[...] + jnp.sum(oh, axis=0, keepdims=True)
    first = lax.broadcasted_iota(jnp.int32, e_ref.shape, 1) == 0
    e_ref[...] = jnp.where(first, i1, i2)
    r_ref[...] = jnp.where(first, r1, r2).astype(jnp.int32)
    w_ref[...] = jnp.where(first, w1, w2)
    cnt_ref[...] = run_ref[...]


def moe_route(x, rw, n_experts, tm=512):
    M, D = x.shape
    low = jnp.asarray(np.tril(np.ones((tm, tm), np.float32), -1), BF16)
    pair = pl.BlockSpec((tm, 2), lambda i: (i, 0))
    return pl.pallas_call(
        functools.partial(_route_kernel, n_experts=n_experts),
        grid=(M // tm,),
        in_specs=[pl.BlockSpec((tm, D), lambda i: (i, 0)),
                  pl.BlockSpec(rw.shape, lambda i: (0, 0)),
                  pl.BlockSpec((tm, tm), lambda i: (0, 0))],
        out_specs=[pair, pair, pair, pl.BlockSpec((1, 128), lambda i: (0, 0))],
        out_shape=[jax.ShapeDtypeStruct((M, 2), jnp.int32), jax.ShapeDtypeStruct((M, 2), jnp.int32),
                   jax.ShapeDtypeStruct((M, 2), F32), jax.ShapeDtypeStruct((1, 128), F32)],
        scratch_shapes=[pltpu.VMEM((1, 128), F32)],
        compiler_params=_cparams(("arbitrary",)),
        name="moe_route",
    )(x, rw, low)


def _gmm_kernel(te_ref, nu_ref, x_ref, wg_ref, wu_ref, wd_ref, o_ref, acc_ref):
    i = pl.program_id(0)
    f = pl.program_id(1)

    @pl.when(i < nu_ref[0])
    def _():
        @pl.when(f == 0)
        def _():
            acc_ref[...] = jnp.zeros_like(acc_ref)

        x = x_ref[...].astype(BF16)
        hg = _dot(x, wg_ref[0])
        hu = _dot(x, wu_ref[0])
        hh = hg * jax.nn.sigmoid(hg) * hu
        acc_ref[...] += _dot(hh.astype(BF16), wd_ref[0])

        @pl.when(f == pl.num_programs(1) - 1)
        def _():
            o_ref[...] = acc_ref[...]


def moe_gmm(tile_expert, n_used, xs, wg, wu, wd, tf):
    R, D = xs.shape
    F = wg.shape[2]
    tr = MOE_ROW_TILE
    grid_spec = pltpu.PrefetchScalarGridSpec(
        num_scalar_prefetch=2,
        grid=(R // tr, F // tf),
        in_specs=[pl.BlockSpec((tr, D), lambda i, f, te, nu: (i, 0)),
                  pl.BlockSpec((1, D, tf), lambda i, f, te, nu: (te[i], 0, f)),
                  pl.BlockSpec((1, D, tf), lambda i, f, te, nu: (te[i], 0, f)),
                  pl.BlockSpec((1, tf, D), lambda i, f, te, nu: (te[i], f, 0))],
        out_specs=pl.BlockSpec((tr, D), lambda i, f, te, nu: (i, 0)),
        scratch_shapes=[pltpu.VMEM((tr, D), F32)],
    )
    return pl.pallas_call(
        _gmm_kernel,
        grid_spec=grid_spec,
        out_shape=jax.ShapeDtypeStruct((R, D), F32),
        compiler_params=_cparams(("arbitrary", "arbitrary")),
        name="moe_gmm",
    )(tile_expert, n_used, xs, wg, wu, wd)


def _combine_kernel(x_ref, o1_ref, o2_ref, w_ref, lg_ref, lb_ref, o_ref, *, alpha):
    w = w_ref[...]
    y = w[:, 0:1] * o1_ref[...] + w[:, 1:2] * o2_ref[...]
    o_ref[...] = _layer_norm(alpha * x_ref[...] + y, lg_ref[...], lb_ref[...])


def moe_combine(x, og, w12, lg, lb, alpha, tm=512):
    M, D = x.shape
    nb = M // tm
    return pl.pallas_call(
        functools.partial(_combine_kernel, alpha=alpha),
        grid=(nb,),
        in_specs=[pl.BlockSpec((tm, D), lambda i: (i, 0)),
                  pl.BlockSpec((tm, D), lambda i: (i, 0)),
                  pl.BlockSpec((tm, D), lambda i: (i + nb, 0)),
                  pl.BlockSpec((tm, 2), lambda i: (i, 0)),
                  pl.BlockSpec((1, D), lambda i: (0, 0)),
                  pl.BlockSpec((1, D), lambda i: (0, 0))],
        out_specs=pl.BlockSpec((tm, D), lambda i: (i, 0)),
        out_shape=jax.ShapeDtypeStruct((M, D), F32),
        compiler_params=_cparams(("arbitrary",)),
        name="moe_combine",
    )(x, og, og, w12, lg, lb)


def _sc_mesh():
    return plsc.VectorSubcoreMesh(core_axis_name="c", subcore_axis_name="s")


def sc_scatter_rows(src, idx, n_out_rows):
    S = src.shape[0]
    M = idx.shape[0]
    nb = S // SC_WINDOW
    assert S % SC_WINDOW == 0 and M % S == 0

    @pl.kernel(out_type=jax.ShapeDtypeStruct((n_out_rows, SC_ROW), src.dtype), mesh=_sc_mesh(),
               scratch_types=[])
    def scatter_kernel(x_hbm, i_hbm, o_hbm):
        def body(x_vmem, i_vmem):
            pltpu.sync_copy(x_vmem, o_hbm.at[i_vmem.at[0]])

        pltpu.emit_pipeline(
            body,
            grid=(M // SC_WINDOW,),
            in_specs=[pl.BlockSpec((SC_WINDOW, SC_ROW), lambda i: (lax.rem(i, nb), 0)),
                      pl.BlockSpec((1, SC_WINDOW), lambda i: (0, i))],
            out_specs=[],
            core_axis_name=("c", "s"),
            dimension_semantics=(pltpu.PARALLEL,),
        )(x_hbm, i_hbm)

    return scatter_kernel(src, idx.reshape(1, M))


def sc_gather_rows(table, idx):
    M = idx.shape[0]
    assert M % SC_WINDOW == 0

    @pl.kernel(out_type=jax.ShapeDtypeStruct((M, SC_ROW), table.dtype), mesh=_sc_mesh(),
               scratch_types=[])
    def gather_kernel(t_hbm, i_hbm, o_hbm):
        def body(i_vmem, o_vmem):
            pltpu.sync_copy(t_hbm.at[i_vmem.at[0]], o_vmem)

        pltpu.emit_pipeline(
            body,
            grid=(M // SC_WINDOW,),
            in_specs=[pl.BlockSpec((1, SC_WINDOW), lambda i: (0, i))],
            out_specs=[pl.BlockSpec((SC_WINDOW, SC_ROW), lambda i: (i, 0))],
            core_axis_name=("c", "s"),
            dimension_semantics=(pltpu.PARALLEL,),
        )(i_hbm, o_hbm)

    return gather_kernel(table, idx.reshape(1, M))


def moe_sparse(x, rw, wg, wu, wd, lg, lb, alpha, n_experts):
    N, D = x.shape
    tr = MOE_ROW_TILE
    e12, r12, w12, counts = moe_route(x, rw, n_experts)
    cnt = counts[0, :n_experts].astype(jnp.int32)
    cnt_pad = (cnt + tr - 1) // tr * tr
    off_end = jnp.cumsum(cnt_pad)
    off = off_end - cnt_pad
    R = 2 * N + n_experts * tr
    eids = jnp.arange(n_experts, dtype=jnp.int32)
    pos = jnp.sum(jnp.where(e12[..., None] == eids, off, 0), axis=-1) + r12
    tile_start = jnp.arange(R // tr, dtype=jnp.int32) * tr
    tile_expert = jnp.minimum(jnp.sum(tile_start[:, None] >= off_end[None, :], axis=1),
                              n_experts - 1).astype(jnp.int32)
    n_used = (off_end[-1:] // tr).astype(jnp.int32)
    pos_flat = pos.T.reshape(2 * N)
    idx_s = _chunk_index(pos_flat, D)
    xs = _from_chunks(sc_scatter_rows(_as_chunks(x), idx_s, R * (D // SC_ROW)), R, D)
    o_sorted = moe_gmm(tile_expert, n_used, xs, wg, wu, wd, 256)
    og = _from_chunks(sc_gather_rows(_as_chunks(o_sorted), idx_s), 2 * N, D)
    return moe_combine(x, og, w12, lg, lb, alpha)


def _as_chunks(a):
    rows, d = a.shape
    return a.reshape(rows // 8, 8, d // SC_ROW, SC_ROW).transpose(0, 2, 1, 3).reshape(-1, SC_ROW)


def _from_chunks(c, rows, d):
    return c.reshape(rows // 8, d // SC_ROW, 8, SC_ROW).transpose(0, 2, 1, 3).reshape(rows, d)


def _chunk_index(row_of, d):
    nk = d // SC_ROW
    r = row_of.reshape(-1, 1, 8)
    k = jnp.arange(nk, dtype=jnp.int32).reshape(1, nk, 1)
    return ((r // 8) * (8 * nk) + k * 8 + r % 8).reshape(-1)


IDX_PAGES_PER_STEP = 32
ATTN_PAGES_PER_STEP = 16


def _page_specs(layer, n, rows, P):
    return [pl.BlockSpec((1, 1, rows, PAGE), (lambda b, s, pt, j=j: (layer, pt[b, s * P + j], 0, 0)))
            for j in range(n)]


def _idx_s_kernel(pt_ref, iqT_ref, iq3_ref, iw_ref, iwc_ref, iknew_ref, tri_ref, low_ref, *refs,
                  topk, P):
    pages = refs[:P]
    am_ref, amnew_ref, s_ref = refs[P:]
    step = pl.program_id(1)
    iqT = iqT_ref[0] * (D_IDX ** -0.5)
    wi = iw_ref[0] * (H_IDX ** -0.5)
    wic = iwc_ref[0] * (H_IDX ** -0.5)
    iq3 = iq3_ref[0]
    for j in range(P):
        kt = pages[j][0, 0]
        hi = kt.astype(BF16)
        lo = (kt - hi.astype(F32)).astype(BF16)
        d = _dot(iq3, jnp.concatenate([hi, lo, hi], axis=0))
        sc = jnp.sum(wic * jnp.maximum(d, 0.0), axis=0, keepdims=True)
        s_ref[pl.ds(step * P + j, 1), :] = _orderable(sc + 0.0)

    @pl.when(step == pl.num_programs(1) - 1)
    def _():
        s = s_ref[...]
        dn = jnp.sum(iqT * iknew_ref[0], axis=0, keepdims=True)
        sn = jnp.sum(wi * jnp.maximum(dn, 0.0), axis=-1, keepdims=True) + 0.0
        kn = _orderable(sn)

        def total(x):
            return jnp.sum(jnp.sum(x, axis=-1, keepdims=True), axis=0, keepdims=True)

        def count_ge(cand):
            return total(jnp.where(s >= cand, 1.0, 0.0)) + jnp.where(kn >= cand, 1.0, 0.0)

        thr = _kth_largest(count_ge, (1, 1), topk)
        c_gt = total(jnp.where(s > thr, 1.0, 0.0)) + jnp.where(kn > thr, 1.0, 0.0)
        need = topk - c_gt
        tie = s == thr
        tie_f = jnp.where(tie, 1.0, 0.0)
        in_row = _dot(tie_f.astype(BF16), tri_ref[...])
        row_tot = jnp.broadcast_to(jnp.sum(tie_f, axis=-1, keepdims=True), tie_f.shape)
        before = _dot(low_ref[...], row_tot.astype(BF16))
        keep_tie = jnp.where(in_row + before < need, 0.0, NEG)
        am_ref[0] = jnp.where(s > thr, 0.0, jnp.where(tie, keep_tie, NEG))
        keep_new = jnp.where(total(tie_f) < need, 0.0, NEG)
        amn = jnp.where(kn > thr, 0.0, jnp.where(kn == thr, keep_new, NEG))
        amnew_ref[0] = jnp.broadcast_to(amn, amnew_ref.shape[1:])


def idx_sample(page_table, layer, iq, iw, ik_new, poolT, topk):
    DB, npg = page_table.shape
    iqT = jnp.transpose(iq, (0, 2, 1))
    iq3 = _split3(iq * (D_IDX ** -0.5), (0, 0, 1))
    iwc = iw.reshape(DB, H_IDX, 1)
    iw = iw.reshape(DB, 1, H_IDX)
    ik_new = ik_new.reshape(DB, D_IDX, 1)
    P = min(IDX_PAGES_PER_STEP, npg)
    assert npg % P == 0
    tri = jnp.asarray(np.triu(np.ones((PAGE, PAGE), np.float32), 1), BF16)
    low = jnp.asarray(np.tril(np.ones((npg, npg), np.float32), -1), BF16)
    per_b = lambda b, s, pt: (b, 0, 0)
    grid_spec = pltpu.PrefetchScalarGridSpec(
        num_scalar_prefetch=1,
        grid=(DB, npg // P),
        in_specs=[pl.BlockSpec((1, D_IDX, H_IDX), per_b),
                  pl.BlockSpec((1, H_IDX, IDX_SPLIT), per_b),
                  pl.BlockSpec((1, 1, H_IDX), per_b),
                  pl.BlockSpec((1, H_IDX, 1), per_b),
                  pl.BlockSpec((1, D_IDX, 1), per_b),
                  pl.BlockSpec((PAGE, PAGE), lambda b, s, pt: (0, 0)),
                  pl.BlockSpec((npg, npg), lambda b, s, pt: (0, 0))]
                 + _page_specs(layer, P, D_IDX, P),
        out_specs=[pl.BlockSpec((1, npg, PAGE), per_b),
                   pl.BlockSpec((1, 1, PAGE), per_b)],
        scratch_shapes=[pltpu.VMEM((npg, PAGE), jnp.int32)],
    )
    return pl.pallas_call(
        functools.partial(_idx_s_kernel, topk=float(topk), P=P),
        grid_spec=grid_spec,
        out_shape=[jax.ShapeDtypeStruct((DB, npg, PAGE), F32),
                   jax.ShapeDtypeStruct((DB, 1, PAGE), F32)],
        compiler_params=_cparams(("arbitrary", "arbitrary")),
        name="idx_sample",
    )(page_table, iqT, iq3, iw, iwc, ik_new, tri, low, *([poolT] * P))


def _expand_rows(x, rep):
    g, n = x.shape
    return jnp.broadcast_to(x[:, None, :], (g, rep, n)).reshape(g * rep, n)


def _group_logits(kt, qb):
    w, n = kt.shape
    return jnp.sum((kt * qb).reshape(N_GROUPS, w // N_GROUPS, n), axis=1)


def _dec_a_kernel(pt_ref, q_ref, am_ref, amnew_ref, knew_ref, vnew_ref, blast_ref, bnew_ref,
                  *refs, scale, P):
    kp = refs[:P]
    vp = refs[P:2 * P]
    o_ref, qb_ref, m_ref, l_ref, acc_ref = refs[2 * P:]
    step = pl.program_id(1)
    is_last = step == pl.num_programs(1) - 1
    W = qb_ref.shape[0]
    rep = W // N_GROUPS

    @pl.when(step == 0)
    def _():
        qb_ref[...] = jnp.broadcast_to(q_ref[0] * scale, qb_ref.shape)
        m_ref[...] = jnp.full_like(m_ref, NEG)
        l_ref[...] = jnp.zeros_like(l_ref)
        acc_ref[...] = jnp.zeros_like(acc_ref)

    last_f = jnp.where(is_last, 1.0, 0.0)
    for j in range(P):
        lg = _group_logits(kp[j][0, 0], qb_ref[...]) + am_ref[0, j:j + 1, :]
        if j == P - 1:
            lg = lg + last_f * blast_ref[...]
        m_old = m_ref[...]
        m_new = jnp.maximum(m_old, lg)
        alpha = jnp.exp(m_old - m_new)
        p = jnp.exp(lg - m_new)
        m_ref[...] = m_new
        l_ref[...] = l_ref[...] * alpha + p
        acc_ref[...] = acc_ref[...] * _expand_rows(alpha, rep) + _expand_rows(p, rep) * vp[j][0, 0]

    @pl.when(is_last)
    def _():
        q = q_ref[0] * scale
        lgn = (jnp.sum((knew_ref[0] * q).reshape(N_GROUPS, rep, 1), axis=1)
               + bnew_ref[...] + amnew_ref[0][:, 0:1])
        m = m_ref[...]
        mx = jnp.maximum(jnp.max(m, axis=-1, keepdims=True), lgn)
        w = jnp.exp(m - mx)
        pn = jnp.exp(lgn - mx)
        l = jnp.sum(l_ref[...] * w, axis=-1, keepdims=True) + pn
        o = (jnp.sum(acc_ref[...] * _expand_rows(w, rep), axis=-1, keepdims=True)
             + _expand_rows(pn, rep) * vnew_ref[0])
        o_ref[0] = o / _expand_rows(l, rep)


def decode_a(page_table, layer, qcol, am, amnew, knew, vnew, blast, bnew, kT, vT, scale):
    DB, npg = page_table.shape
    W = qcol.shape[1]
    P = min(ATTN_PAGES_PER_STEP, npg)
    assert npg % P == 0
    per_b = lambda b, s, pt: (b, 0, 0)
    c2 = lambda b, s, pt: (0, 0)
    grid_spec = pltpu.PrefetchScalarGridSpec(
        num_scalar_prefetch=1,
        grid=(DB, npg // P),
        in_specs=[pl.BlockSpec((1, W, 1), per_b),
                  pl.BlockSpec((1, P, PAGE), lambda b, s, pt: (b, s, 0)),
                  pl.BlockSpec((1, 1, PAGE), per_b),
                  pl.BlockSpec((1, W, 1), per_b),
                  pl.BlockSpec((1, W, 1), per_b),
                  pl.BlockSpec((N_GROUPS, PAGE), c2),
                  pl.BlockSpec((N_GROUPS, 1), c2)]
                 + _page_specs(layer, P, W, P) + _page_specs(layer, P, W, P),
        out_specs=pl.BlockSpec((1, W, 1), per_b),
        scratch_shapes=[pltpu.VMEM((W, PAGE), F32), pltpu.VMEM((N_GROUPS, PAGE), F32),
                        pltpu.VMEM((N_GROUPS, PAGE), F32), pltpu.VMEM((W, PAGE), F32)],
    )
    return pl.pallas_call(
        functools.partial(_dec_a_kernel, scale=scale, P=P),
        grid_spec=grid_spec,
        out_shape=jax.ShapeDtypeStruct((DB, W, 1), F32),
        compiler_params=_cparams(("arbitrary", "arbitrary")),
        name="decode_a",
    )(page_table, qcol, am, amnew, knew, vnew, blast, bnew, *([kT] * P), *([vT] * P))


def _dec_b_kernel(pt_ref, q_ref, knew_ref, vnew_ref, blast_ref, bnew_ref, r_ref, pm_ref,
                  lam_ref, g_ref, *refs, scale, lam_init, P):
    kp = refs[:P]
    vp = refs[P:2 * P]
    o_ref, qb_ref, m_ref, l_ref, acc_ref = refs[2 * P:]
    step = pl.program_id(1)
    is_last = step == pl.num_programs(1) - 1
    W = qb_ref.shape[0]
    rep = W // N_GROUPS

    @pl.when(step == 0)
    def _():
        qb_ref[...] = jnp.broadcast_to(q_ref[0] * scale, qb_ref.shape)
        m_ref[...] = jnp.full_like(m_ref, NEG)
        l_ref[...] = jnp.zeros_like(l_ref)
        acc_ref[...] = jnp.zeros_like(acc_ref)

    last_f = jnp.where(is_last, 1.0, 0.0)
    lgs = [_group_logits(kp[j][0, 0], qb_ref[...]) for j in range(P)]
    lgs[P - 1] = lgs[P - 1] + last_f * blast_ref[...]
    m_old = m_ref[...]
    m_new = m_old
    for lg in lgs:
        m_new = jnp.maximum(m_new, jnp.max(lg, axis=-1, keepdims=True))
    alpha = jnp.exp(m_old - m_new)
    ps = [jnp.exp(lg - m_new) for lg in lgs]
    m_ref[...] = m_new
    l_ref[...] = l_ref[...] * alpha + sum(jnp.sum(p, axis=-1, keepdims=True) for p in ps)
    pr = _dot(jnp.concatenate(ps, axis=0).astype(BF16), r_ref[...])
    o = jnp.zeros(acc_ref.shape, F32)
    for j in range(P):
        prj = (pr[j * N_GROUPS:(j + 1) * N_GROUPS] * pm_ref[...]).astype(BF16)
        o = o + _dot(prj, vp[j][0, 0].astype(BF16))
    acc_ref[...] = acc_ref[...] * alpha + o

    @pl.when(is_last)
    def _():
        q = q_ref[0] * scale
        lgn = jnp.sum((knew_ref[0] * q).reshape(N_GROUPS, rep, 1), axis=1) + bnew_ref[...]
        m = m_ref[...]
        mx = jnp.maximum(m, lgn)
        a2 = jnp.exp(m - mx)
        pn = jnp.exp(lgn - mx)
        l = l_ref[...] * a2 + pn
        on = (acc_ref[...] * a2 + pn * vnew_ref[0]) / l
        lam = _lambda(lam_ref[...], lam_init)
        r = lax.broadcasted_iota(jnp.int32, (N_GROUPS, N_GROUPS), 0)
        c = lax.broadcasted_iota(jnp.int32, (N_GROUPS, N_GROUPS), 1)
        comb = jnp.where(c == 2 * r, 1.0, 0.0) - lam * jnp.where(c == 2 * r + 1, 1.0, 0.0)
        o = _dot(comb, on, True)
        o = o * lax.rsqrt(jnp.mean(o * o, axis=-1, keepdims=True) + SUBLN_EPS)
        o_ref[0] = o * g_ref[...] * (1.0 - lam_init)


def decode_b(page_table, layer, qcol, knew, vnew_g, blast, bnew, rmat, pmask, lamv, g, kT, vrows,
             scale, lam_init):
    DB, npg = page_table.shape
    W = qcol.shape[1]
    P = min(ATTN_PAGES_PER_STEP, npg)
    dv = 2 * D_B
    assert npg % P == 0
    per_b = lambda b, s, pt: (b, 0, 0)
    c2 = lambda b, s, pt: (0, 0)
    grid_spec = pltpu.PrefetchScalarGridSpec(
        num_scalar_prefetch=1,
        grid=(DB, npg // P),
        in_specs=[pl.BlockSpec((1, W, 1), per_b),
                  pl.BlockSpec((1, W, 1), per_b),
                  pl.BlockSpec((1, N_GROUPS, dv), per_b),
                  pl.BlockSpec((N_GROUPS, PAGE), c2),
                  pl.BlockSpec((N_GROUPS, 1), c2),
                  pl.BlockSpec((PAGE, W), c2),
                  pl.BlockSpec((N_GROUPS, W), c2),
                  pl.BlockSpec((4, D_B), c2),
                  pl.BlockSpec((1, dv), c2)]
                 + _page_specs(layer, P, W, P) + _page_specs(layer, P, W, P),
        out_specs=pl.BlockSpec((1, N_GROUPS, dv), per_b),
        scratch_shapes=[pltpu.VMEM((W, PAGE), F32), pltpu.VMEM((N_GROUPS, 1), F32),
                        pltpu.VMEM((N_GROUPS, 1), F32), pltpu.VMEM((N_GROUPS, dv), F32)],
    )
    return pl.pallas_call(
        functools.partial(_dec_b_kernel, scale=scale, lam_init=lam_init, P=P),
        grid_spec=grid_spec,
        out_shape=jax.ShapeDtypeStruct((DB, N_GROUPS, dv), F32),
        compiler_params=_cparams(("arbitrary", "arbitrary")),
        name="decode_b",
    )(page_table, qcol, knew, vnew_g, blast, bnew, rmat, pmask, lamv, g,
      *([kT] * P), *([vrows] * P))


def _dec_m_kernel(q_ref, k_ref, v_ref, o_ref):
    n = k_ref.shape[2] // 8
    q = q_ref[0]
    k3 = k_ref[0, 0].reshape(n, 8, D_M)
    lg = jnp.sum(k3 * q[None], axis=-1, keepdims=True)
    m8 = jnp.max(lg, axis=0)
    m4 = jnp.maximum(m8[:H_M], m8[H_M:])
    p = jnp.exp(lg - jnp.concatenate([m4, m4], axis=0)[None])
    l8 = jnp.sum(p, axis=0)
    o8 = jnp.sum(p * v_ref[0, 0].reshape(n, 8, D_M), axis=0)
    o_ref[0] = (o8[:H_M] + o8[H_M:]) / (l8[:H_M] + l8[H_M:])


def decode_m(layer, q8, mem_k, mem_v):
    DB = q8.shape[0]
    rows = mem_k.shape[2]
    return pl.pallas_call(
        _dec_m_kernel,
        grid=(DB,),
        in_specs=[pl.BlockSpec((1, 8, D_M), lambda b: (b, 0, 0)),
                  pl.BlockSpec((1, 1, rows, D_M), lambda b: (layer, b, 0, 0)),
                  pl.BlockSpec((1, 1, rows, D_M), lambda b: (layer, b, 0, 0))],
        out_specs=pl.BlockSpec((1, H_M, D_M), lambda b: (b, 0, 0)),
        out_shape=jax.ShapeDtypeStruct((DB, H_M, D_M), F32),
        compiler_params=_cparams(("arbitrary",)),
        name="decode_m",
    )(q8, mem_k, mem_v)


def _t5_bucket_np(rel):
    n = np.maximum(rel, 0)
    max_exact = NUM_BUCKETS // 2
    nf = np.maximum(n, 1).astype(np.float32)
    large = max_exact + (np.log(nf / np.float32(max_exact))
                         / np.float32(math.log(MAX_DISTANCE / max_exact))
                         * np.float32(NUM_BUCKETS - max_exact)).astype(np.int32)
    large = np.minimum(large, NUM_BUCKETS - 1)
    return np.where(n < max_exact, n, large).astype(np.int32)


def _far_bucket(first_far, last_far):
    b = _t5_bucket_np(np.arange(first_far, last_far + 1))
    assert (b == b[0]).all(), "relative-position bias must be constant beyond one tile"
    return int(b[0])


def _bias_lookup(rel_bias, buckets):
    onehot = jax.nn.one_hot(jnp.asarray(buckets, jnp.int32), NUM_BUCKETS, dtype=F32)
    return jnp.einsum('...b,bh->...h', onehot, rel_bias, precision=HIGHEST)


def _band_tables(rel_bias, heads, t_max):
    far = _far_bucket(TQ + 1, max(t_max, TQ + 1))
    r = np.arange(TQ)[None, :] - np.arange(TQ)[:, None]
    tabs = []
    for d in range(2):
        rel = r + d * TQ
        vals = _bias_lookup(rel_bias, _t5_bucket_np(rel))[..., heads] - rel_bias[far][heads]
        vals = jnp.where((rel >= 0)[..., None], vals, NEG)
        tabs.append(jnp.transpose(vals, (2, 0, 1)))
    return jnp.stack(tabs, axis=1).astype(F32)


def _split3(x, order):
    hi = x.astype(BF16)
    lo = (x - hi.astype(F32)).astype(BF16)
    return jnp.concatenate([(hi, lo)[o] for o in order], axis=-1)


def kernel(x_prompt, x_sample, cache_a_k, cache_a_v, cache_idx_k, cache_b_k, cache_b_v, cache_mem_k, cache_mem_v, page_table, mem_prompt, w_in, w_mem_kv, lambda_q1, lambda_k1, lambda_q2, lambda_k2, subln_g, w_branch_a, w_branch_b, w_branch_m, w_out, rel_bias, ln1_g, ln1_b, ln2_g, ln2_b, ffn_w_gate, ffn_w_up, ffn_w_down, router_w, expert_w_gate, expert_w_up, expert_w_down):
    B, T, D = x_prompt.shape
    DB, TS, _ = x_sample.shape
    depth = w_in.shape[0]
    n_pool = cache_a_k.shape[1]
    npg = page_table.shape[1]
    past = npg * PAGE
    n_mem = mem_prompt.shape[1]
    assert TS == 1 and T % TQ == 0 and cache_a_k.shape[2] == PAGE
    alpha = (2 * depth) ** 0.25
    topk_p = min(TOPK_MAX, T // 4)
    topk_s = min(TOPK_MAX, (past + TS) // 4)
    W = HEAD_COLS
    nq = T // TQ

    c_ik = 3 * A_WIDTH + H_IDX * D_IDX
    c_b = c_ik + D_IDX + H_IDX
    n_main = 8 * W + 3 * D
    tail_pad = 128 - D_IDX - H_IDX

    low = jnp.asarray(np.tril(np.ones((TQ, TQ), np.float32), -1), BF16)
    band_a = _band_tables(rel_bias, np.arange(H_A), T)
    band_b = _band_tables(rel_bias, H_A + np.arange(H_B), T)

    far_s = _far_bucket(PAGE + 1, past)
    b_last = (_bias_lookup(rel_bias, _t5_bucket_np(PAGE - np.arange(PAGE))) - rel_bias[far_s]).T
    b_new = (rel_bias[0] - rel_bias[far_s])[:, None]
    blast_a, bnew_a = b_last[:H_A], b_new[:H_A]
    blast_b, bnew_b = jnp.repeat(b_last[H_A:], 2, axis=0), jnp.repeat(b_new[H_A:], 2, axis=0)
    lane = np.arange(W)
    rmat = jnp.asarray(lane[None, :] // H_B == np.arange(PAGE)[:, None], BF16)
    pmask = jnp.asarray(lane[None, :] % H_B == np.arange(N_GROUPS)[:, None] // 2, F32)

    idxT = jnp.transpose(cache_idx_k, (0, 1, 3, 2))
    akT = jnp.transpose(cache_a_k, (0, 1, 3, 4, 2)).reshape(depth, n_pool, W, PAGE)
    avT = jnp.transpose(cache_a_v, (0, 1, 3, 4, 2)).reshape(depth, n_pool, W, PAGE)
    bkT = jnp.transpose(cache_b_k, (0, 1, 3, 4, 5, 2)).reshape(depth, n_pool, W, PAGE)
    bv_rows = cache_b_v.reshape(depth, n_pool, PAGE * H_B, 2 * D_B)
    memk_rows = cache_mem_k.reshape(depth, DB, n_mem * H_M, D_M)
    memv_rows = cache_mem_v.reshape(depth, DB, n_mem * H_M, D_M)

    outs = {n: [] for n in ('akp', 'avp', 'ikp', 'bkp', 'bvp', 'mkp', 'mvp',
                            'aks', 'avs', 'iks', 'bks', 'bvs')}
    x_p = x_prompt.reshape(B * T, D)
    x_s = x_sample.reshape(DB, D)
    for l in range(depth):
        lam_init = 0.8 - 0.6 * math.exp(-0.3 * l)
        lamv = jnp.stack([lambda_q1[l], lambda_k1[l], lambda_q2[l], lambda_k2[l]])
        wl = w_in[l]
        w_perm = jnp.concatenate([wl[:, :c_ik], wl[:, c_b:], wl[:, c_ik:c_b],
                                  jnp.zeros((D, tail_pad), F32)], axis=1)
        w_main_bf = w_perm[:, :n_main].astype(BF16)
        w_tail_bf = w_perm[:, n_main:].astype(BF16)
        wa_bf, wb_bf, wm_bf, wo_bf = (w[l].astype(BF16) for w in (w_branch_a, w_branch_b, w_branch_m, w_out))
        lg1, lb1 = ln1_g[l][None], ln1_b[l][None]
        lg2, lb2 = ln2_g[l][None], ln2_b[l][None]
        g_sub = subln_g[l][None]
        j = l // 2
        if l % 2 == 0:
            rw = jnp.zeros((D, 128), BF16)
            wg, wu, wd = ffn_w_gate[j][None], ffn_w_up[j][None], ffn_w_down[j][None]
            routed = False
        else:
            rw = jnp.pad(router_w[j], ((0, 0), (0, 128 - N_EXPERTS))).astype(BF16)
            wg, wu, wd = expert_w_gate[j], expert_w_up[j], expert_w_down[j]
            routed = True

        h = matmul(x_p, w_main_bf, 1024 if B * T % 1024 == 0 else 512, 1024,
                   name="proj_prompt")
        tail = matmul(x_p, w_tail_bf, 512, 128, name="proj_tail")
        h3 = h.reshape(B, T, n_main)
        col = lambda c: h3[:, :, c * W:(c + 1) * W]

        def heads(c, nh, dh, dt):
            return col(c).reshape(B, T, nh, dh).transpose(0, 2, 1, 3).astype(dt)

        def q_t(c, scale):
            return jnp.transpose(col(c) * scale, (0, 2, 1)).astype(BF16)

        def v_t(c):
            return col(c).reshape(B, nq, TQ, W).transpose(0, 1, 3, 2).astype(BF16)

        a_k, a_v, b_k, b_v = col(1), col(2), col(5), col(6)
        i_k = tail[:, :D_IDX].reshape(B, T, D_IDX)
        i_w = tail[:, D_IDX:D_IDX + H_IDX].reshape(B, T, H_IDX)
        iq = col(3).reshape(B, T, H_IDX, D_IDX) * (D_IDX ** -0.5)
        iqT = jnp.transpose(_split3(iq, (0, 0, 1)), (0, 2, 3, 1))
        ik3 = _split3(i_k, (0, 1, 0))
        wT = jnp.transpose(i_w * (H_IDX ** -0.5), (0, 2, 1))
        o_a = attn_a_prompt(q_t(0, D_A ** -0.5), heads(1, H_A, D_A, BF16), v_t(2), iqT, ik3, wT,
                            band_a, low, topk_p)
        o_b = attn_b_prompt(q_t(4, D_B ** -0.5), heads(5, 2 * H_B, D_B, BF16), v_t(6),
                            band_b, lamv, g_sub.T, lam_init)
        kv = matmul(mem_prompt.reshape(B * n_mem, D), w_mem_kv[l].astype(BF16), 512, 1024,
                    name="proj_mem")
        mk = kv[:, :M_WIDTH].reshape(B, n_mem, H_M, D_M)
        mv = kv[:, M_WIDTH:].reshape(B, n_mem, H_M, D_M)
        qm = (heads(7, H_M, D_M, F32) * (D_M ** -0.5)).astype(BF16)
        o_m = attn_m_prompt(qm, mk.transpose(0, 2, 1, 3).astype(BF16),
                            mv.transpose(0, 2, 1, 3).astype(BF16))
        x_p = merge(x_p, o_a.reshape(B * T, W), o_b.reshape(B * T, W), o_m.reshape(B * T, W),
                    h, 8 * W // D, wa_bf, wb_bf, wm_bf, wo_bf, lg1, lb1, alpha, 512, name="merge_prompt")
        x_p_mid = x_p
        outs['akp'].append(a_k.reshape(B, T, H_A, D_A))
        outs['avp'].append(a_v.reshape(B, T, H_A, D_A))
        outs['ikp'].append(i_k)
        outs['bkp'].append(b_k.reshape(B, T, H_B, 2, D_B))
        outs['bvp'].append(b_v.reshape(B, T, H_B, 2 * D_B))
        outs['mkp'].append(mk)
        outs['mvp'].append(mv)

        hs = matmul(x_s, w_main_bf, DB, 1024, name="proj_sample")
        ts = matmul(x_s, w_tail_bf, DB, 128, name="proj_tail_sample")
        sa_q, sa_k, sa_v, s_iq = (hs[:, c * W:(c + 1) * W] for c in range(4))
        sb_q, sb_k, sb_v, sm_q = (hs[:, c * W:(c + 1) * W] for c in range(4, 8))
        s_ik = ts[:, :D_IDX]
        s_iw = ts[:, D_IDX:D_IDX + H_IDX]
        colv = lambda t: t.reshape(DB, -1, 1)
        am, amn = idx_sample(page_table, l, s_iq.reshape(DB, H_IDX, D_IDX), s_iw, s_ik, idxT, topk_s)
        o_a = decode_a(page_table, l, colv(sa_q), am, amn, colv(sa_k), colv(sa_v),
                       blast_a, bnew_a, akT, avT, D_A ** -0.5).reshape(DB, W)
        vnew_g = jnp.repeat(sb_v.reshape(DB, H_B, 2 * D_B), 2, axis=1)
        o_b = decode_b(page_table, l, colv(sb_q), colv(sb_k), vnew_g, blast_b, bnew_b, rmat, pmask,
                       lamv, g_sub, bkT, bv_rows, D_B ** -0.5, lam_init)[:, :H_B].reshape(DB, W)
        q8 = jnp.tile(sm_q.reshape(DB, H_M, D_M) * (D_M ** -0.5), (1, 2, 1))
        o_m = decode_m(l, q8, memk_rows, memv_rows).reshape(DB, W)
        x_s = merge(x_s, o_a, o_b, o_m, hs, 8 * W // D, wa_bf, wb_bf, wm_bf, wo_bf, lg1, lb1, alpha,
                    DB, name="merge_sample")
        wg_bf, wu_bf, wd_bf = wg.astype(BF16), wu.astype(BF16), wd.astype(BF16)
        x_s = ffn(x_s, rw, wg_bf, wu_bf, wd_bf, lg2, lb2, alpha, DB, 256, routed, name="ffn_sample")
        if routed:
            x_p = moe_sparse(x_p_mid, rw, wg_bf, wu_bf, wd_bf, lg2, lb2, alpha, N_EXPERTS)
        else:
            x_p = ffn(x_p_mid, rw, wg_bf, wu_bf, wd_bf, lg2, lb2, alpha,
                      1024 if B * T % 1024 == 0 else 512, 256, False, name="ffn_prompt")
        outs['aks'].append(sa_k.reshape(DB, TS, H_A, D_A))
        outs['avs'].append(sa_v.reshape(DB, TS, H_A, D_A))
        outs['iks'].append(s_ik.reshape(DB, TS, D_IDX))
        outs['bks'].append(sb_k.reshape(DB, TS, H_B, 2, D_B))
        outs['bvs'].append(sb_v.reshape(DB, TS, H_B, 2 * D_B))

    st = {n: jnp.stack(v) for n, v in outs.items()}
    return (x_p.reshape(B, T, D), x_s.reshape(DB, TS, D),
            st['akp'], st['avp'], st['ikp'], st['bkp'], st['bvp'], st['mkp'], st['mvp'],
            st['aks'], st['avs'], st['iks'], st['bks'], st['bvs'])
```

```python
import functools
import math

import jax
import jax.numpy as jnp
import numpy as np
from jax import lax
from jax.experimental import pallas as pl
from jax.experimental.pallas import tpu as pltpu
from jax.experimental.pallas import tpu_sc as plsc

F32 = jnp.float32
BF16 = jnp.bfloat16
HIGHEST = lax.Precision.HIGHEST

H_A, D_A = 8, 64
H_IDX, D_IDX = 8, 64
TOPK_MAX = 256
H_B, D_B = 4, 64
H_M, D_M = 4, 128
NUM_BUCKETS, MAX_DISTANCE = 32, 128
N_EXPERTS = 8
LN_EPS = 1e-5
SUBLN_EPS = 1e-5
A_WIDTH = H_A * D_A
B_WIDTH = H_B * 2 * D_B
M_WIDTH = H_M * D_M
HEAD_COLS = 512
PAGE = 128
N_GROUPS = 8

NEG = -1e30
KEY_NEG_INF = -2139095041
INT_MIN = -2147483648

VMEM_LIMIT = 56 * 1024 * 1024

TQ = 256
IDX_SPLIT = 3 * D_IDX


def _cparams(sem):
    return pltpu.CompilerParams(dimension_semantics=sem, vmem_limit_bytes=VMEM_LIMIT)


def _dot(a, b, precise=False):
    if precise:
        return jnp.dot(a, b, preferred_element_type=F32, precision=HIGHEST)
    return jnp.dot(a, b, preferred_element_type=F32)


def _dot_nt(a, b):
    return lax.dot_general(a, b, (((1,), (1,)), ((), ())), preferred_element_type=F32)


def _orderable(s):
    b = lax.bitcast_convert_type(s, jnp.int32)
    return jnp.where(b < 0, b ^ jnp.int32(0x7FFFFFFF), b)


def _layer_norm(v, g, b):
    mu = jnp.mean(v, axis=-1, keepdims=True)
    d = v - mu
    var = jnp.mean(d * d, axis=-1, keepdims=True)
    return d * lax.rsqrt(var + LN_EPS) * g + b


def _kth_largest(count_ge, shape, topk):
    t0 = jnp.full(shape, INT_MIN, jnp.int32)
    t0 = jnp.where(count_ge(jnp.zeros(shape, jnp.int32)) >= topk, jnp.zeros(shape, jnp.int32), t0)

    def bit_body(b, t):
        cand = t | jnp.left_shift(jnp.int32(1), 30 - b)
        return jnp.where(count_ge(cand) >= topk, cand, t)

    return lax.fori_loop(0, 31, bit_body, t0)


def _lambda(lv, lam_init):
    return (jnp.exp(jnp.sum(lv[0:1] * lv[1:2], axis=-1, keepdims=True))
            - jnp.exp(jnp.sum(lv[2:3] * lv[3:4], axis=-1, keepdims=True)) + lam_init)


def _mm_kernel(x_ref, w_ref, o_ref):
    o_ref[...] = _dot(x_ref[...].astype(BF16), w_ref[...])


def matmul(x, w, tm, tn, name="matmul"):
    M, K = x.shape
    N = w.shape[1]
    assert M % tm == 0 and N % tn == 0
    return pl.pallas_call(
        _mm_kernel,
        grid=(N // tn, M // tm),
        in_specs=[pl.BlockSpec((tm, K), lambda j, i: (i, 0)),
                  pl.BlockSpec((K, tn), lambda j, i: (0, j))],
        out_specs=pl.BlockSpec((tm, tn), lambda j, i: (i, j)),
        out_shape=jax.ShapeDtypeStruct((M, N), F32),
        compiler_params=_cparams(("arbitrary", "arbitrary")),
        name=name,
    )(x, w)


def _set_row(full, g, row):
    r = lax.broadcasted_iota(jnp.int32, full.shape, 0)
    return jnp.where(r == g, row, full)


def _attend_tile(kj, band_d, ml, *, n_groups, dv, qT_ref, k_ref, vT_ref, band_ref, am, acc_ref,
                 s_ref, p_ref, tq):
    m_all, l_all = ml
    ksl = pl.ds(pl.multiple_of(kj * tq, tq), tq)
    gpb = n_groups // band_ref.shape[0]
    gpv = (n_groups * dv) // vT_ref.shape[2]
    dq = qT_ref.shape[1] // n_groups
    m_out, l_out = m_all, l_all
    for g in range(n_groups):
        s_ref[g] = _dot(k_ref[0, g, ksl, :], qT_ref[0, g * dq:(g + 1) * dq, :])
    alphas = []
    for g in range(n_groups):
        s = s_ref[g]
        if am is not None:
            s = s + am
        if band_d is not None:
            s = s + band_ref[g // gpb, band_d]
        m_old = m_all[g:g + 1, :]
        m_new = jnp.maximum(m_old, jnp.max(s, axis=0, keepdims=True))
        alpha = jnp.exp(m_old - m_new)
        p = jnp.exp(s - m_new)
        p_ref[g] = p.astype(BF16)
        alphas.append(alpha)
        m_out = _set_row(m_out, g, m_new)
        l_out = _set_row(l_out, g, alpha * l_all[g:g + 1, :] + jnp.sum(p, axis=0, keepdims=True))
    for g in range(n_groups):
        vrows = slice((g // gpv) * dv, (g // gpv + 1) * dv)
        arows = slice(g * dv, (g + 1) * dv)
        acc_ref[arows, :] = alphas[g] * acc_ref[arows, :] + _dot(vT_ref[0, kj, vrows, :], p_ref[g])
    return m_out, l_out


def _attend_causal(i, tile_fn, n_groups, tq):
    ml = (jnp.full((n_groups, tq), NEG, F32), jnp.zeros((n_groups, tq), F32))
    ml = lax.fori_loop(0, jnp.maximum(i - 1, 0), lambda kj, c: tile_fn(kj, None, c), ml)
    ml = lax.cond(i >= 1, lambda c: tile_fn(i - 1, 1, c), lambda c: c, ml)
    return tile_fn(i, 0, ml)


def _attn_a_kernel(qT_ref, k_ref, vT_ref, iqT_ref, ik_ref, wT_ref, band_ref, low_ref,
                   o_ref, key_ref, am_ref, acc_ref, s_ref, p_ref, *, topk):
    i = pl.program_id(1)
    nk = i + 1
    tq = qT_ref.shape[2]
    qpos = i * tq + lax.broadcasted_iota(jnp.int32, (tq, tq), 1)
    krow = lax.broadcasted_iota(jnp.int32, (tq, tq), 0)

    def score_body(kj, c):
        ikt = ik_ref[0, pl.ds(pl.multiple_of(kj * tq, tq), tq), :]
        s = jnp.zeros((tq, tq), F32)
        for h in range(H_IDX):
            s = s + wT_ref[0, h:h + 1, :] * jnp.maximum(_dot(ikt, iqT_ref[0, h]), 0.0)
        s = jnp.where(kj * tq + krow <= qpos, s, -jnp.inf)
        key_ref[kj] = _orderable(s)
        return c

    lax.fori_loop(0, nk, score_body, 0)

    def count(pred):
        def body(kj, c):
            t = jnp.where(pred(key_ref[kj]), 1.0, 0.0)
            return c + jnp.sum(t.reshape(tq // 8, 8, tq), axis=0)
        c = lax.fori_loop(0, nk, body, jnp.zeros((8, tq), F32))
        return jnp.sum(c, axis=0, keepdims=True)

    thr = _kth_largest(lambda cand: count(lambda s: s >= cand), (1, tq), topk)
    c_gt = count(lambda s: s > thr)
    need = jnp.where(thr > KEY_NEG_INF, topk - c_gt, 0.0)

    def mask_body(kj, run):
        s = key_ref[kj]
        tie = s == thr
        tie_f = jnp.where(tie, 1.0, 0.0)
        pre = _dot(low_ref[...], tie_f.astype(BF16)) + run
        keep_tie = jnp.where(pre < need, 0.0, NEG)
        am_ref[kj] = jnp.where(s > thr, 0.0, jnp.where(tie, keep_tie, NEG))
        return run + jnp.sum(tie_f, axis=0, keepdims=True)

    lax.fori_loop(0, nk, mask_body, jnp.zeros((1, tq), F32))

    acc_ref[...] = jnp.zeros_like(acc_ref)

    def tile_fn(kj, band_d, ml):
        return _attend_tile(kj, band_d, ml, n_groups=H_A, dv=D_A, qT_ref=qT_ref, k_ref=k_ref,
                            vT_ref=vT_ref, band_ref=band_ref, am=am_ref[kj], acc_ref=acc_ref,
                            s_ref=s_ref, p_ref=p_ref, tq=tq)

    _, l_all = _attend_causal(i, tile_fn, H_A, tq)
    inv = 1.0 / l_all
    oT = jnp.concatenate([acc_ref[h * D_A:(h + 1) * D_A, :] * inv[h:h + 1, :] for h in range(H_A)],
                         axis=0)
    o_ref[0] = oT.T


def attn_a_prompt(qT, k, vT, iqT, ik, wT, band, low, topk):
    B, H, T, D = k.shape
    nq = T // TQ
    W = H * D
    return pl.pallas_call(
        functools.partial(_attn_a_kernel, topk=float(topk)),
        grid=(B, nq),
        in_specs=[
            pl.BlockSpec((1, W, TQ), lambda b, i: (b, 0, i)),
            pl.BlockSpec((1, H, T, D), lambda b, i: (b, 0, 0, 0)),
            pl.BlockSpec((1, nq, W, TQ), lambda b, i: (b, 0, 0, 0)),
            pl.BlockSpec((1, H_IDX, IDX_SPLIT, TQ), lambda b, i: (b, 0, 0, i)),
            pl.BlockSpec((1, T, IDX_SPLIT), lambda b, i: (b, 0, 0)),
            pl.BlockSpec((1, H_IDX, TQ), lambda b, i: (b, 0, i)),
            pl.BlockSpec((H, 2, TQ, TQ), lambda b, i: (0, 0, 0, 0)),
            pl.BlockSpec((TQ, TQ), lambda b, i: (0, 0)),
        ],
        out_specs=pl.BlockSpec((1, TQ, W), lambda b, i: (b, i, 0)),
        out_shape=jax.ShapeDtypeStruct((B, T, W), F32),
        scratch_shapes=[pltpu.VMEM((nq, TQ, TQ), jnp.int32),
                        pltpu.VMEM((nq, TQ, TQ), F32),
                        pltpu.VMEM((W, TQ), F32),
                        pltpu.VMEM((H, TQ, TQ), F32), pltpu.VMEM((H, TQ, TQ), BF16)],
        compiler_params=_cparams(("arbitrary", "arbitrary")),
        name="attn_a_prompt",
    )(qT, k, vT, iqT, ik, wT, band, low)


def _attn_b_kernel(qT_ref, k_ref, vT_ref, band_ref, lam_ref, g_ref, o_ref,
                   acc_ref, s_ref, p_ref, *, lam_init):
    i = pl.program_id(1)
    tq = qT_ref.shape[2]
    dv = 2 * D_B
    acc_ref[...] = jnp.zeros_like(acc_ref)

    def tile_fn(kj, band_d, ml):
        return _attend_tile(kj, band_d, ml, n_groups=2 * H_B, dv=dv, qT_ref=qT_ref, k_ref=k_ref,
                            vT_ref=vT_ref, band_ref=band_ref, am=None, acc_ref=acc_ref,
                            s_ref=s_ref, p_ref=p_ref, tq=tq)

    _, l_all = _attend_causal(i, tile_fn, 2 * H_B, tq)
    lam = _lambda(lam_ref[...], lam_init)
    inv = 1.0 / l_all
    parts = []
    for h in range(H_B):
        g0, g1 = 2 * h, 2 * h + 1
        o = (acc_ref[g0 * dv:(g0 + 1) * dv, :] * inv[g0:g0 + 1, :]
             - lam * (acc_ref[g1 * dv:(g1 + 1) * dv, :] * inv[g1:g1 + 1, :]))
        o = o * lax.rsqrt(jnp.mean(o * o, axis=0, keepdims=True) + SUBLN_EPS)
        parts.append(o * g_ref[...] * (1.0 - lam_init))
    o_ref[0] = jnp.concatenate(parts, axis=0).T


def attn_b_prompt(qT, k, vT, band, lamv, gcol, lam_init):
    B, G, T, D = k.shape
    nq = T // TQ
    W = G * D
    return pl.pallas_call(
        functools.partial(_attn_b_kernel, lam_init=lam_init),
        grid=(B, nq),
        in_specs=[
            pl.BlockSpec((1, W, TQ), lambda b, i: (b, 0, i)),
            pl.BlockSpec((1, G, T, D), lambda b, i: (b, 0, 0, 0)),
            pl.BlockSpec((1, nq, W, TQ), lambda b, i: (b, 0, 0, 0)),
            pl.BlockSpec((H_B, 2, TQ, TQ), lambda b, i: (0, 0, 0, 0)),
            pl.BlockSpec((4, D_B), lambda b, i: (0, 0)),
            pl.BlockSpec((2 * D_B, 1), lambda b, i: (0, 0)),
        ],
        out_specs=pl.BlockSpec((1, TQ, W), lambda b, i: (b, i, 0)),
        out_shape=jax.ShapeDtypeStruct((B, T, W), F32),
        scratch_shapes=[pltpu.VMEM((G * 2 * D_B, TQ), F32),
                        pltpu.VMEM((G, TQ, TQ), F32), pltpu.VMEM((G, TQ, TQ), BF16)],
        compiler_params=_cparams(("arbitrary", "arbitrary")),
        name="attn_b_prompt",
    )(qT, k, vT, band, lamv, gcol)


def _attn_m_kernel(q_ref, k_ref, v_ref, o_ref):
    for h in range(H_M):
        s = _dot_nt(q_ref[0, h], k_ref[0, h])
        m = jnp.max(s, axis=-1, keepdims=True)
        p = jnp.exp(s - m)
        l = jnp.sum(p, axis=-1, keepdims=True)
        o_ref[0, :, h * D_M:(h + 1) * D_M] = _dot(p.astype(BF16), v_ref[0, h]) / l


def attn_m_prompt(q, k, v):
    B, H, T, D = q.shape
    NM = k.shape[2]
    tq = 512
    return pl.pallas_call(
        _attn_m_kernel,
        grid=(B, T // tq),
        in_specs=[
            pl.BlockSpec((1, H, tq, D), lambda b, i: (b, 0, i, 0)),
            pl.BlockSpec((1, H, NM, D), lambda b, i: (b, 0, 0, 0)),
            pl.BlockSpec((1, H, NM, D), lambda b, i: (b, 0, 0, 0)),
        ],
        out_specs=pl.BlockSpec((1, tq, M_WIDTH), lambda b, i: (b, i, 0)),
        out_shape=jax.ShapeDtypeStruct((B, T, M_WIDTH), F32),
        compiler_params=_cparams(("arbitrary", "arbitrary")),
        name="attn_m_prompt",
    )(q, k, v)


def _merge_kernel(x_ref, oa_ref, ob_ref, om_ref, g0_ref, g1_ref, g2_ref,
                  wa_ref, wb_ref, wm_ref, wo_ref, lg_ref, lb_ref, o_ref, *, alpha):
    ya = _dot(oa_ref[...].astype(BF16), wa_ref[...])
    yb = _dot(ob_ref[...].astype(BF16), wb_ref[...])
    ym = _dot(om_ref[...].astype(BF16), wm_ref[...])
    hmix = (jax.nn.sigmoid(g0_ref[...]) * ya + jax.nn.sigmoid(g1_ref[...]) * yb
            + jax.nn.sigmoid(g2_ref[...]) * ym)
    mix = _dot(hmix.astype(BF16), wo_ref[...])
    o_ref[...] = _layer_norm(alpha * x_ref[...] + mix, lg_ref[...], lb_ref[...])


def merge(x, oa, ob, om, hproj, gate_col0, wa, wb, wm, wo, lg, lb, alpha, tm, name="merge"):
    M, D = x.shape
    row = lambda i: (i, 0)
    const = lambda i: (0, 0)
    gspec = [pl.BlockSpec((tm, D), (lambda i, c=c: (i, gate_col0 + c))) for c in range(3)]
    return pl.pallas_call(
        functools.partial(_merge_kernel, alpha=alpha),
        grid=(M // tm,),
        in_specs=[pl.BlockSpec((tm, D), row),
                  pl.BlockSpec((tm, HEAD_COLS), row), pl.BlockSpec((tm, HEAD_COLS), row),
                  pl.BlockSpec((tm, HEAD_COLS), row)] + gspec + [
                  pl.BlockSpec((HEAD_COLS, D), const), pl.BlockSpec((HEAD_COLS, D), const),
                  pl.BlockSpec((HEAD_COLS, D), const), pl.BlockSpec((D, D), const),
                  pl.BlockSpec((1, D), const), pl.BlockSpec((1, D), const)],
        out_specs=pl.BlockSpec((tm, D), row),
        out_shape=jax.ShapeDtypeStruct((M, D), F32),
        compiler_params=_cparams(("arbitrary",)),
        name=name,
    )(x, oa, ob, om, hproj, hproj, hproj, wa, wb, wm, wo, lg, lb)


def _ffn_kernel(x_ref, rw_ref, wg_ref, wu_ref, wd_ref, lg_ref, lb_ref, o_ref,
                acc_ref, gate_ref, *, alpha, routed, n_experts):
    e = pl.program_id(1)
    f = pl.program_id(2)
    first = jnp.logical_and(e == 0, f == 0)
    last = jnp.logical_and(e == pl.num_programs(1) - 1, f == pl.num_programs(2) - 1)
    x = x_ref[...]

    @pl.when(first)
    def _():
        acc_ref[...] = jnp.zeros_like(acc_ref)
        if routed:
            logits = _dot(x.astype(BF16), rw_ref[...])
            i1, i2, w1, w2 = _top2(logits, n_experts)
            lane = lax.broadcasted_iota(jnp.int32, logits.shape, 1)
            gate_ref[...] = jnp.where(lane == i1, w1, 0.0) + jnp.where(lane == i2, w2, 0.0)

    xin = x.astype(BF16)
    hg = _dot(xin, wg_ref[0])
    hu = _dot(xin, wu_ref[0])
    hh = hg * jax.nn.sigmoid(hg) * hu
    part = _dot(hh.astype(BF16), wd_ref[0])
    if routed:
        gate = gate_ref[...]
        lane = lax.broadcasted_iota(jnp.int32, gate.shape, 1)
        part = part * jnp.sum(jnp.where(lane == e, gate, 0.0), axis=-1, keepdims=True)
    acc_ref[...] += part

    @pl.when(last)
    def _():
        o_ref[...] = _layer_norm(alpha * x + acc_ref[...], lg_ref[...], lb_ref[...])


def ffn(x, rw, wg, wu, wd, lg, lb, alpha, tm, tf, routed, name="ffn"):
    M, D = x.shape
    E, _, F = wg.shape
    assert M % tm == 0 and F % tf == 0
    return pl.pallas_call(
        functools.partial(_ffn_kernel, alpha=alpha, routed=routed, n_experts=E),
        grid=(M // tm, E, F // tf),
        in_specs=[pl.BlockSpec((tm, D), lambda i, e, f: (i, 0)),
                  pl.BlockSpec(rw.shape, lambda i, e, f: (0, 0)),
                  pl.BlockSpec((1, D, tf), lambda i, e, f: (e, 0, f)),
                  pl.BlockSpec((1, D, tf), lambda i, e, f: (e, 0, f)),
                  pl.BlockSpec((1, tf, D), lambda i, e, f: (e, f, 0)),
                  pl.BlockSpec((1, D), lambda i, e, f: (0, 0)),
                  pl.BlockSpec((1, D), lambda i, e, f: (0, 0))],
        out_specs=pl.BlockSpec((tm, D), lambda i, e, f: (i, 0)),
        out_shape=jax.ShapeDtypeStruct((M, D), F32),
        scratch_shapes=[pltpu.VMEM((tm, D), F32), pltpu.VMEM((tm, rw.shape[1]), F32)],
        compiler_params=_cparams(("arbitrary", "arbitrary", "arbitrary")),
        name=name,
    )(x, rw, wg, wu, wd, lg, lb)


MOE_ROW_TILE = 512
FF_TILE_MAX = 1408


def _ff_tile(d_ff):
    return max(t for t in range(128, min(d_ff, FF_TILE_MAX) + 1, 128) if d_ff % t == 0)
SC_WINDOW = 128
SC_ROW = 128


def _top2(logits, n_experts):
    ne = logits.shape[-1]
    lane = lax.broadcasted_iota(jnp.int32, logits.shape, 1)
    logits = jnp.where(lane < n_experts, logits, -jnp.inf)
    m1 = jnp.max(logits, axis=-1, keepdims=True)
    i1 = jnp.min(jnp.where(logits == m1, lane, ne), axis=-1, keepdims=True)
    rest = jnp.where(lane == i1, -jnp.inf, logits)
    m2 = jnp.max(rest, axis=-1, keepdims=True)
    i2 = jnp.min(jnp.where(rest == m2, lane, ne), axis=-1, keepdims=True)
    e2 = jnp.exp(m2 - m1)
    return i1, i2, 1.0 / (1.0 + e2), e2 / (1.0 + e2)


def _route_kernel(x_ref, rw_ref, low_ref, e_ref, r_ref, w_ref, cnt_ref, run_ref, *, n_experts):
    @pl.when(pl.program_id(0) == 0)
    def _():
        run_ref[...] = jnp.zeros_like(run_ref)

    logits = _dot(x_ref[...].astype(BF16), rw_ref[...])
    i1, i2, w1, w2 = _top2(logits, n_experts)
    lane = lax.broadcasted_iota(jnp.int32, logits.shape, 1)
    oh1 = jnp.where(lane == i1, 1.0, 0.0)
    oh2 = jnp.where(lane == i2, 1.0, 0.0)
    oh = oh1 + oh2
    pre = _dot(low_ref[...], oh.astype(BF16)) + run_ref[...]
    r1 = jnp.sum(oh1 * pre, axis=-1, keepdims=True)
    r2 = jnp.sum(oh2 * pre, axis=-1, keepdims=True)
    run_ref[...] = run_ref[...] + jnp.sum(oh, axis=0, keepdims=True)
    first = lax.broadcasted_iota(jnp.int32, e_ref.shape, 1) == 0
    e_ref[...] = jnp.where(first, i1, i2)
    r_ref[...] = jnp.where(first, r1, r2).astype(jnp.int32)
    w_ref[...] = jnp.where(first, w1, w2)
    cnt_ref[...] = run_ref[...]


def moe_route(x, rw, n_experts, tm=512):
    M, D = x.shape
    low = jnp.asarray(np.tril(np.ones((tm, tm), np.float32), -1), BF16)
    pair = pl.BlockSpec((tm, 2), lambda i: (i, 0))
    return pl.pallas_call(
        functools.partial(_route_kernel, n_experts=n_experts),
        grid=(M // tm,),
        in_specs=[pl.BlockSpec((tm, D), lambda i: (i, 0)),
                  pl.BlockSpec(rw.shape, lambda i: (0, 0)),
                  pl.BlockSpec((tm, tm), lambda i: (0, 0))],
        out_specs=[pair, pair, pair, pl.BlockSpec((1, 128), lambda i: (0, 0))],
        out_shape=[jax.ShapeDtypeStruct((M, 2), jnp.int32), jax.ShapeDtypeStruct((M, 2), jnp.int32),
                   jax.ShapeDtypeStruct((M, 2), F32), jax.ShapeDtypeStruct((1, 128), F32)],
        scratch_shapes=[pltpu.VMEM((1, 128), F32)],
        compiler_params=_cparams(("arbitrary",)),
        name="moe_route",
    )(x, rw, low)


def _gmm_kernel(te_ref, nu_ref, x_ref, wg_ref, wu_ref, wd_ref, o_ref, acc_ref):
    i = pl.program_id(0)
    f = pl.program_id(1)

    @pl.when(i < nu_ref[0])
    def _():
        @pl.when(f == 0)
        def _():
            acc_ref[...] = jnp.zeros_like(acc_ref)

        x = x_ref[...].astype(BF16)
        hg = _dot(x, wg_ref[0])
        hu = _dot(x, wu_ref[0])
        hh = hg * jax.nn.sigmoid(hg) * hu
        acc_ref[...] += _dot(hh.astype(BF16), wd_ref[0])

        @pl.when(f == pl.num_programs(1) - 1)
        def _():
            o_ref[...] = acc_ref[...]


def moe_gmm(tile_expert, n_used, xs, wg, wu, wd, tf):
    R, D = xs.shape
    F = wg.shape[2]
    tr = MOE_ROW_TILE
    grid_spec = pltpu.PrefetchScalarGridSpec(
        num_scalar_prefetch=2,
        grid=(R // tr, F // tf),
        in_specs=[pl.BlockSpec((tr, D), lambda i, f, te, nu: (i, 0)),
                  pl.BlockSpec((1, D, tf), lambda i, f, te, nu: (te[i], 0, f)),
                  pl.BlockSpec((1, D, tf), lambda i, f, te, nu: (te[i], 0, f)),
                  pl.BlockSpec((1, tf, D), lambda i, f, te, nu: (te[i], f, 0))],
        out_specs=pl.BlockSpec((tr, D), lambda i, f, te, nu: (i, 0)),
        scratch_shapes=[pltpu.VMEM((tr, D), F32)],
    )
    return pl.pallas_call(
        _gmm_kernel,
        grid_spec=grid_spec,
        out_shape=jax.ShapeDtypeStruct((R, D), F32),
        compiler_params=_cparams(("arbitrary", "arbitrary")),
        name="moe_gmm",
    )(tile_expert, n_used, xs, wg, wu, wd)


def _combine_kernel(x_ref, o1_ref, o2_ref, w_ref, lg_ref, lb_ref, o_ref, *, alpha):
    w = w_ref[...]
    y = w[:, 0:1] * o1_ref[...] + w[:, 1:2] * o2_ref[...]
    o_ref[...] = _layer_norm(alpha * x_ref[...] + y, lg_ref[...], lb_ref[...])


def moe_combine(x, og, w12, lg, lb, alpha, tm=512):
    M, D = x.shape
    nb = M // tm
    return pl.pallas_call(
        functools.partial(_combine_kernel, alpha=alpha),
        grid=(nb,),
        in_specs=[pl.BlockSpec((tm, D), lambda i: (i, 0)),
                  pl.BlockSpec((tm, D), lambda i: (i, 0)),
                  pl.BlockSpec((tm, D), lambda i: (i + nb, 0)),
                  pl.BlockSpec((tm, 2), lambda i: (i, 0)),
                  pl.BlockSpec((1, D), lambda i: (0, 0)),
                  pl.BlockSpec((1, D), lambda i: (0, 0))],
        out_specs=pl.BlockSpec((tm, D), lambda i: (i, 0)),
        out_shape=jax.ShapeDtypeStruct((M, D), F32),
        compiler_params=_cparams(("arbitrary",)),
        name="moe_combine",
    )(x, og, og, w12, lg, lb)


def _sc_mesh():
    return plsc.VectorSubcoreMesh(core_axis_name="c", subcore_axis_name="s")


def sc_scatter_rows(src, idx, n_out_rows):
    S = src.shape[0]
    M = idx.shape[0]
    nb = S // SC_WINDOW
    assert S % SC_WINDOW == 0 and M % S == 0

    @pl.kernel(out_type=jax.ShapeDtypeStruct((n_out_rows, SC_ROW), src.dtype), mesh=_sc_mesh(),
               scratch_types=[])
    def scatter_kernel(x_hbm, i_hbm, o_hbm):
        def body(x_vmem, i_vmem):
            pltpu.sync_copy(x_vmem, o_hbm.at[i_vmem.at[0]])

        pltpu.emit_pipeline(
            body,
            grid=(M // SC_WINDOW,),
            in_specs=[pl.BlockSpec((SC_WINDOW, SC_ROW), lambda i: (lax.rem(i, nb), 0)),
                      pl.BlockSpec((1, SC_WINDOW), lambda i: (0, i))],
            out_specs=[],
            core_axis_name=("c", "s"),
            dimension_semantics=(pltpu.PARALLEL,),
        )(x_hbm, i_hbm)

    return scatter_kernel(src, idx.reshape(1, M))


def sc_gather_rows(table, idx):
    M = idx.shape[0]
    assert M % SC_WINDOW == 0

    @pl.kernel(out_type=jax.ShapeDtypeStruct((M, SC_ROW), table.dtype), mesh=_sc_mesh(),
               scratch_types=[])
    def gather_kernel(t_hbm, i_hbm, o_hbm):
        def body(i_vmem, o_vmem):
            pltpu.sync_copy(t_hbm.at[i_vmem.at[0]], o_vmem)

        pltpu.emit_pipeline(
            body,
            grid=(M // SC_WINDOW,),
            in_specs=[pl.BlockSpec((1, SC_WINDOW), lambda i: (0, i))],
            out_specs=[pl.BlockSpec((SC_WINDOW, SC_ROW), lambda i: (i, 0))],
            core_axis_name=("c", "s"),
            dimension_semantics=(pltpu.PARALLEL,),
        )(i_hbm, o_hbm)

    return gather_kernel(table, idx.reshape(1, M))


def moe_sparse(x, rw, wg, wu, wd, lg, lb, alpha, n_experts):
    N, D = x.shape
    tr = MOE_ROW_TILE
    e12, r12, w12, counts = moe_route(x, rw, n_experts)
    cnt = counts[0, :n_experts].astype(jnp.int32)
    cnt_pad = (cnt + tr - 1) // tr * tr
    off_end = jnp.cumsum(cnt_pad)
    off = off_end - cnt_pad
    R = 2 * N + n_experts * tr
    eids = jnp.arange(n_experts, dtype=jnp.int32)
    pos = jnp.sum(jnp.where(e12[..., None] == eids, off, 0), axis=-1) + r12
    tile_start = jnp.arange(R // tr, dtype=jnp.int32) * tr
    tile_expert = jnp.minimum(jnp.sum(tile_start[:, None] >= off_end[None, :], axis=1),
                              n_experts - 1).astype(jnp.int32)
    n_used = (off_end[-1:] // tr).astype(jnp.int32)
    pos_flat = pos.T.reshape(2 * N)
    idx_s = _chunk_index(pos_flat, D)
    xs = _from_chunks(sc_scatter_rows(_as_chunks(x), idx_s, R * (D // SC_ROW)), R, D)
    o_sorted = moe_gmm(tile_expert, n_used, xs, wg, wu, wd, _ff_tile(wg.shape[2]))
    og = _from_chunks(sc_gather_rows(_as_chunks(o_sorted), idx_s), 2 * N, D)
    return moe_combine(x, og, w12, lg, lb, alpha)


def _as_chunks(a):
    rows, d = a.shape
    return a.reshape(rows // 8, 8, d // SC_ROW, SC_ROW).transpose(0, 2, 1, 3).reshape(-1, SC_ROW)


def _from_chunks(c, rows, d):
    return c.reshape(rows // 8, d // SC_ROW, 8, SC_ROW).transpose(0, 2, 1, 3).reshape(rows, d)


def _chunk_index(row_of, d):
    nk = d // SC_ROW
    r = row_of.reshape(-1, 1, 8)
    k = jnp.arange(nk, dtype=jnp.int32).reshape(1, nk, 1)
    return ((r // 8) * (8 * nk) + k * 8 + r % 8).reshape(-1)


IDX_PAGES_PER_STEP = 32
ATTN_PAGES_PER_STEP = 16


def _page_specs(layer, n, rows, P):
    return [pl.BlockSpec((1, 1, rows, PAGE), (lambda b, s, pt, j=j: (layer, pt[b, s * P + j], 0, 0)))
            for j in range(n)]


def _idx_score_kernel(pt_ref, iqT_ref, iq3_ref, iw_ref, iwc_ref, iknew_ref, *refs, P):
    pages = refs[:P]
    key_ref, knew_ref = refs[P:]
    step = pl.program_id(1)
    wic = iwc_ref[0] * (H_IDX ** -0.5)
    iq3 = iq3_ref[0]
    for j in range(P):
        kt = pages[j][0, 0]
        hi = kt.astype(BF16)
        lo = (kt - hi.astype(F32)).astype(BF16)
        d = _dot(iq3, jnp.concatenate([hi, lo, hi], axis=0))
        sc = jnp.sum(wic * jnp.maximum(d, 0.0), axis=0, keepdims=True)
        key_ref[0, pl.ds(step * P + j, 1), :] = _orderable(sc + 0.0)

    @pl.when(step == pl.num_programs(1) - 1)
    def _():
        iqT = iqT_ref[0] * (D_IDX ** -0.5)
        wi = iw_ref[0] * (H_IDX ** -0.5)
        dn = jnp.sum(iqT * iknew_ref[0], axis=0, keepdims=True)
        sn = jnp.sum(wi * jnp.maximum(dn, 0.0), axis=-1, keepdims=True) + 0.0
        knew_ref[0] = jnp.broadcast_to(_orderable(sn), knew_ref.shape[1:])


def _idx_select_kernel(key_ref, knew_ref, tri_ref, low_ref, am_ref, amnew_ref, *, topk, n_samples):
    s = key_ref[...]
    npg = s.shape[0] // n_samples
    kn = knew_ref[...][:, 0:1]

    def per_sample(x):
        return jnp.sum(jnp.sum(x.reshape(n_samples, npg, x.shape[-1]), axis=1), axis=-1, keepdims=True)

    def count_ge(cand):
        t = jnp.where(s >= _expand_rows(cand, npg), 1.0, 0.0)
        return per_sample(t) + jnp.where(kn >= cand, 1.0, 0.0)

    thr = _kth_largest(count_ge, (n_samples, 1), topk)
    thr_r = _expand_rows(thr, npg)
    c_gt = per_sample(jnp.where(s > thr_r, 1.0, 0.0)) + jnp.where(kn > thr, 1.0, 0.0)
    need = topk - c_gt
    tie = s == thr_r
    tie_f = jnp.where(tie, 1.0, 0.0)
    in_row = _dot(tie_f.astype(BF16), tri_ref[...])
    row_tot = jnp.broadcast_to(jnp.sum(tie_f, axis=-1, keepdims=True), tie_f.shape)
    before = jnp.einsum('bpq,bql->bpl', low_ref[...],
                        row_tot.astype(BF16).reshape(n_samples, npg, PAGE),
                        preferred_element_type=F32).reshape(s.shape)
    keep_tie = jnp.where(in_row + before < _expand_rows(need, npg), 0.0, NEG)
    am_ref[...] = jnp.where(s > thr_r, 0.0, jnp.where(tie, keep_tie, NEG))
    keep_new = jnp.where(per_sample(tie_f) < need, 0.0, NEG)
    amn = jnp.where(kn > thr, 0.0, jnp.where(kn == thr, keep_new, NEG))
    amnew_ref[...] = jnp.broadcast_to(amn, amnew_ref.shape)


def idx_sample(page_table, layer, iq, iw, ik_new, poolT, topk):
    DB, npg = page_table.shape
    iqT = jnp.transpose(iq, (0, 2, 1))
    iq3 = _split3(iq * (D_IDX ** -0.5), (0, 0, 1))
    iwc = iw.reshape(DB, H_IDX, 1)
    iw = iw.reshape(DB, 1, H_IDX)
    ik_new = ik_new.reshape(DB, D_IDX, 1)
    P = min(IDX_PAGES_PER_STEP, npg)
    assert npg % P == 0
    per_b = lambda b, s, pt: (b, 0, 0)
    grid_spec = pltpu.PrefetchScalarGridSpec(
        num_scalar_prefetch=1,
        grid=(DB, npg // P),
        in_specs=[pl.BlockSpec((1, D_IDX, H_IDX), per_b),
                  pl.BlockSpec((1, H_IDX, IDX_SPLIT), per_b),
                  pl.BlockSpec((1, 1, H_IDX), per_b),
                  pl.BlockSpec((1, H_IDX, 1), per_b),
                  pl.BlockSpec((1, D_IDX, 1), per_b)]
                 + _page_specs(layer, P, D_IDX, P),
        out_specs=[pl.BlockSpec((1, npg, PAGE), per_b),
                   pl.BlockSpec((1, 1, PAGE), per_b)],
    )
    keys, knew = pl.pallas_call(
        functools.partial(_idx_score_kernel, P=P),
        grid_spec=grid_spec,
        out_shape=[jax.ShapeDtypeStruct((DB, npg, PAGE), jnp.int32),
                   jax.ShapeDtypeStruct((DB, 1, PAGE), jnp.int32)],
        compiler_params=_cparams(("arbitrary", "arbitrary")),
        name="idx_score",
    )(page_table, iqT, iq3, iw, iwc, ik_new, *([poolT] * P))
    tri = jnp.asarray(np.triu(np.ones((PAGE, PAGE), np.float32), 1), BF16)
    low = jnp.asarray(np.broadcast_to(np.tril(np.ones((npg, npg), np.float32), -1), (DB, npg, npg)), BF16)
    whole = lambda shape: pl.BlockSpec(shape, lambda i: (0,) * len(shape))
    am, amn = pl.pallas_call(
        functools.partial(_idx_select_kernel, topk=float(topk), n_samples=DB),
        grid=(1,),
        in_specs=[whole((DB * npg, PAGE)), whole((DB, PAGE)), whole((PAGE, PAGE)), whole((DB, npg, npg))],
        out_specs=[whole((DB * npg, PAGE)), whole((DB, PAGE))],
        out_shape=[jax.ShapeDtypeStruct((DB * npg, PAGE), F32), jax.ShapeDtypeStruct((DB, PAGE), F32)],
        compiler_params=_cparams(("arbitrary",)),
        name="idx_select",
    )(keys.reshape(DB * npg, PAGE), knew.reshape(DB, PAGE), tri, low)
    return am.reshape(DB, npg, PAGE), amn.reshape(DB, 1, PAGE)


def _expand_rows(x, rep):
    g, n = x.shape
    return jnp.broadcast_to(x[:, None, :], (g, rep, n)).reshape(g * rep, n)


def _group_logits(kt, qb):
    w, n = kt.shape
    return jnp.sum((kt * qb).reshape(N_GROUPS, w // N_GROUPS, n), axis=1)


def _dec_a_kernel(pt_ref, q_ref, am_ref, amnew_ref, knew_ref, vnew_ref, blast_ref, bnew_ref,
                  *refs, scale, P):
    kp = refs[:P]
    vp = refs[P:2 * P]
    o_ref, qb_ref, m_ref, l_ref, acc_ref = refs[2 * P:]
    step = pl.program_id(1)
    is_last = step == pl.num_programs(1) - 1
    W = qb_ref.shape[0]
    rep = W // N_GROUPS

    @pl.when(step == 0)
    def _():
        qb_ref[...] = jnp.broadcast_to(q_ref[0] * scale, qb_ref.shape)
        m_ref[...] = jnp.full_like(m_ref, NEG)
        l_ref[...] = jnp.zeros_like(l_ref)
        acc_ref[...] = jnp.zeros_like(acc_ref)

    last_f = jnp.where(is_last, 1.0, 0.0)
    for j in range(P):
        lg = _group_logits(kp[j][0, 0], qb_ref[...]) + am_ref[0, j:j + 1, :]
        if j == P - 1:
            lg = lg + last_f * blast_ref[...]
        m_old = m_ref[...]
        m_new = jnp.maximum(m_old, lg)
        alpha = jnp.exp(m_old - m_new)
        p = jnp.exp(lg - m_new)
        m_ref[...] = m_new
        l_ref[...] = l_ref[...] * alpha + p
        acc_ref[...] = acc_ref[...] * _expand_rows(alpha, rep) + _expand_rows(p, rep) * vp[j][0, 0]

    @pl.when(is_last)
    def _():
        q = q_ref[0] * scale
        lgn = (jnp.sum((knew_ref[0] * q).reshape(N_GROUPS, rep, 1), axis=1)
               + bnew_ref[...] + amnew_ref[0][:, 0:1])
        m = m_ref[...]
        mx = jnp.maximum(jnp.max(m, axis=-1, keepdims=True), lgn)
        w = jnp.exp(m - mx)
        pn = jnp.exp(lgn - mx)
        l = jnp.sum(l_ref[...] * w, axis=-1, keepdims=True) + pn
        o = (jnp.sum(acc_ref[...] * _expand_rows(w, rep), axis=-1, keepdims=True)
             + _expand_rows(pn, rep) * vnew_ref[0])
        o_ref[0] = o / _expand_rows(l, rep)


def decode_a(page_table, layer, qcol, am, amnew, knew, vnew, blast, bnew, kT, vT, scale):
    DB, npg = page_table.shape
    W = qcol.shape[1]
    P = min(ATTN_PAGES_PER_STEP, npg)
    assert npg % P == 0
    per_b = lambda b, s, pt: (b, 0, 0)
    c2 = lambda b, s, pt: (0, 0)
    grid_spec = pltpu.PrefetchScalarGridSpec(
        num_scalar_prefetch=1,
        grid=(DB, npg // P),
        in_specs=[pl.BlockSpec((1, W, 1), per_b),
                  pl.BlockSpec((1, P, PAGE), lambda b, s, pt: (b, s, 0)),
                  pl.BlockSpec((1, 1, PAGE), per_b),
                  pl.BlockSpec((1, W, 1), per_b),
                  pl.BlockSpec((1, W, 1), per_b),
                  pl.BlockSpec((N_GROUPS, PAGE), c2),
                  pl.BlockSpec((N_GROUPS, 1), c2)]
                 + _page_specs(layer, P, W, P) + _page_specs(layer, P, W, P),
        out_specs=pl.BlockSpec((1, W, 1), per_b),
        scratch_shapes=[pltpu.VMEM((W, PAGE), F32), pltpu.VMEM((N_GROUPS, PAGE), F32),
                        pltpu.VMEM((N_GROUPS, PAGE), F32), pltpu.VMEM((W, PAGE), F32)],
    )
    return pl.pallas_call(
        functools.partial(_dec_a_kernel, scale=scale, P=P),
        grid_spec=grid_spec,
        out_shape=jax.ShapeDtypeStruct((DB, W, 1), F32),
        compiler_params=_cparams(("arbitrary", "arbitrary")),
        name="decode_a",
    )(page_table, qcol, am, amnew, knew, vnew, blast, bnew, *([kT] * P), *([vT] * P))


def _dec_b_kernel(pt_ref, q_ref, knew_ref, vnew_ref, blast_ref, bnew_ref, r_ref, pm_ref,
                  lam_ref, g_ref, *refs, scale, lam_init, P):
    kp = refs[:P]
    vp = refs[P:2 * P]
    o_ref, qb_ref, m_ref, l_ref, acc_ref = refs[2 * P:]
    step = pl.program_id(1)
    is_last = step == pl.num_programs(1) - 1
    W = qb_ref.shape[0]
    rep = W // N_GROUPS

    @pl.when(step == 0)
    def _():
        qb_ref[...] = jnp.broadcast_to(q_ref[0] * scale, qb_ref.shape)
        m_ref[...] = jnp.full_like(m_ref, NEG)
        l_ref[...] = jnp.zeros_like(l_ref)
        acc_ref[...] = jnp.zeros_like(acc_ref)

    last_f = jnp.where(is_last, 1.0, 0.0)
    lgs = [_group_logits(kp[j][0, 0], qb_ref[...]) for j in range(P)]
    lgs[P - 1] = lgs[P - 1] + last_f * blast_ref[...]
    m_old = m_ref[...]
    m_new = m_old
    for lg in lgs:
        m_new = jnp.maximum(m_new, jnp.max(lg, axis=-1, keepdims=True))
    alpha = jnp.exp(m_old - m_new)
    ps = [jnp.exp(lg - m_new) for lg in lgs]
    m_ref[...] = m_new
    l_ref[...] = l_ref[...] * alpha + sum(jnp.sum(p, axis=-1, keepdims=True) for p in ps)
    pr = _dot(jnp.concatenate(ps, axis=0).astype(BF16), r_ref[...])
    o = jnp.zeros(acc_ref.shape, F32)
    for j in range(P):
        prj = (pr[j * N_GROUPS:(j + 1) * N_GROUPS] * pm_ref[...]).astype(BF16)
        o = o + _dot(prj, vp[j][0, 0].astype(BF16))
    acc_ref[...] = acc_ref[...] * alpha + o

    @pl.when(is_last)
    def _():
        q = q_ref[0] * scale
        lgn = jnp.sum((knew_ref[0] * q).reshape(N_GROUPS, rep, 1), axis=1) + bnew_ref[...]
        m = m_ref[...]
        mx = jnp.maximum(m, lgn)
        a2 = jnp.exp(m - mx)
        pn = jnp.exp(lgn - mx)
        l = l_ref[...] * a2 + pn
        on = (acc_ref[...] * a2 + pn * vnew_ref[0]) / l
        lam = _lambda(lam_ref[...], lam_init)
        r = lax.broadcasted_iota(jnp.int32, (N_GROUPS, N_GROUPS), 0)
        c = lax.broadcasted_iota(jnp.int32, (N_GROUPS, N_GROUPS), 1)
        comb = jnp.where(c == 2 * r, 1.0, 0.0) - lam * jnp.where(c == 2 * r + 1, 1.0, 0.0)
        o = _dot(comb, on, True)
        o = o * lax.rsqrt(jnp.mean(o * o, axis=-1, keepdims=True) + SUBLN_EPS)
        o_ref[0] = o * g_ref[...] * (1.0 - lam_init)


def decode_b(page_table, layer, qcol, knew, vnew_g, blast, bnew, rmat, pmask, lamv, g, kT, vrows,
             scale, lam_init):
    DB, npg = page_table.shape
    W = qcol.shape[1]
    P = min(ATTN_PAGES_PER_STEP, npg)
    dv = 2 * D_B
    assert npg % P == 0
    per_b = lambda b, s, pt: (b, 0, 0)
    c2 = lambda b, s, pt: (0, 0)
    grid_spec = pltpu.PrefetchScalarGridSpec(
        num_scalar_prefetch=1,
        grid=(DB, npg // P),
        in_specs=[pl.BlockSpec((1, W, 1), per_b),
                  pl.BlockSpec((1, W, 1), per_b),
                  pl.BlockSpec((1, N_GROUPS, dv), per_b),
                  pl.BlockSpec((N_GROUPS, PAGE), c2),
                  pl.BlockSpec((N_GROUPS, 1), c2),
                  pl.BlockSpec((PAGE, W), c2),
                  pl.BlockSpec((N_GROUPS, W), c2),
                  pl.BlockSpec((4, D_B), c2),
                  pl.BlockSpec((1, dv), c2)]
                 + _page_specs(layer, P, W, P) + _page_specs(layer, P, W, P),
        out_specs=pl.BlockSpec((1, N_GROUPS, dv), per_b),
        scratch_shapes=[pltpu.VMEM((W, PAGE), F32), pltpu.VMEM((N_GROUPS, 1), F32),
                        pltpu.VMEM((N_GROUPS, 1), F32), pltpu.VMEM((N_GROUPS, dv), F32)],
    )
    return pl.pallas_call(
        functools.partial(_dec_b_kernel, scale=scale, lam_init=lam_init, P=P),
        grid_spec=grid_spec,
        out_shape=jax.ShapeDtypeStruct((DB, N_GROUPS, dv), F32),
        compiler_params=_cparams(("arbitrary", "arbitrary")),
        name="decode_b",
    )(page_table, qcol, knew, vnew_g, blast, bnew, rmat, pmask, lamv, g,
      *([kT] * P), *([vrows] * P))


def _dec_m_kernel(q_ref, k_ref, v_ref, o_ref):
    n = k_ref.shape[2] // 8
    q = q_ref[0]
    k3 = k_ref[0, 0].reshape(n, 8, D_M)
    lg = jnp.sum(k3 * q[None], axis=-1, keepdims=True)
    m8 = jnp.max(lg, axis=0)
    m4 = jnp.maximum(m8[:H_M], m8[H_M:])
    p = jnp.exp(lg - jnp.concatenate([m4, m4], axis=0)[None])
    l8 = jnp.sum(p, axis=0)
    o8 = jnp.sum(p * v_ref[0, 0].reshape(n, 8, D_M), axis=0)
    o_ref[0] = (o8[:H_M] + o8[H_M:]) / (l8[:H_M] + l8[H_M:])


def decode_m(layer, q8, mem_k, mem_v):
    DB = q8.shape[0]
    rows = mem_k.shape[2]
    return pl.pallas_call(
        _dec_m_kernel,
        grid=(DB,),
        in_specs=[pl.BlockSpec((1, 8, D_M), lambda b: (b, 0, 0)),
                  pl.BlockSpec((1, 1, rows, D_M), lambda b: (layer, b, 0, 0)),
                  pl.BlockSpec((1, 1, rows, D_M), lambda b: (layer, b, 0, 0))],
        out_specs=pl.BlockSpec((1, H_M, D_M), lambda b: (b, 0, 0)),
        out_shape=jax.ShapeDtypeStruct((DB, H_M, D_M), F32),
        compiler_params=_cparams(("arbitrary",)),
        name="decode_m",
    )(q8, mem_k, mem_v)


def _t5_bucket_np(rel):
    n = np.maximum(rel, 0)
    max_exact = NUM_BUCKETS // 2
    nf = np.maximum(n, 1).astype(np.float32)
    large = max_exact + (np.log(nf / np.float32(max_exact))
                         / np.float32(math.log(MAX_DISTANCE / max_exact))
                         * np.float32(NUM_BUCKETS - max_exact)).astype(np.int32)
    large = np.minimum(large, NUM_BUCKETS - 1)
    return np.where(n < max_exact, n, large).astype(np.int32)


def _far_bucket(first_far, last_far):
    b = _t5_bucket_np(np.arange(first_far, last_far + 1))
    assert (b == b[0]).all(), "relative-position bias must be constant beyond one tile"
    return int(b[0])


def _bias_lookup(rel_bias, buckets):
    onehot = jax.nn.one_hot(jnp.asarray(buckets, jnp.int32), NUM_BUCKETS, dtype=F32)
    return jnp.einsum('...b,bh->...h', onehot, rel_bias, precision=HIGHEST)


def _band_tables(rel_bias, heads, t_max):
    far = _far_bucket(TQ + 1, max(t_max, TQ + 1))
    r = np.arange(TQ)[None, :] - np.arange(TQ)[:, None]
    tabs = []
    for d in range(2):
        rel = r + d * TQ
        vals = _bias_lookup(rel_bias, _t5_bucket_np(rel))[..., heads] - rel_bias[far][heads]
        vals = jnp.where((rel >= 0)[..., None], vals, NEG)
        tabs.append(jnp.transpose(vals, (2, 0, 1)))
    return jnp.stack(tabs, axis=1).astype(F32)


def _split3(x, order):
    hi = x.astype(BF16)
    lo = (x - hi.astype(F32)).astype(BF16)
    return jnp.concatenate([(hi, lo)[o] for o in order], axis=-1)


def kernel(x_prompt, x_sample, cache_a_k, cache_a_v, cache_idx_k, cache_b_k, cache_b_v, cache_mem_k, cache_mem_v, page_table, mem_prompt, w_in, w_mem_kv, lambda_q1, lambda_k1, lambda_q2, lambda_k2, subln_g, w_branch_a, w_branch_b, w_branch_m, w_out, rel_bias, ln1_g, ln1_b, ln2_g, ln2_b, ffn_w_gate, ffn_w_up, ffn_w_down, router_w, expert_w_gate, expert_w_up, expert_w_down):
    B, T, D = x_prompt.shape
    DB, TS, _ = x_sample.shape
    depth = w_in.shape[0]
    n_pool = cache_a_k.shape[1]
    npg = page_table.shape[1]
    past = npg * PAGE
    n_mem = mem_prompt.shape[1]
    assert TS == 1 and T % TQ == 0 and cache_a_k.shape[2] == PAGE
    alpha = (2 * depth) ** 0.25
    topk_p = min(TOPK_MAX, T // 4)
    topk_s = min(TOPK_MAX, (past + TS) // 4)
    W = HEAD_COLS
    nq = T // TQ

    c_ik = 3 * A_WIDTH + H_IDX * D_IDX
    c_b = c_ik + D_IDX + H_IDX
    n_main = 8 * W + 3 * D
    tail_pad = 128 - D_IDX - H_IDX

    low = jnp.asarray(np.tril(np.ones((TQ, TQ), np.float32), -1), BF16)
    band_a = _band_tables(rel_bias, np.arange(H_A), T)
    band_b = _band_tables(rel_bias, H_A + np.arange(H_B), T)

    far_s = _far_bucket(PAGE + 1, past)
    b_last = (_bias_lookup(rel_bias, _t5_bucket_np(PAGE - np.arange(PAGE))) - rel_bias[far_s]).T
    b_new = (rel_bias[0] - rel_bias[far_s])[:, None]
    blast_a, bnew_a = b_last[:H_A], b_new[:H_A]
    blast_b, bnew_b = jnp.repeat(b_last[H_A:], 2, axis=0), jnp.repeat(b_new[H_A:], 2, axis=0)
    lane = np.arange(W)
    rmat = jnp.asarray(lane[None, :] // H_B == np.arange(PAGE)[:, None], BF16)
    pmask = jnp.asarray(lane[None, :] % H_B == np.arange(N_GROUPS)[:, None] // 2, F32)

    idxT = jnp.transpose(cache_idx_k, (0, 1, 3, 2))
    akT = jnp.transpose(cache_a_k, (0, 1, 3, 4, 2)).reshape(depth, n_pool, W, PAGE)
    avT = jnp.transpose(cache_a_v, (0, 1, 3, 4, 2)).reshape(depth, n_pool, W, PAGE)
    bkT = jnp.transpose(cache_b_k, (0, 1, 3, 4, 5, 2)).reshape(depth, n_pool, W, PAGE)
    bv_rows = cache_b_v.reshape(depth, n_pool, PAGE * H_B, 2 * D_B)
    memk_rows = cache_mem_k.reshape(depth, DB, n_mem * H_M, D_M)
    memv_rows = cache_mem_v.reshape(depth, DB, n_mem * H_M, D_M)

    outs = {n: [] for n in ('akp', 'avp', 'ikp', 'bkp', 'bvp', 'mkp', 'mvp',
                            'aks', 'avs', 'iks', 'bks', 'bvs')}
    x_p = x_prompt.reshape(B * T, D)
    x_s = x_sample.reshape(DB, D)
    for l in range(depth):
        lam_init = 0.8 - 0.6 * math.exp(-0.3 * l)
        lamv = jnp.stack([lambda_q1[l], lambda_k1[l], lambda_q2[l], lambda_k2[l]])
        wl = w_in[l]
        w_perm = jnp.concatenate([wl[:, :c_ik], wl[:, c_b:], wl[:, c_ik:c_b],
                                  jnp.zeros((D, tail_pad), F32)], axis=1)
        w_main_bf = w_perm[:, :n_main].astype(BF16)
        w_tail_bf = w_perm[:, n_main:].astype(BF16)
        wa_bf, wb_bf, wm_bf, wo_bf = (w[l].astype(BF16) for w in (w_branch_a, w_branch_b, w_branch_m, w_out))
        lg1, lb1 = ln1_g[l][None], ln1_b[l][None]
        lg2, lb2 = ln2_g[l][None], ln2_b[l][None]
        g_sub = subln_g[l][None]
        j = l // 2
        if l % 2 == 0:
            rw = jnp.zeros((D, 128), BF16)
            wg, wu, wd = ffn_w_gate[j][None], ffn_w_up[j][None], ffn_w_down[j][None]
            routed = False
        else:
            rw = jnp.pad(router_w[j], ((0, 0), (0, 128 - N_EXPERTS))).astype(BF16)
            wg, wu, wd = expert_w_gate[j], expert_w_up[j], expert_w_down[j]
            routed = True

        h = matmul(x_p, w_main_bf, 1024 if B * T % 1024 == 0 else 512, 1024,
                   name="proj_prompt")
        tail = matmul(x_p, w_tail_bf, 512, 128, name="proj_tail")
        h3 = h.reshape(B, T, n_main)
        col = lambda c: h3[:, :, c * W:(c + 1) * W]

        def heads(c, nh, dh, dt):
            return col(c).reshape(B, T, nh, dh).transpose(0, 2, 1, 3).astype(dt)

        def q_t(c, scale):
            return jnp.transpose(col(c) * scale, (0, 2, 1)).astype(BF16)

        def v_t(c):
            return col(c).reshape(B, nq, TQ, W).transpose(0, 1, 3, 2).astype(BF16)

        a_k, a_v, b_k, b_v = col(1), col(2), col(5), col(6)
        i_k = tail[:, :D_IDX].reshape(B, T, D_IDX)
        i_w = tail[:, D_IDX:D_IDX + H_IDX].reshape(B, T, H_IDX)
        iq = col(3).reshape(B, T, H_IDX, D_IDX) * (D_IDX ** -0.5)
        iqT = jnp.transpose(_split3(iq, (0, 0, 1)), (0, 2, 3, 1))
        ik3 = _split3(i_k, (0, 1, 0))
        wT = jnp.transpose(i_w * (H_IDX ** -0.5), (0, 2, 1))
        o_a = attn_a_prompt(q_t(0, D_A ** -0.5), heads(1, H_A, D_A, BF16), v_t(2), iqT, ik3, wT,
                            band_a, low, topk_p)
        o_b = attn_b_prompt(q_t(4, D_B ** -0.5), heads(5, 2 * H_B, D_B, BF16), v_t(6),
                            band_b, lamv, g_sub.T, lam_init)
        kv = matmul(mem_prompt.reshape(B * n_mem, D), w_mem_kv[l].astype(BF16), 512, 1024,
                    name="proj_mem")
        mk = kv[:, :M_WIDTH].reshape(B, n_mem, H_M, D_M)
        mv = kv[:, M_WIDTH:].reshape(B, n_mem, H_M, D_M)
        qm = (heads(7, H_M, D_M, F32) * (D_M ** -0.5)).astype(BF16)
        o_m = attn_m_prompt(qm, mk.transpose(0, 2, 1, 3).astype(BF16),
                            mv.transpose(0, 2, 1, 3).astype(BF16))
        x_p = merge(x_p, o_a.reshape(B * T, W), o_b.reshape(B * T, W), o_m.reshape(B * T, W),
                    h, 8 * W // D, wa_bf, wb_bf, wm_bf, wo_bf, lg1, lb1, alpha, 512, name="merge_prompt")
        x_p_mid = x_p
        outs['akp'].append(a_k.reshape(B, T, H_A, D_A))
        outs['avp'].append(a_v.reshape(B, T, H_A, D_A))
        outs['ikp'].append(i_k)
        outs['bkp'].append(b_k.reshape(B, T, H_B, 2, D_B))
        outs['bvp'].append(b_v.reshape(B, T, H_B, 2 * D_B))
        outs['mkp'].append(mk)
        outs['mvp'].append(mv)

        hs = matmul(x_s, w_main_bf, DB, 1024, name="proj_sample")
        ts = matmul(x_s, w_tail_bf, DB, 128, name="proj_tail_sample")
        sa_q, sa_k, sa_v, s_iq = (hs[:, c * W:(c + 1) * W] for c in range(4))
        sb_q, sb_k, sb_v, sm_q = (hs[:, c * W:(c + 1) * W] for c in range(4, 8))
        s_ik = ts[:, :D_IDX]
        s_iw = ts[:, D_IDX:D_IDX + H_IDX]
        colv = lambda t: t.reshape(DB, -1, 1)
        am, amn = idx_sample(page_table, l, s_iq.reshape(DB, H_IDX, D_IDX), s_iw, s_ik, idxT, topk_s)
        o_a = decode_a(page_table, l, colv(sa_q), am, amn, colv(sa_k), colv(sa_v),
                       blast_a, bnew_a, akT, avT, D_A ** -0.5).reshape(DB, W)
        vnew_g = jnp.repeat(sb_v.reshape(DB, H_B, 2 * D_B), 2, axis=1)
        o_b = decode_b(page_table, l, colv(sb_q), colv(sb_k), vnew_g, blast_b, bnew_b, rmat, pmask,
                       lamv, g_sub, bkT, bv_rows, D_B ** -0.5, lam_init)[:, :H_B].reshape(DB, W)
        q8 = jnp.tile(sm_q.reshape(DB, H_M, D_M) * (D_M ** -0.5), (1, 2, 1))
        o_m = decode_m(l, q8, memk_rows, memv_rows).reshape(DB, W)
        x_s = merge(x_s, o_a, o_b, o_m, hs, 8 * W // D, wa_bf, wb_bf, wm_bf, wo_bf, lg1, lb1, alpha,
                    DB, name="merge_sample")
        wg_bf, wu_bf, wd_bf = wg.astype(BF16), wu.astype(BF16), wd.astype(BF16)
        x_s = ffn(x_s, rw, wg_bf, wu_bf, wd_bf, lg2, lb2, alpha, DB, _ff_tile(wg.shape[2]), routed,
                  name="ffn_sample")
        if routed:
            x_p = moe_sparse(x_p_mid, rw, wg_bf, wu_bf, wd_bf, lg2, lb2, alpha, N_EXPERTS)
        else:
            x_p = ffn(x_p_mid, rw, wg_bf, wu_bf, wd_bf, lg2, lb2, alpha,
                      512, _ff_tile(wg.shape[2]), False, name="ffn_prompt")
        outs['aks'].append(sa_k.reshape(DB, TS, H_A, D_A))
        outs['avs'].append(sa_v.reshape(DB, TS, H_A, D_A))
        outs['iks'].append(s_ik.reshape(DB, TS, D_IDX))
        outs['bks'].append(sb_k.reshape(DB, TS, H_B, 2, D_B))
        outs['bvs'].append(sb_v.reshape(DB, TS, H_B, 2 * D_B))

    st = {n: jnp.stack(v) for n, v in outs.items()}
    return (x_p.reshape(B, T, D), x_s.reshape(DB, TS, D),
            st['akp'], st['avp'], st['ikp'], st['bkp'], st['bvp'], st['mkp'], st['mvp'],
            st['aks'], st['avs'], st['iks'], st['bks'], st['bvs'])
```

```python
import functools
import math

import jax
import jax.numpy as jnp
import numpy as np
from jax import lax
from jax.experimental import pallas as pl
from jax.experimental.pallas import tpu as pltpu
from jax.experimental.pallas import tpu_sc as plsc

F32 = jnp.float32
BF16 = jnp.bfloat16
HIGHEST = lax.Precision.HIGHEST

H_A, D_A = 8, 64
H_IDX, D_IDX = 8, 64
TOPK_MAX = 256
H_B, D_B = 4, 64
H_M, D_M = 4, 128
NUM_BUCKETS, MAX_DISTANCE = 32, 128
N_EXPERTS = 8
LN_EPS = 1e-5
SUBLN_EPS = 1e-5
A_WIDTH = H_A * D_A
B_WIDTH = H_B * 2 * D_B
M_WIDTH = H_M * D_M
HEAD_COLS = 512
PAGE = 128
N_GROUPS = 8

NEG = -1e30
KEY_NEG_INF = -2139095041
INT_MIN = -2147483648

VMEM_LIMIT = 56 * 1024 * 1024

TQ = 256
IDX_SPLIT = 3 * D_IDX


def _cparams(sem):
    return pltpu.CompilerParams(dimension_semantics=sem, vmem_limit_bytes=VMEM_LIMIT)


def _dot(a, b, precise=False):
    if precise:
        return jnp.dot(a, b, preferred_element_type=F32, precision=HIGHEST)
    return jnp.dot(a, b, preferred_element_type=F32)


def _dot_nt(a, b):
    return lax.dot_general(a, b, (((1,), (1,)), ((), ())), preferred_element_type=F32)


def _orderable(s):
    b = lax.bitcast_convert_type(s, jnp.int32)
    return jnp.where(b < 0, b ^ jnp.int32(0x7FFFFFFF), b)


def _layer_norm(v, g, b):
    mu = jnp.mean(v, axis=-1, keepdims=True)
    d = v - mu
    var = jnp.mean(d * d, axis=-1, keepdims=True)
    return d * lax.rsqrt(var + LN_EPS) * g + b


def _kth_largest(count_ge, shape, topk):
    t0 = jnp.full(shape, INT_MIN, jnp.int32)
    t0 = jnp.where(count_ge(jnp.zeros(shape, jnp.int32)) >= topk, jnp.zeros(shape, jnp.int32), t0)

    def bit_body(b, t):
        cand = t | jnp.left_shift(jnp.int32(1), 30 - b)
        return jnp.where(count_ge(cand) >= topk, cand, t)

    return lax.fori_loop(0, 31, bit_body, t0)


def _lambda(lv, lam_init):
    return (jnp.exp(jnp.sum(lv[0:1] * lv[1:2], axis=-1, keepdims=True))
            - jnp.exp(jnp.sum(lv[2:3] * lv[3:4], axis=-1, keepdims=True)) + lam_init)


def _mm_kernel(x_ref, w_ref, o_ref):
    o_ref[...] = _dot(x_ref[...].astype(BF16), w_ref[...])


def matmul(x, w, tm, tn, name="matmul"):
    M, K = x.shape
    N = w.shape[1]
    assert M % tm == 0 and N % tn == 0
    return pl.pallas_call(
        _mm_kernel,
        grid=(N // tn, M // tm),
        in_specs=[pl.BlockSpec((tm, K), lambda j, i: (i, 0)),
                  pl.BlockSpec((K, tn), lambda j, i: (0, j))],
        out_specs=pl.BlockSpec((tm, tn), lambda j, i: (i, j)),
        out_shape=jax.ShapeDtypeStruct((M, N), F32),
        compiler_params=_cparams(("arbitrary", "arbitrary")),
        name=name,
    )(x, w)


def _set_row(full, g, row):
    r = lax.broadcasted_iota(jnp.int32, full.shape, 0)
    return jnp.where(r == g, row, full)


def _attend_tile(kj, band_d, ml, *, n_groups, dv, qT_ref, k_ref, vT_ref, band_ref, am, acc_ref,
                 s_ref, p_ref, tq):
    m_all, l_all = ml
    ksl = pl.ds(pl.multiple_of(kj * tq, tq), tq)
    gpb = n_groups // band_ref.shape[0]
    gpv = (n_groups * dv) // vT_ref.shape[2]
    dq = qT_ref.shape[1] // n_groups
    m_out, l_out = m_all, l_all
    for g in range(n_groups):
        s_ref[g] = _dot(k_ref[0, g, ksl, :], qT_ref[0, g * dq:(g + 1) * dq, :])
    alphas = []
    for g in range(n_groups):
        s = s_ref[g]
        if am is not None:
            s = s + am
        if band_d is not None:
            s = s + band_ref[g // gpb, band_d]
        m_old = m_all[g:g + 1, :]
        m_new = jnp.maximum(m_old, jnp.max(s, axis=0, keepdims=True))
        alpha = jnp.exp(m_old - m_new)
        p = jnp.exp(s - m_new)
        p_ref[g] = p.astype(BF16)
        alphas.append(alpha)
        m_out = _set_row(m_out, g, m_new)
        l_out = _set_row(l_out, g, alpha * l_all[g:g + 1, :] + jnp.sum(p, axis=0, keepdims=True))
    for g in range(n_groups):
        vrows = slice((g // gpv) * dv, (g // gpv + 1) * dv)
        arows = slice(g * dv, (g + 1) * dv)
        acc_ref[arows, :] = alphas[g] * acc_ref[arows, :] + _dot(vT_ref[0, kj, vrows, :], p_ref[g])
    return m_out, l_out


def _attend_causal(i, tile_fn, n_groups, tq):
    ml = (jnp.full((n_groups, tq), NEG, F32), jnp.zeros((n_groups, tq), F32))
    ml = lax.fori_loop(0, jnp.maximum(i - 1, 0), lambda kj, c: tile_fn(kj, None, c), ml)
    ml = lax.cond(i >= 1, lambda c: tile_fn(i - 1, 1, c), lambda c: c, ml)
    return tile_fn(i, 0, ml)


def _attn_a_kernel(qT_ref, k_ref, vT_ref, iqT_ref, ik_ref, wT_ref, band_ref, low_ref,
                   o_ref, key_ref, am_ref, acc_ref, s_ref, p_ref, *, topk):
    i = pl.program_id(1)
    nk = i + 1
    tq = qT_ref.shape[2]
    qpos = i * tq + lax.broadcasted_iota(jnp.int32, (tq, tq), 1)
    krow = lax.broadcasted_iota(jnp.int32, (tq, tq), 0)

    def score_body(kj, c):
        ikt = ik_ref[0, pl.ds(pl.multiple_of(kj * tq, tq), tq), :]
        s = jnp.zeros((tq, tq), F32)
        for h in range(H_IDX):
            s = s + wT_ref[0, h:h + 1, :] * jnp.maximum(_dot(ikt, iqT_ref[0, h]), 0.0)
        s = jnp.where(kj * tq + krow <= qpos, s, -jnp.inf)
        key_ref[kj] = _orderable(s)
        return c

    lax.fori_loop(0, nk, score_body, 0)

    def count(pred):
        def body(kj, c):
            t = jnp.where(pred(key_ref[kj]), 1.0, 0.0)
            return c + jnp.sum(t.reshape(tq // 8, 8, tq), axis=0)
        c = lax.fori_loop(0, nk, body, jnp.zeros((8, tq), F32))
        return jnp.sum(c, axis=0, keepdims=True)

    thr = _kth_largest(lambda cand: count(lambda s: s >= cand), (1, tq), topk)
    c_gt = count(lambda s: s > thr)
    need = jnp.where(thr > KEY_NEG_INF, topk - c_gt, 0.0)

    def mask_body(kj, run):
        s = key_ref[kj]
        tie = s == thr
        tie_f = jnp.where(tie, 1.0, 0.0)
        pre = _dot(low_ref[...], tie_f.astype(BF16)) + run
        keep_tie = jnp.where(pre < need, 0.0, NEG)
        am_ref[kj] = jnp.where(s > thr, 0.0, jnp.where(tie, keep_tie, NEG))
        return run + jnp.sum(tie_f, axis=0, keepdims=True)

    lax.fori_loop(0, nk, mask_body, jnp.zeros((1, tq), F32))

    acc_ref[...] = jnp.zeros_like(acc_ref)

    def tile_fn(kj, band_d, ml):
        return _attend_tile(kj, band_d, ml, n_groups=H_A, dv=D_A, qT_ref=qT_ref, k_ref=k_ref,
                            vT_ref=vT_ref, band_ref=band_ref, am=am_ref[kj], acc_ref=acc_ref,
                            s_ref=s_ref, p_ref=p_ref, tq=tq)

    _, l_all = _attend_causal(i, tile_fn, H_A, tq)
    inv = 1.0 / l_all
    oT = jnp.concatenate([acc_ref[h * D_A:(h + 1) * D_A, :] * inv[h:h + 1, :] for h in range(H_A)],
                         axis=0)
    o_ref[0] = oT.T


def attn_a_prompt(qT, k, vT, iqT, ik, wT, band, low, topk):
    B, H, T, D = k.shape
    nq = T // TQ
    W = H * D
    return pl.pallas_call(
        functools.partial(_attn_a_kernel, topk=float(topk)),
        grid=(B, nq),
        in_specs=[
            pl.BlockSpec((1, W, TQ), lambda b, i: (b, 0, i)),
            pl.BlockSpec((1, H, T, D), lambda b, i: (b, 0, 0, 0)),
            pl.BlockSpec((1, nq, W, TQ), lambda b, i: (b, 0, 0, 0)),
            pl.BlockSpec((1, H_IDX, IDX_SPLIT, TQ), lambda b, i: (b, 0, 0, i)),
            pl.BlockSpec((1, T, IDX_SPLIT), lambda b, i: (b, 0, 0)),
            pl.BlockSpec((1, H_IDX, TQ), lambda b, i: (b, 0, i)),
            pl.BlockSpec((H, 2, TQ, TQ), lambda b, i: (0, 0, 0, 0)),
            pl.BlockSpec((TQ, TQ), lambda b, i: (0, 0)),
        ],
        out_specs=pl.BlockSpec((1, TQ, W), lambda b, i: (b, i, 0)),
        out_shape=jax.ShapeDtypeStruct((B, T, W), F32),
        scratch_shapes=[pltpu.VMEM((nq, TQ, TQ), jnp.int32),
                        pltpu.VMEM((nq, TQ, TQ), F32),
                        pltpu.VMEM((W, TQ), F32),
                        pltpu.VMEM((H, TQ, TQ), F32), pltpu.VMEM((H, TQ, TQ), BF16)],
        compiler_params=_cparams(("arbitrary", "arbitrary")),
        name="attn_a_prompt",
    )(qT, k, vT, iqT, ik, wT, band, low)


def _attn_b_kernel(qT_ref, k_ref, vT_ref, band_ref, lam_ref, g_ref, o_ref,
                   acc_ref, s_ref, p_ref, *, lam_init):
    i = pl.program_id(1)
    tq = qT_ref.shape[2]
    dv = 2 * D_B
    acc_ref[...] = jnp.zeros_like(acc_ref)

    def tile_fn(kj, band_d, ml):
        return _attend_tile(kj, band_d, ml, n_groups=2 * H_B, dv=dv, qT_ref=qT_ref, k_ref=k_ref,
                            vT_ref=vT_ref, band_ref=band_ref, am=None, acc_ref=acc_ref,
                            s_ref=s_ref, p_ref=p_ref, tq=tq)

    _, l_all = _attend_causal(i, tile_fn, 2 * H_B, tq)
    lam = _lambda(lam_ref[...], lam_init)
    inv = 1.0 / l_all
    parts = []
    for h in range(H_B):
        g0, g1 = 2 * h, 2 * h + 1
        o = (acc_ref[g0 * dv:(g0 + 1) * dv, :] * inv[g0:g0 + 1, :]
             - lam * (acc_ref[g1 * dv:(g1 + 1) * dv, :] * inv[g1:g1 + 1, :]))
        o = o * lax.rsqrt(jnp.mean(o * o, axis=0, keepdims=True) + SUBLN_EPS)
        parts.append(o * g_ref[...] * (1.0 - lam_init))
    o_ref[0] = jnp.concatenate(parts, axis=0).T


def attn_b_prompt(qT, k, vT, band, lamv, gcol, lam_init):
    B, G, T, D = k.shape
    nq = T // TQ
    W = G * D
    return pl.pallas_call(
        functools.partial(_attn_b_kernel, lam_init=lam_init),
        grid=(B, nq),
        in_specs=[
            pl.BlockSpec((1, W, TQ), lambda b, i: (b, 0, i)),
            pl.BlockSpec((1, G, T, D), lambda b, i: (b, 0, 0, 0)),
            pl.BlockSpec((1, nq, W, TQ), lambda b, i: (b, 0, 0, 0)),
            pl.BlockSpec((H_B, 2, TQ, TQ), lambda b, i: (0, 0, 0, 0)),
            pl.BlockSpec((4, D_B), lambda b, i: (0, 0)),
            pl.BlockSpec((2 * D_B, 1), lambda b, i: (0, 0)),
        ],
        out_specs=pl.BlockSpec((1, TQ, W), lambda b, i: (b, i, 0)),
        out_shape=jax.ShapeDtypeStruct((B, T, W), F32),
        scratch_shapes=[pltpu.VMEM((G * 2 * D_B, TQ), F32),
                        pltpu.VMEM((G, TQ, TQ), F32), pltpu.VMEM((G, TQ, TQ), BF16)],
        compiler_params=_cparams(("arbitrary", "arbitrary")),
        name="attn_b_prompt",
    )(qT, k, vT, band, lamv, gcol)


def _attn_m_kernel(q_ref, k_ref, v_ref, o_ref):
    for h in range(H_M):
        s = _dot_nt(q_ref[0, h], k_ref[0, h])
        m = jnp.max(s, axis=-1, keepdims=True)
        p = jnp.exp(s - m)
        l = jnp.sum(p, axis=-1, keepdims=True)
        o_ref[0, :, h * D_M:(h + 1) * D_M] = _dot(p.astype(BF16), v_ref[0, h]) / l


def attn_m_prompt(q, k, v):
    B, H, T, D = q.shape
    NM = k.shape[2]
    tq = 512
    return pl.pallas_call(
        _attn_m_kernel,
        grid=(B, T // tq),
        in_specs=[
            pl.BlockSpec((1, H, tq, D), lambda b, i: (b, 0, i, 0)),
            pl.BlockSpec((1, H, NM, D), lambda b, i: (b, 0, 0, 0)),
            pl.BlockSpec((1, H, NM, D), lambda b, i: (b, 0, 0, 0)),
        ],
        out_specs=pl.BlockSpec((1, tq, M_WIDTH), lambda b, i: (b, i, 0)),
        out_shape=jax.ShapeDtypeStruct((B, T, M_WIDTH), F32),
        compiler_params=_cparams(("arbitrary", "arbitrary")),
        name="attn_m_prompt",
    )(q, k, v)


def _merge_kernel(x_ref, oa_ref, ob_ref, om_ref, g0_ref, g1_ref, g2_ref,
                  wa_ref, wb_ref, wm_ref, wo_ref, lg_ref, lb_ref, o_ref, *, alpha):
    ya = _dot(oa_ref[...].astype(BF16), wa_ref[...])
    yb = _dot(ob_ref[...].astype(BF16), wb_ref[...])
    ym = _dot(om_ref[...].astype(BF16), wm_ref[...])
    hmix = (jax.nn.sigmoid(g0_ref[...]) * ya + jax.nn.sigmoid(g1_ref[...]) * yb
            + jax.nn.sigmoid(g2_ref[...]) * ym)
    mix = _dot(hmix.astype(BF16), wo_ref[...])
    o_ref[...] = _layer_norm(alpha * x_ref[...] + mix, lg_ref[...], lb_ref[...])


def merge(x, oa, ob, om, hproj, gate_col0, wa, wb, wm, wo, lg, lb, alpha, tm, name="merge"):
    M, D = x.shape
    row = lambda i: (i, 0)
    const = lambda i: (0, 0)
    gspec = [pl.BlockSpec((tm, D), (lambda i, c=c: (i, gate_col0 + c))) for c in range(3)]
    return pl.pallas_call(
        functools.partial(_merge_kernel, alpha=alpha),
        grid=(M // tm,),
        in_specs=[pl.BlockSpec((tm, D), row),
                  pl.BlockSpec((tm, HEAD_COLS), row), pl.BlockSpec((tm, HEAD_COLS), row),
                  pl.BlockSpec((tm, HEAD_COLS), row)] + gspec + [
                  pl.BlockSpec((HEAD_COLS, D), const), pl.BlockSpec((HEAD_COLS, D), const),
                  pl.BlockSpec((HEAD_COLS, D), const), pl.BlockSpec((D, D), const),
                  pl.BlockSpec((1, D), const), pl.BlockSpec((1, D), const)],
        out_specs=pl.BlockSpec((tm, D), row),
        out_shape=jax.ShapeDtypeStruct((M, D), F32),
        compiler_params=_cparams(("arbitrary",)),
        name=name,
    )(x, oa, ob, om, hproj, hproj, hproj, wa, wb, wm, wo, lg, lb)


def _ffn_kernel(x_ref, rw_ref, wg_ref, wu_ref, wd_ref, lg_ref, lb_ref, o_ref,
                acc_ref, gate_ref, *, alpha, routed, n_experts):
    e = pl.program_id(1)
    f = pl.program_id(2)
    first = jnp.logical_and(e == 0, f == 0)
    last = jnp.logical_and(e == pl.num_programs(1) - 1, f == pl.num_programs(2) - 1)
    x = x_ref[...]

    @pl.when(first)
    def _():
        acc_ref[...] = jnp.zeros_like(acc_ref)
        if routed:
            logits = _dot(x.astype(BF16), rw_ref[...])
            i1, i2, w1, w2 = _top2(logits, n_experts)
            lane = lax.broadcasted_iota(jnp.int32, logits.shape, 1)
            gate_ref[...] = jnp.where(lane == i1, w1, 0.0) + jnp.where(lane == i2, w2, 0.0)

    xin = x.astype(BF16)
    hg = _dot(xin, wg_ref[0])
    hu = _dot(xin, wu_ref[0])
    hh = hg * jax.nn.sigmoid(hg) * hu
    part = _dot(hh.astype(BF16), wd_ref[0])
    if routed:
        gate = gate_ref[...]
        lane = lax.broadcasted_iota(jnp.int32, gate.shape, 1)
        part = part * jnp.sum(jnp.where(lane == e, gate, 0.0), axis=-1, keepdims=True)
    acc_ref[...] += part

    @pl.when(last)
    def _():
        o_ref[...] = _layer_norm(alpha * x + acc_ref[...], lg_ref[...], lb_ref[...])


def ffn(x, rw, wg, wu, wd, lg, lb, alpha, tm, tf, routed, name="ffn"):
    M, D = x.shape
    E, _, F = wg.shape
    assert M % tm == 0 and F % tf == 0
    return pl.pallas_call(
        functools.partial(_ffn_kernel, alpha=alpha, routed=routed, n_experts=E),
        grid=(M // tm, E, F // tf),
        in_specs=[pl.BlockSpec((tm, D), lambda i, e, f: (i, 0)),
                  pl.BlockSpec(rw.shape, lambda i, e, f: (0, 0)),
                  pl.BlockSpec((1, D, tf), lambda i, e, f: (e, 0, f)),
                  pl.BlockSpec((1, D, tf), lambda i, e, f: (e, 0, f)),
                  pl.BlockSpec((1, tf, D), lambda i, e, f: (e, f, 0)),
                  pl.BlockSpec((1, D), lambda i, e, f: (0, 0)),
                  pl.BlockSpec((1, D), lambda i, e, f: (0, 0))],
        out_specs=pl.BlockSpec((tm, D), lambda i, e, f: (i, 0)),
        out_shape=jax.ShapeDtypeStruct((M, D), F32),
        scratch_shapes=[pltpu.VMEM((tm, D), F32), pltpu.VMEM((tm, rw.shape[1]), F32)],
        compiler_params=_cparams(("arbitrary", "arbitrary", "arbitrary")),
        name=name,
    )(x, rw, wg, wu, wd, lg, lb)


MOE_ROW_TILE = 512
FF_TILE_MAX = 1408


def _ff_tile(d_ff):
    return max(t for t in range(128, min(d_ff, FF_TILE_MAX) + 1, 128) if d_ff % t == 0)
SC_WINDOW = 128
SC_ROW = 128


def _top2(logits, n_experts):
    ne = logits.shape[-1]
    lane = lax.broadcasted_iota(jnp.int32, logits.shape, 1)
    logits = jnp.where(lane < n_experts, logits, -jnp.inf)
    m1 = jnp.max(logits, axis=-1, keepdims=True)
    i1 = jnp.min(jnp.where(logits == m1, lane, ne), axis=-1, keepdims=True)
    rest = jnp.where(lane == i1, -jnp.inf, logits)
    m2 = jnp.max(rest, axis=-1, keepdims=True)
    i2 = jnp.min(jnp.where(rest == m2, lane, ne), axis=-1, keepdims=True)
    e2 = jnp.exp(m2 - m1)
    return i1, i2, 1.0 / (1.0 + e2), e2 / (1.0 + e2)


def _route_kernel(x_ref, rw_ref, low_ref, e_ref, r_ref, w_ref, cnt_ref, run_ref, *, n_experts):
    @pl.when(pl.program_id(0) == 0)
    def _():
        run_ref[...] = jnp.zeros_like(run_ref)

    logits = _dot(x_ref[...].astype(BF16), rw_ref[...])
    i1, i2, w1, w2 = _top2(logits, n_experts)
    lane = lax.broadcasted_iota(jnp.int32, logits.shape, 1)
    oh1 = jnp.where(lane == i1, 1.0, 0.0)
    oh2 = jnp.where(lane == i2, 1.0, 0.0)
    oh = oh1 + oh2
    pre = _dot(low_ref[...], oh.astype(BF16)) + run_ref[...]
    r1 = jnp.sum(oh1 * pre, axis=-1, keepdims=True)
    r2 = jnp.sum(oh2 * pre, axis=-1, keepdims=True)
    run_ref[...] = run_ref[...] + jnp.sum(oh, axis=0, keepdims=True)
    first = lax.broadcasted_iota(jnp.int32, e_ref.shape, 1) == 0
    e_ref[...] = jnp.where(first, i1, i2)
    r_ref[...] = jnp.where(first, r1, r2).astype(jnp.int32)
    w_ref[...] = jnp.where(first, w1, w2)
    cnt_ref[...] = run_ref[...]


def moe_route(x, rw, n_experts, tm=512):
    M, D = x.shape
    low = jnp.asarray(np.tril(np.ones((tm, tm), np.float32), -1), BF16)
    pair = pl.BlockSpec((tm, 2), lambda i: (i, 0))
    return pl.pallas_call(
        functools.partial(_route_kernel, n_experts=n_experts),
        grid=(M // tm,),
        in_specs=[pl.BlockSpec((tm, D), lambda i: (i, 0)),
                  pl.BlockSpec(rw.shape, lambda i: (0, 0)),
                  pl.BlockSpec((tm, tm), lambda i: (0, 0))],
        out_specs=[pair, pair, pair, pl.BlockSpec((1, 128), lambda i: (0, 0))],
        out_shape=[jax.ShapeDtypeStruct((M, 2), jnp.int32), jax.ShapeDtypeStruct((M, 2), jnp.int32),
                   jax.ShapeDtypeStruct((M, 2), F32), jax.ShapeDtypeStruct((1, 128), F32)],
        scratch_shapes=[pltpu.VMEM((1, 128), F32)],
        compiler_params=_cparams(("arbitrary",)),
        name="moe_route",
    )(x, rw, low)


def _gmm_kernel(te_ref, nu_ref, x_ref, wg_ref, wu_ref, wd_ref, o_ref, acc_ref):
    i = pl.program_id(0)
    f = pl.program_id(1)

    @pl.when(i < nu_ref[0])
    def _():
        @pl.when(f == 0)
        def _():
            acc_ref[...] = jnp.zeros_like(acc_ref)

        x = x_ref[...].astype(BF16)
        hg = _dot(x, wg_ref[0])
        hu = _dot(x, wu_ref[0])
        hh = hg * jax.nn.sigmoid(hg) * hu
        acc_ref[...] += _dot(hh.astype(BF16), wd_ref[0])

        @pl.when(f == pl.num_programs(1) - 1)
        def _():
            o_ref[...] = acc_ref[...]


def moe_gmm(tile_expert, n_used, xs, wg, wu, wd, tf):
    R, D = xs.shape
    F = wg.shape[2]
    tr = MOE_ROW_TILE
    grid_spec = pltpu.PrefetchScalarGridSpec(
        num_scalar_prefetch=2,
        grid=(R // tr, F // tf),
        in_specs=[pl.BlockSpec((tr, D), lambda i, f, te, nu: (i, 0)),
                  pl.BlockSpec((1, D, tf), lambda i, f, te, nu: (te[i], 0, f)),
                  pl.BlockSpec((1, D, tf), lambda i, f, te, nu: (te[i], 0, f)),
                  pl.BlockSpec((1, tf, D), lambda i, f, te, nu: (te[i], f, 0))],
        out_specs=pl.BlockSpec((tr, D), lambda i, f, te, nu: (i, 0)),
        scratch_shapes=[pltpu.VMEM((tr, D), F32)],
    )
    return pl.pallas_call(
        _gmm_kernel,
        grid_spec=grid_spec,
        out_shape=jax.ShapeDtypeStruct((R, D), F32),
        compiler_params=_cparams(("arbitrary", "arbitrary")),
        name="moe_gmm",
    )(tile_expert, n_used, xs, wg, wu, wd)


def _combine_kernel(x_ref, o1_ref, o2_ref, w_ref, lg_ref, lb_ref, o_ref, *, alpha):
    w = w_ref[...]
    y = w[:, 0:1] * o1_ref[...] + w[:, 1:2] * o2_ref[...]
    o_ref[...] = _layer_norm(alpha * x_ref[...] + y, lg_ref[...], lb_ref[...])


def moe_combine(x, og, w12, lg, lb, alpha, tm=512):
    M, D = x.shape
    nb = M // tm
    return pl.pallas_call(
        functools.partial(_combine_kernel, alpha=alpha),
        grid=(nb,),
        in_specs=[pl.BlockSpec((tm, D), lambda i: (i, 0)),
                  pl.BlockSpec((tm, D), lambda i: (i, 0)),
                  pl.BlockSpec((tm, D), lambda i: (i + nb, 0)),
                  pl.BlockSpec((tm, 2), lambda i: (i, 0)),
                  pl.BlockSpec((1, D), lambda i: (0, 0)),
                  pl.BlockSpec((1, D), lambda i: (0, 0))],
        out_specs=pl.BlockSpec((tm, D), lambda i: (i, 0)),
        out_shape=jax.ShapeDtypeStruct((M, D), F32),
        compiler_params=_cparams(("arbitrary",)),
        name="moe_combine",
    )(x, og, og, w12, lg, lb)


def _sc_mesh():
    return plsc.VectorSubcoreMesh(core_axis_name="c", subcore_axis_name="s")


def sc_scatter_rows(src, idx, n_out_rows):
    S = src.shape[0]
    M = idx.shape[0]
    nb = S // SC_WINDOW
    assert S % SC_WINDOW == 0 and M % S == 0

    @pl.kernel(out_type=jax.ShapeDtypeStruct((n_out_rows, SC_ROW), src.dtype), mesh=_sc_mesh(),
               scratch_types=[])
    def scatter_kernel(x_hbm, i_hbm, o_hbm):
        def body(x_vmem, i_vmem):
            pltpu.sync_copy(x_vmem, o_hbm.at[i_vmem.at[0]])

        pltpu.emit_pipeline(
            body,
            grid=(M // SC_WINDOW,),
            in_specs=[pl.BlockSpec((SC_WINDOW, SC_ROW), lambda i: (lax.rem(i, nb), 0)),
                      pl.BlockSpec((1, SC_WINDOW), lambda i: (0, i))],
            out_specs=[],
            core_axis_name=("c", "s"),
            dimension_semantics=(pltpu.PARALLEL,),
        )(x_hbm, i_hbm)

    return scatter_kernel(src, idx.reshape(1, M))


def sc_gather_rows(table, idx):
    M = idx.shape[0]
    assert M % SC_WINDOW == 0

    @pl.kernel(out_type=jax.ShapeDtypeStruct((M, SC_ROW), table.dtype), mesh=_sc_mesh(),
               scratch_types=[])
    def gather_kernel(t_hbm, i_hbm, o_hbm):
        def body(i_vmem, o_vmem):
            pltpu.sync_copy(t_hbm.at[i_vmem.at[0]], o_vmem)

        pltpu.emit_pipeline(
            body,
            grid=(M // SC_WINDOW,),
            in_specs=[pl.BlockSpec((1, SC_WINDOW), lambda i: (0, i))],
            out_specs=[pl.BlockSpec((SC_WINDOW, SC_ROW), lambda i: (i, 0))],
            core_axis_name=("c", "s"),
            dimension_semantics=(pltpu.PARALLEL,),
        )(i_hbm, o_hbm)

    return gather_kernel(table, idx.reshape(1, M))


def moe_dispatch(x, rw, n_experts):
    N, D = x.shape
    tr = MOE_ROW_TILE
    e12, r12, w12, counts = moe_route(x, rw, n_experts)
    cnt = counts[0, :n_experts].astype(jnp.int32)
    cnt_pad = (cnt + tr - 1) // tr * tr
    off_end = jnp.cumsum(cnt_pad)
    off = off_end - cnt_pad
    R = 2 * N + n_experts * tr
    eids = jnp.arange(n_experts, dtype=jnp.int32)
    pos = jnp.sum(jnp.where(e12[..., None] == eids, off, 0), axis=-1) + r12
    tile_start = jnp.arange(R // tr, dtype=jnp.int32) * tr
    tile_expert = jnp.minimum(jnp.sum(tile_start[:, None] >= off_end[None, :], axis=1),
                              n_experts - 1).astype(jnp.int32)
    n_used = (off_end[-1:] // tr).astype(jnp.int32)
    pos_flat = pos.T.reshape(2 * N)
    idx_s = _chunk_index(pos_flat, D)
    xs = _from_chunks(sc_scatter_rows(_as_chunks(x), idx_s, R * (D // SC_ROW)), R, D)
    return xs, tile_expert, n_used, idx_s, w12


def moe_gather(o_sorted, idx_s, n_tokens):
    return _from_chunks(sc_gather_rows(_as_chunks(o_sorted), idx_s), 2 * n_tokens, o_sorted.shape[1])


def _as_chunks(a):
    rows, d = a.shape
    return a.reshape(rows // 8, 8, d // SC_ROW, SC_ROW).transpose(0, 2, 1, 3).reshape(-1, SC_ROW)


def _from_chunks(c, rows, d):
    return c.reshape(rows // 8, d // SC_ROW, 8, SC_ROW).transpose(0, 2, 1, 3).reshape(rows, d)


def _chunk_index(row_of, d):
    nk = d // SC_ROW
    r = row_of.reshape(-1, 1, 8)
    k = jnp.arange(nk, dtype=jnp.int32).reshape(1, nk, 1)
    return ((r // 8) * (8 * nk) + k * 8 + r % 8).reshape(-1)


IDX_PAGES_PER_STEP = 32
ATTN_PAGES_PER_STEP = 16


def _page_specs(layer, n, rows, P):
    return [pl.BlockSpec((1, 1, rows, PAGE), (lambda b, s, pt, j=j: (layer, pt[b, s * P + j], 0, 0)))
            for j in range(n)]


def _idx_score_kernel(pt_ref, iqT_ref, iq3_ref, iw_ref, iwc_ref, iknew_ref, *refs, P):
    pages = refs[:P]
    key_ref, knew_ref = refs[P:]
    step = pl.program_id(1)
    wic = iwc_ref[0] * (H_IDX ** -0.5)
    iq3 = iq3_ref[0]
    for j in range(P):
        kt = pages[j][0, 0]
        hi = kt.astype(BF16)
        lo = (kt - hi.astype(F32)).astype(BF16)
        d = _dot(iq3, jnp.concatenate([hi, lo, hi], axis=0))
        sc = jnp.sum(wic * jnp.maximum(d, 0.0), axis=0, keepdims=True)
        key_ref[0, pl.ds(step * P + j, 1), :] = _orderable(sc + 0.0)

    @pl.when(step == pl.num_programs(1) - 1)
    def _():
        iqT = iqT_ref[0] * (D_IDX ** -0.5)
        wi = iw_ref[0] * (H_IDX ** -0.5)
        dn = jnp.sum(iqT * iknew_ref[0], axis=0, keepdims=True)
        sn = jnp.sum(wi * jnp.maximum(dn, 0.0), axis=-1, keepdims=True) + 0.0
        knew_ref[0] = jnp.broadcast_to(_orderable(sn), knew_ref.shape[1:])


def _idx_select_kernel(key_ref, knew_ref, tri_ref, low_ref, am_ref, amnew_ref, *, topk, n_samples):
    s = key_ref[...]
    npg = s.shape[0] // n_samples
    kn = knew_ref[...][:, 0:1]

    def per_sample(x):
        return jnp.sum(jnp.sum(x.reshape(n_samples, npg, x.shape[-1]), axis=1), axis=-1, keepdims=True)

    def count_ge(cand):
        t = jnp.where(s >= _expand_rows(cand, npg), 1.0, 0.0)
        return per_sample(t) + jnp.where(kn >= cand, 1.0, 0.0)

    thr = _kth_largest(count_ge, (n_samples, 1), topk)
    thr_r = _expand_rows(thr, npg)
    c_gt = per_sample(jnp.where(s > thr_r, 1.0, 0.0)) + jnp.where(kn > thr, 1.0, 0.0)
    need = topk - c_gt
    tie = s == thr_r
    tie_f = jnp.where(tie, 1.0, 0.0)
    in_row = _dot(tie_f.astype(BF16), tri_ref[...])
    row_tot = jnp.broadcast_to(jnp.sum(tie_f, axis=-1, keepdims=True), tie_f.shape)
    before = jnp.einsum('bpq,bql->bpl', low_ref[...],
                        row_tot.astype(BF16).reshape(n_samples, npg, PAGE),
                        preferred_element_type=F32).reshape(s.shape)
    keep_tie = jnp.where(in_row + before < _expand_rows(need, npg), 0.0, NEG)
    am_ref[...] = jnp.where(s > thr_r, 0.0, jnp.where(tie, keep_tie, NEG))
    keep_new = jnp.where(per_sample(tie_f) < need, 0.0, NEG)
    amn = jnp.where(kn > thr, 0.0, jnp.where(kn == thr, keep_new, NEG))
    amnew_ref[...] = jnp.broadcast_to(amn, amnew_ref.shape)


def idx_sample(page_table, layer, iq, iw, ik_new, poolT, topk):
    DB, npg = page_table.shape
    iqT = jnp.transpose(iq, (0, 2, 1))
    iq3 = _split3(iq * (D_IDX ** -0.5), (0, 0, 1))
    iwc = iw.reshape(DB, H_IDX, 1)
    iw = iw.reshape(DB, 1, H_IDX)
    ik_new = ik_new.reshape(DB, D_IDX, 1)
    P = min(IDX_PAGES_PER_STEP, npg)
    assert npg % P == 0
    per_b = lambda b, s, pt: (b, 0, 0)
    grid_spec = pltpu.PrefetchScalarGridSpec(
        num_scalar_prefetch=1,
        grid=(DB, npg // P),
        in_specs=[pl.BlockSpec((1, D_IDX, H_IDX), per_b),
                  pl.BlockSpec((1, H_IDX, IDX_SPLIT), per_b),
                  pl.BlockSpec((1, 1, H_IDX), per_b),
                  pl.BlockSpec((1, H_IDX, 1), per_b),
                  pl.BlockSpec((1, D_IDX, 1), per_b)]
                 + _page_specs(layer, P, D_IDX, P),
        out_specs=[pl.BlockSpec((1, npg, PAGE), per_b),
                   pl.BlockSpec((1, 1, PAGE), per_b)],
    )
    keys, knew = pl.pallas_call(
        functools.partial(_idx_score_kernel, P=P),
        grid_spec=grid_spec,
        out_shape=[jax.ShapeDtypeStruct((DB, npg, PAGE), jnp.int32),
                   jax.ShapeDtypeStruct((DB, 1, PAGE), jnp.int32)],
        compiler_params=_cparams(("arbitrary", "arbitrary")),
        name="idx_score",
    )(page_table, iqT, iq3, iw, iwc, ik_new, *([poolT] * P))
    tri = jnp.asarray(np.triu(np.ones((PAGE, PAGE), np.float32), 1), BF16)
    low = jnp.asarray(np.broadcast_to(np.tril(np.ones((npg, npg), np.float32), -1), (DB, npg, npg)), BF16)
    whole = lambda shape: pl.BlockSpec(shape, lambda i: (0,) * len(shape))
    am, amn = pl.pallas_call(
        functools.partial(_idx_select_kernel, topk=float(topk), n_samples=DB),
        grid=(1,),
        in_specs=[whole((DB * npg, PAGE)), whole((DB, PAGE)), whole((PAGE, PAGE)), whole((DB, npg, npg))],
        out_specs=[whole((DB * npg, PAGE)), whole((DB, PAGE))],
        out_shape=[jax.ShapeDtypeStruct((DB * npg, PAGE), F32), jax.ShapeDtypeStruct((DB, PAGE), F32)],
        compiler_params=_cparams(("arbitrary",)),
        name="idx_select",
    )(keys.reshape(DB * npg, PAGE), knew.reshape(DB, PAGE), tri, low)
    return am.reshape(DB, npg, PAGE), amn.reshape(DB, 1, PAGE)


def _expand_rows(x, rep):
    g, n = x.shape
    return jnp.broadcast_to(x[:, None, :], (g, rep, n)).reshape(g * rep, n)


def _group_logits(kt, qb):
    w, n = kt.shape
    return jnp.sum((kt * qb).reshape(N_GROUPS, w // N_GROUPS, n), axis=1)


def _dec_a_kernel(pt_ref, q_ref, am_ref, amnew_ref, knew_ref, vnew_ref, blast_ref, bnew_ref,
                  *refs, scale, P):
    kp = refs[:P]
    vp = refs[P:2 * P]
    o_ref, qb_ref, m_ref, l_ref, acc_ref = refs[2 * P:]
    step = pl.program_id(1)
    is_last = step == pl.num_programs(1) - 1
    W = qb_ref.shape[0]
    rep = W // N_GROUPS

    @pl.when(step == 0)
    def _():
        qb_ref[...] = jnp.broadcast_to(q_ref[0] * scale, qb_ref.shape)
        m_ref[...] = jnp.full_like(m_ref, NEG)
        l_ref[...] = jnp.zeros_like(l_ref)
        acc_ref[...] = jnp.zeros_like(acc_ref)

    last_f = jnp.where(is_last, 1.0, 0.0)
    for j in range(P):
        lg = _group_logits(kp[j][0, 0], qb_ref[...]) + am_ref[0, j:j + 1, :]
        if j == P - 1:
            lg = lg + last_f * blast_ref[...]
        m_old = m_ref[...]
        m_new = jnp.maximum(m_old, lg)
        alpha = jnp.exp(m_old - m_new)
        p = jnp.exp(lg - m_new)
        m_ref[...] = m_new
        l_ref[...] = l_ref[...] * alpha + p
        acc_ref[...] = acc_ref[...] * _expand_rows(alpha, rep) + _expand_rows(p, rep) * vp[j][0, 0]

    @pl.when(is_last)
    def _():
        q = q_ref[0] * scale
        lgn = (jnp.sum((knew_ref[0] * q).reshape(N_GROUPS, rep, 1), axis=1)
               + bnew_ref[...] + amnew_ref[0][:, 0:1])
        m = m_ref[...]
        mx = jnp.maximum(jnp.max(m, axis=-1, keepdims=True), lgn)
        w = jnp.exp(m - mx)
        pn = jnp.exp(lgn - mx)
        l = jnp.sum(l_ref[...] * w, axis=-1, keepdims=True) + pn
        o = (jnp.sum(acc_ref[...] * _expand_rows(w, rep), axis=-1, keepdims=True)
             + _expand_rows(pn, rep) * vnew_ref[0])
        o_ref[0] = o / _expand_rows(l, rep)


def decode_a(page_table, layer, qcol, am, amnew, knew, vnew, blast, bnew, kT, vT, scale):
    DB, npg = page_table.shape
    W = qcol.shape[1]
    P = min(ATTN_PAGES_PER_STEP, npg)
    assert npg % P == 0
    per_b = lambda b, s, pt: (b, 0, 0)
    c2 = lambda b, s, pt: (0, 0)
    grid_spec = pltpu.PrefetchScalarGridSpec(
        num_scalar_prefetch=1,
        grid=(DB, npg // P),
        in_specs=[pl.BlockSpec((1, W, 1), per_b),
                  pl.BlockSpec((1, P, PAGE), lambda b, s, pt: (b, s, 0)),
                  pl.BlockSpec((1, 1, PAGE), per_b),
                  pl.BlockSpec((1, W, 1), per_b),
                  pl.BlockSpec((1, W, 1), per_b),
                  pl.BlockSpec((N_GROUPS, PAGE), c2),
                  pl.BlockSpec((N_GROUPS, 1), c2)]
                 + _page_specs(layer, P, W, P) + _page_specs(layer, P, W, P),
        out_specs=pl.BlockSpec((1, W, 1), per_b),
        scratch_shapes=[pltpu.VMEM((W, PAGE), F32), pltpu.VMEM((N_GROUPS, PAGE), F32),
                        pltpu.VMEM((N_GROUPS, PAGE), F32), pltpu.VMEM((W, PAGE), F32)],
    )
    return pl.pallas_call(
        functools.partial(_dec_a_kernel, scale=scale, P=P),
        grid_spec=grid_spec,
        out_shape=jax.ShapeDtypeStruct((DB, W, 1), F32),
        compiler_params=_cparams(("arbitrary", "arbitrary")),
        name="decode_a",
    )(page_table, qcol, am, amnew, knew, vnew, blast, bnew, *([kT] * P), *([vT] * P))


def _dec_b_kernel(pt_ref, q_ref, knew_ref, vnew_ref, blast_ref, bnew_ref, r_ref, pm_ref,
                  lam_ref, g_ref, *refs, scale, lam_init, P):
    kp = refs[:P]
    vp = refs[P:2 * P]
    o_ref, qb_ref, m_ref, l_ref, acc_ref = refs[2 * P:]
    step = pl.program_id(1)
    is_last = step == pl.num_programs(1) - 1
    W = qb_ref.shape[0]
    rep = W // N_GROUPS

    @pl.when(step == 0)
    def _():
        qb_ref[...] = jnp.broadcast_to(q_ref[0] * scale, qb_ref.shape)
        m_ref[...] = jnp.full_like(m_ref, NEG)
        l_ref[...] = jnp.zeros_like(l_ref)
        acc_ref[...] = jnp.zeros_like(acc_ref)

    last_f = jnp.where(is_last, 1.0, 0.0)
    lgs = [_group_logits(kp[j][0, 0], qb_ref[...]) for j in range(P)]
    lgs[P - 1] = lgs[P - 1] + last_f * blast_ref[...]
    m_old = m_ref[...]
    m_new = m_old
    for lg in lgs:
        m_new = jnp.maximum(m_new, jnp.max(lg, axis=-1, keepdims=True))
    alpha = jnp.exp(m_old - m_new)
    ps = [jnp.exp(lg - m_new) for lg in lgs]
    m_ref[...] = m_new
    l_ref[...] = l_ref[...] * alpha + sum(jnp.sum(p, axis=-1, keepdims=True) for p in ps)
    pr = _dot(jnp.concatenate(ps, axis=0).astype(BF16), r_ref[...])
    o = jnp.zeros(acc_ref.shape, F32)
    for j in range(P):
        prj = (pr[j * N_GROUPS:(j + 1) * N_GROUPS] * pm_ref[...]).astype(BF16)
        o = o + _dot(prj, vp[j][0, 0].astype(BF16))
    acc_ref[...] = acc_ref[...] * alpha + o

    @pl.when(is_last)
    def _():
        q = q_ref[0] * scale
        lgn = jnp.sum((knew_ref[0] * q).reshape(N_GROUPS, rep, 1), axis=1) + bnew_ref[...]
        m = m_ref[...]
        mx = jnp.maximum(m, lgn)
        a2 = jnp.exp(m - mx)
        pn = jnp.exp(lgn - mx)
        l = l_ref[...] * a2 + pn
        on = (acc_ref[...] * a2 + pn * vnew_ref[0]) / l
        lam = _lambda(lam_ref[...], lam_init)
        r = lax.broadcasted_iota(jnp.int32, (N_GROUPS, N_GROUPS), 0)
        c = lax.broadcasted_iota(jnp.int32, (N_GROUPS, N_GROUPS), 1)
        comb = jnp.where(c == 2 * r, 1.0, 0.0) - lam * jnp.where(c == 2 * r + 1, 1.0, 0.0)
        o = _dot(comb, on, True)
        o = o * lax.rsqrt(jnp.mean(o * o, axis=-1, keepdims=True) + SUBLN_EPS)
        o_ref[0] = o * g_ref[...] * (1.0 - lam_init)


def decode_b(page_table, layer, qcol, knew, vnew_g, blast, bnew, rmat, pmask, lamv, g, kT, vrows,
             scale, lam_init):
    DB, npg = page_table.shape
    W = qcol.shape[1]
    P = min(ATTN_PAGES_PER_STEP, npg)
    dv = 2 * D_B
    assert npg % P == 0
    per_b = lambda b, s, pt: (b, 0, 0)
    c2 = lambda b, s, pt: (0, 0)
    grid_spec = pltpu.PrefetchScalarGridSpec(
        num_scalar_prefetch=1,
        grid=(DB, npg // P),
        in_specs=[pl.BlockSpec((1, W, 1), per_b),
                  pl.BlockSpec((1, W, 1), per_b),
                  pl.BlockSpec((1, N_GROUPS, dv), per_b),
                  pl.BlockSpec((N_GROUPS, PAGE), c2),
                  pl.BlockSpec((N_GROUPS, 1), c2),
                  pl.BlockSpec((PAGE, W), c2),
                  pl.BlockSpec((N_GROUPS, W), c2),
                  pl.BlockSpec((4, D_B), c2),
                  pl.BlockSpec((1, dv), c2)]
                 + _page_specs(layer, P, W, P) + _page_specs(layer, P, W, P),
        out_specs=pl.BlockSpec((1, N_GROUPS, dv), per_b),
        scratch_shapes=[pltpu.VMEM((W, PAGE), F32), pltpu.VMEM((N_GROUPS, 1), F32),
                        pltpu.VMEM((N_GROUPS, 1), F32), pltpu.VMEM((N_GROUPS, dv), F32)],
    )
    return pl.pallas_call(
        functools.partial(_dec_b_kernel, scale=scale, lam_init=lam_init, P=P),
        grid_spec=grid_spec,
        out_shape=jax.ShapeDtypeStruct((DB, N_GROUPS, dv), F32),
        compiler_params=_cparams(("arbitrary", "arbitrary")),
        name="decode_b",
    )(page_table, qcol, knew, vnew_g, blast, bnew, rmat, pmask, lamv, g,
      *([kT] * P), *([vrows] * P))


def _dec_m_kernel(q_ref, k_ref, v_ref, o_ref):
    n = k_ref.shape[2] // 8
    q = q_ref[0]
    k3 = k_ref[0, 0].reshape(n, 8, D_M)
    lg = jnp.sum(k3 * q[None], axis=-1, keepdims=True)
    m8 = jnp.max(lg, axis=0)
    m4 = jnp.maximum(m8[:H_M], m8[H_M:])
    p = jnp.exp(lg - jnp.concatenate([m4, m4], axis=0)[None])
    l8 = jnp.sum(p, axis=0)
    o8 = jnp.sum(p * v_ref[0, 0].reshape(n, 8, D_M), axis=0)
    o_ref[0] = (o8[:H_M] + o8[H_M:]) / (l8[:H_M] + l8[H_M:])


def decode_m(layer, q8, mem_k, mem_v):
    DB = q8.shape[0]
    rows = mem_k.shape[2]
    return pl.pallas_call(
        _dec_m_kernel,
        grid=(DB,),
        in_specs=[pl.BlockSpec((1, 8, D_M), lambda b: (b, 0, 0)),
                  pl.BlockSpec((1, 1, rows, D_M), lambda b: (layer, b, 0, 0)),
                  pl.BlockSpec((1, 1, rows, D_M), lambda b: (layer, b, 0, 0))],
        out_specs=pl.BlockSpec((1, H_M, D_M), lambda b: (b, 0, 0)),
        out_shape=jax.ShapeDtypeStruct((DB, H_M, D_M), F32),
        compiler_params=_cparams(("arbitrary",)),
        name="decode_m",
    )(q8, mem_k, mem_v)


def _t5_bucket_np(rel):
    n = np.maximum(rel, 0)
    max_exact = NUM_BUCKETS // 2
    nf = np.maximum(n, 1).astype(np.float32)
    large = max_exact + (np.log(nf / np.float32(max_exact))
                         / np.float32(math.log(MAX_DISTANCE / max_exact))
                         * np.float32(NUM_BUCKETS - max_exact)).astype(np.int32)
    large = np.minimum(large, NUM_BUCKETS - 1)
    return np.where(n < max_exact, n, large).astype(np.int32)


def _far_bucket(first_far, last_far):
    b = _t5_bucket_np(np.arange(first_far, last_far + 1))
    assert (b == b[0]).all(), "relative-position bias must be constant beyond one tile"
    return int(b[0])


def _bias_lookup(rel_bias, buckets):
    onehot = jax.nn.one_hot(jnp.asarray(buckets, jnp.int32), NUM_BUCKETS, dtype=F32)
    return jnp.einsum('...b,bh->...h', onehot, rel_bias, precision=HIGHEST)


def _band_tables(rel_bias, heads, t_max):
    far = _far_bucket(TQ + 1, max(t_max, TQ + 1))
    r = np.arange(TQ)[None, :] - np.arange(TQ)[:, None]
    tabs = []
    for d in range(2):
        rel = r + d * TQ
        vals = _bias_lookup(rel_bias, _t5_bucket_np(rel))[..., heads] - rel_bias[far][heads]
        vals = jnp.where((rel >= 0)[..., None], vals, NEG)
        tabs.append(jnp.transpose(vals, (2, 0, 1)))
    return jnp.stack(tabs, axis=1).astype(F32)


def _split3(x, order):
    hi = x.astype(BF16)
    lo = (x - hi.astype(F32)).astype(BF16)
    return jnp.concatenate([(hi, lo)[o] for o in order], axis=-1)


def kernel(x_prompt, x_sample, cache_a_k, cache_a_v, cache_idx_k, cache_b_k, cache_b_v, cache_mem_k, cache_mem_v, page_table, mem_prompt, w_in, w_mem_kv, lambda_q1, lambda_k1, lambda_q2, lambda_k2, subln_g, w_branch_a, w_branch_b, w_branch_m, w_out, rel_bias, ln1_g, ln1_b, ln2_g, ln2_b, ffn_w_gate, ffn_w_up, ffn_w_down, router_w, expert_w_gate, expert_w_up, expert_w_down):
    B, T, D = x_prompt.shape
    DB, TS, _ = x_sample.shape
    depth = w_in.shape[0]
    n_pool = cache_a_k.shape[1]
    npg = page_table.shape[1]
    past = npg * PAGE
    n_mem = mem_prompt.shape[1]
    assert TS == 1 and T % TQ == 0 and cache_a_k.shape[2] == PAGE
    alpha = (2 * depth) ** 0.25
    topk_p = min(TOPK_MAX, T // 4)
    topk_s = min(TOPK_MAX, (past + TS) // 4)
    W = HEAD_COLS
    nq = T // TQ

    c_ik = 3 * A_WIDTH + H_IDX * D_IDX
    c_b = c_ik + D_IDX + H_IDX
    n_main = 8 * W + 3 * D
    tail_pad = 128 - D_IDX - H_IDX

    low = jnp.asarray(np.tril(np.ones((TQ, TQ), np.float32), -1), BF16)
    band_a = _band_tables(rel_bias, np.arange(H_A), T)
    band_b = _band_tables(rel_bias, H_A + np.arange(H_B), T)

    far_s = _far_bucket(PAGE + 1, past)
    b_last = (_bias_lookup(rel_bias, _t5_bucket_np(PAGE - np.arange(PAGE))) - rel_bias[far_s]).T
    b_new = (rel_bias[0] - rel_bias[far_s])[:, None]
    blast_a, bnew_a = b_last[:H_A], b_new[:H_A]
    blast_b, bnew_b = jnp.repeat(b_last[H_A:], 2, axis=0), jnp.repeat(b_new[H_A:], 2, axis=0)
    lane = np.arange(W)
    rmat = jnp.asarray(lane[None, :] // H_B == np.arange(PAGE)[:, None], BF16)
    pmask = jnp.asarray(lane[None, :] % H_B == np.arange(N_GROUPS)[:, None] // 2, F32)

    idxT = jnp.transpose(cache_idx_k, (0, 1, 3, 2))
    akT = jnp.transpose(cache_a_k, (0, 1, 3, 4, 2)).reshape(depth, n_pool, W, PAGE)
    avT = jnp.transpose(cache_a_v, (0, 1, 3, 4, 2)).reshape(depth, n_pool, W, PAGE)
    bkT = jnp.transpose(cache_b_k, (0, 1, 3, 4, 5, 2)).reshape(depth, n_pool, W, PAGE)
    bv_rows = cache_b_v.reshape(depth, n_pool, PAGE * H_B, 2 * D_B)
    memk_rows = cache_mem_k.reshape(depth, DB, n_mem * H_M, D_M)
    memv_rows = cache_mem_v.reshape(depth, DB, n_mem * H_M, D_M)

    outs = {n: [] for n in ('akp', 'avp', 'ikp', 'bkp', 'bvp', 'mkp', 'mvp',
                            'aks', 'avs', 'iks', 'bks', 'bvs')}
    x_p = x_prompt.reshape(B * T, D)
    x_s = x_sample.reshape(DB, D)
    for l in range(depth):
        lam_init = 0.8 - 0.6 * math.exp(-0.3 * l)
        lamv = jnp.stack([lambda_q1[l], lambda_k1[l], lambda_q2[l], lambda_k2[l]])
        wl = w_in[l]
        w_perm = jnp.concatenate([wl[:, :c_ik], wl[:, c_b:], wl[:, c_ik:c_b],
                                  jnp.zeros((D, tail_pad), F32)], axis=1)
        w_main_bf = w_perm[:, :n_main].astype(BF16)
        w_tail_bf = w_perm[:, n_main:].astype(BF16)
        wa_bf, wb_bf, wm_bf, wo_bf = (w[l].astype(BF16) for w in (w_branch_a, w_branch_b, w_branch_m, w_out))
        lg1, lb1 = ln1_g[l][None], ln1_b[l][None]
        lg2, lb2 = ln2_g[l][None], ln2_b[l][None]
        g_sub = subln_g[l][None]
        j = l // 2
        if l % 2 == 0:
            rw = jnp.zeros((D, 128), BF16)
            wg, wu, wd = ffn_w_gate[j][None], ffn_w_up[j][None], ffn_w_down[j][None]
            routed = False
        else:
            rw = jnp.pad(router_w[j], ((0, 0), (0, 128 - N_EXPERTS))).astype(BF16)
            wg, wu, wd = expert_w_gate[j], expert_w_up[j], expert_w_down[j]
            routed = True

        h = matmul(x_p, w_main_bf, 1024 if B * T % 1024 == 0 else 512, 1024,
                   name="proj_prompt")
        tail = matmul(x_p, w_tail_bf, 512, 128, name="proj_tail")
        h3 = h.reshape(B, T, n_main)
        col = lambda c: h3[:, :, c * W:(c + 1) * W]

        def heads(c, nh, dh, dt):
            return col(c).reshape(B, T, nh, dh).transpose(0, 2, 1, 3).astype(dt)

        def q_t(c, scale):
            return jnp.transpose(col(c) * scale, (0, 2, 1)).astype(BF16)

        def v_t(c):
            return col(c).reshape(B, nq, TQ, W).transpose(0, 1, 3, 2).astype(BF16)

        a_k, a_v, b_k, b_v = col(1), col(2), col(5), col(6)
        i_k = tail[:, :D_IDX].reshape(B, T, D_IDX)
        i_w = tail[:, D_IDX:D_IDX + H_IDX].reshape(B, T, H_IDX)
        iq = col(3).reshape(B, T, H_IDX, D_IDX) * (D_IDX ** -0.5)
        iqT = jnp.transpose(_split3(iq, (0, 0, 1)), (0, 2, 3, 1))
        ik3 = _split3(i_k, (0, 1, 0))
        wT = jnp.transpose(i_w * (H_IDX ** -0.5), (0, 2, 1))
        o_a = attn_a_prompt(q_t(0, D_A ** -0.5), heads(1, H_A, D_A, BF16), v_t(2), iqT, ik3, wT,
                            band_a, low, topk_p)
        o_b = attn_b_prompt(q_t(4, D_B ** -0.5), heads(5, 2 * H_B, D_B, BF16), v_t(6),
                            band_b, lamv, g_sub.T, lam_init)
        kv = matmul(mem_prompt.reshape(B * n_mem, D), w_mem_kv[l].astype(BF16), 512, 1024,
                    name="proj_mem")
        mk = kv[:, :M_WIDTH].reshape(B, n_mem, H_M, D_M)
        mv = kv[:, M_WIDTH:].reshape(B, n_mem, H_M, D_M)
        qm = (heads(7, H_M, D_M, F32) * (D_M ** -0.5)).astype(BF16)
        o_m = attn_m_prompt(qm, mk.transpose(0, 2, 1, 3).astype(BF16),
                            mv.transpose(0, 2, 1, 3).astype(BF16))
        x_p = merge(x_p, o_a.reshape(B * T, W), o_b.reshape(B * T, W), o_m.reshape(B * T, W),
                    h, 8 * W // D, wa_bf, wb_bf, wm_bf, wo_bf, lg1, lb1, alpha, 512, name="merge_prompt")
        x_p_mid = x_p
        if routed:
            xs, tile_expert, n_used, idx_s, w12 = moe_dispatch(x_p_mid, rw, N_EXPERTS)
        outs['akp'].append(a_k.reshape(B, T, H_A, D_A))
        outs['avp'].append(a_v.reshape(B, T, H_A, D_A))
        outs['ikp'].append(i_k)
        outs['bkp'].append(b_k.reshape(B, T, H_B, 2, D_B))
        outs['bvp'].append(b_v.reshape(B, T, H_B, 2 * D_B))
        outs['mkp'].append(mk)
        outs['mvp'].append(mv)

        hs = matmul(x_s, w_main_bf, DB, 1024, name="proj_sample")
        ts = matmul(x_s, w_tail_bf, DB, 128, name="proj_tail_sample")
        sa_q, sa_k, sa_v, s_iq = (hs[:, c * W:(c + 1) * W] for c in range(4))
        sb_q, sb_k, sb_v, sm_q = (hs[:, c * W:(c + 1) * W] for c in range(4, 8))
        s_ik = ts[:, :D_IDX]
        s_iw = ts[:, D_IDX:D_IDX + H_IDX]
        colv = lambda t: t.reshape(DB, -1, 1)
        am, amn = idx_sample(page_table, l, s_iq.reshape(DB, H_IDX, D_IDX), s_iw, s_ik, idxT, topk_s)
        o_a = decode_a(page_table, l, colv(sa_q), am, amn, colv(sa_k), colv(sa_v),
                       blast_a, bnew_a, akT, avT, D_A ** -0.5).reshape(DB, W)
        vnew_g = jnp.repeat(sb_v.reshape(DB, H_B, 2 * D_B), 2, axis=1)
        o_b = decode_b(page_table, l, colv(sb_q), colv(sb_k), vnew_g, blast_b, bnew_b, rmat, pmask,
                       lamv, g_sub, bkT, bv_rows, D_B ** -0.5, lam_init)[:, :H_B].reshape(DB, W)
        q8 = jnp.tile(sm_q.reshape(DB, H_M, D_M) * (D_M ** -0.5), (1, 2, 1))
        o_m = decode_m(l, q8, memk_rows, memv_rows).reshape(DB, W)
        wg_bf, wu_bf, wd_bf = wg.astype(BF16), wu.astype(BF16), wd.astype(BF16)
        if routed:
            o_sorted = moe_gmm(tile_expert, n_used, xs, wg_bf, wu_bf, wd_bf, _ff_tile(wg.shape[2]))
            og = moe_gather(o_sorted, idx_s, B * T)
        x_s = merge(x_s, o_a, o_b, o_m, hs, 8 * W // D, wa_bf, wb_bf, wm_bf, wo_bf, lg1, lb1, alpha,
                    DB, name="merge_sample")
        x_s = ffn(x_s, rw, wg_bf, wu_bf, wd_bf, lg2, lb2, alpha, DB, _ff_tile(wg.shape[2]), routed,
                  name="ffn_sample")
        if routed:
            x_p = moe_combine(x_p_mid, og, w12, lg2, lb2, alpha)
        else:
            x_p = ffn(x_p_mid, rw, wg_bf, wu_bf, wd_bf, lg2, lb2, alpha,
                      512, _ff_tile(wg.shape[2]), False, name="ffn_prompt")
        outs['aks'].append(sa_k.reshape(DB, TS, H_A, D_A))
        outs['avs'].append(sa_v.reshape(DB, TS, H_A, D_A))
        outs['iks'].append(s_ik.reshape(DB, TS, D_IDX))
        outs['bks'].append(sb_k.reshape(DB, TS, H_B, 2, D_B))
        outs['bvs'].append(sb_v.reshape(DB, TS, H_B, 2 * D_B))

    st = {n: jnp.stack(v) for n, v in outs.items()}
    return (x_p.reshape(B, T, D), x_s.reshape(DB, TS, D),
            st['akp'], st['avp'], st['ikp'], st['bkp'], st['bvp'], st['mkp'], st['mvp'],
            st['aks'], st['avs'], st['iks'], st['bks'], st['bvs'])
```

```python
import functools
import math

import jax
import jax.numpy as jnp
import numpy as np
from jax import lax
from jax.experimental import pallas as pl
from jax.experimental.pallas import tpu as pltpu
from jax.experimental.pallas import tpu_sc as plsc

F32 = jnp.float32
BF16 = jnp.bfloat16
HIGHEST = lax.Precision.HIGHEST

H_A, D_A = 8, 64
H_IDX, D_IDX = 8, 64
TOPK_MAX = 256
H_B, D_B = 4, 64
H_M, D_M = 4, 128
NUM_BUCKETS, MAX_DISTANCE = 32, 128
N_EXPERTS = 8
LN_EPS = 1e-5
SUBLN_EPS = 1e-5
A_WIDTH = H_A * D_A
B_WIDTH = H_B * 2 * D_B
M_WIDTH = H_M * D_M
HEAD_COLS = 512
PAGE = 128
N_GROUPS = 8

NEG = -1e30
KEY_NEG_INF = -2139095041
INT_MIN = -2147483648

VMEM_LIMIT = 56 * 1024 * 1024

TQ = 256
IDX_SPLIT = 3 * D_IDX


def _cparams(sem):
    return pltpu.CompilerParams(dimension_semantics=sem, vmem_limit_bytes=VMEM_LIMIT)


def _dot(a, b, precise=False):
    if precise:
        return jnp.dot(a, b, preferred_element_type=F32, precision=HIGHEST)
    return jnp.dot(a, b, preferred_element_type=F32)


def _dot_nt(a, b):
    return lax.dot_general(a, b, (((1,), (1,)), ((), ())), preferred_element_type=F32)


def _orderable(s):
    b = lax.bitcast_convert_type(s, jnp.int32)
    return jnp.where(b < 0, b ^ jnp.int32(0x7FFFFFFF), b)


def _layer_norm(v, g, b):
    mu = jnp.mean(v, axis=-1, keepdims=True)
    d = v - mu
    var = jnp.mean(d * d, axis=-1, keepdims=True)
    return d * lax.rsqrt(var + LN_EPS) * g + b


def _kth_largest(count_ge, shape, topk):
    t0 = jnp.full(shape, INT_MIN, jnp.int32)
    t0 = jnp.where(count_ge(jnp.zeros(shape, jnp.int32)) >= topk, jnp.zeros(shape, jnp.int32), t0)

    def bit_body(b, t):
        cand = t | jnp.left_shift(jnp.int32(1), 30 - b)
        return jnp.where(count_ge(cand) >= topk, cand, t)

    return lax.fori_loop(0, 31, bit_body, t0)


def _lambda(lv, lam_init):
    return (jnp.exp(jnp.sum(lv[0:1] * lv[1:2], axis=-1, keepdims=True))
            - jnp.exp(jnp.sum(lv[2:3] * lv[3:4], axis=-1, keepdims=True)) + lam_init)


def _mm_kernel(x_ref, w_ref, o_ref):
    o_ref[...] = _dot(x_ref[...].astype(BF16), w_ref[...])


def matmul(x, w, tm, tn, name="matmul"):
    M, K = x.shape
    N = w.shape[1]
    assert M % tm == 0 and N % tn == 0
    return pl.pallas_call(
        _mm_kernel,
        grid=(N // tn, M // tm),
        in_specs=[pl.BlockSpec((tm, K), lambda j, i: (i, 0)),
                  pl.BlockSpec((K, tn), lambda j, i: (0, j))],
        out_specs=pl.BlockSpec((tm, tn), lambda j, i: (i, j)),
        out_shape=jax.ShapeDtypeStruct((M, N), F32),
        compiler_params=_cparams(("arbitrary", "arbitrary")),
        name=name,
    )(x, w)


def _set_row(full, g, row):
    r = lax.broadcasted_iota(jnp.int32, full.shape, 0)
    return jnp.where(r == g, row, full)


def _attend_tile(kj, band_d, ml, *, n_groups, dv, qT_ref, k_ref, vT_ref, band_ref, am, acc_ref,
                 s_ref, p_ref, tq):
    m_all, l_all = ml
    ksl = pl.ds(pl.multiple_of(kj * tq, tq), tq)
    gpb = n_groups // band_ref.shape[0]
    gpv = (n_groups * dv) // vT_ref.shape[2]
    dq = qT_ref.shape[1] // n_groups
    m_out, l_out = m_all, l_all
    for g in range(n_groups):
        s_ref[g] = _dot(k_ref[0, g, ksl, :], qT_ref[0, g * dq:(g + 1) * dq, :])
    alphas = []
    for g in range(n_groups):
        s = s_ref[g]
        if am is not None:
            s = s + am
        if band_d is not None:
            s = s + band_ref[g // gpb, band_d]
        m_old = m_all[g:g + 1, :]
        m_new = jnp.maximum(m_old, jnp.max(s, axis=0, keepdims=True))
        alpha = jnp.exp(m_old - m_new)
        p = jnp.exp(s - m_new)
        p_ref[g] = p.astype(BF16)
        alphas.append(alpha)
        m_out = _set_row(m_out, g, m_new)
        l_out = _set_row(l_out, g, alpha * l_all[g:g + 1, :] + jnp.sum(p, axis=0, keepdims=True))
    for g in range(n_groups):
        vrows = slice((g // gpv) * dv, (g // gpv + 1) * dv)
        arows = slice(g * dv, (g + 1) * dv)
        acc_ref[arows, :] = alphas[g] * acc_ref[arows, :] + _dot(vT_ref[0, kj, vrows, :], p_ref[g])
    return m_out, l_out


def _attend_causal(i, tile_fn, n_groups, tq):
    ml = (jnp.full((n_groups, tq), NEG, F32), jnp.zeros((n_groups, tq), F32))
    ml = lax.fori_loop(0, jnp.maximum(i - 1, 0), lambda kj, c: tile_fn(kj, None, c), ml)
    ml = lax.cond(i >= 1, lambda c: tile_fn(i - 1, 1, c), lambda c: c, ml)
    return tile_fn(i, 0, ml)


def _attn_a_kernel(qT_ref, k_ref, vT_ref, iqT_ref, ik_ref, wT_ref, band_ref, low_ref,
                   o_ref, key_ref, am_ref, acc_ref, s_ref, p_ref, *, topk):
    i = pl.program_id(1)
    nk = i + 1
    tq = qT_ref.shape[2]
    qpos = i * tq + lax.broadcasted_iota(jnp.int32, (tq, tq), 1)
    krow = lax.broadcasted_iota(jnp.int32, (tq, tq), 0)

    def score_body(kj, c):
        ikt = ik_ref[0, pl.ds(pl.multiple_of(kj * tq, tq), tq), :]
        s = jnp.zeros((tq, tq), F32)
        for h in range(H_IDX):
            s = s + wT_ref[0, h:h + 1, :] * jnp.maximum(_dot(ikt, iqT_ref[0, h]), 0.0)
        s = jnp.where(kj * tq + krow <= qpos, s, -jnp.inf)
        key_ref[kj] = _orderable(s)
        return c

    lax.fori_loop(0, nk, score_body, 0)

    def count(pred):
        def tile_count(kj):
            t = jnp.where(pred(key_ref[kj]), 1.0, 0.0)
            return jnp.sum(t.reshape(tq // 8, 8, tq), axis=0)

        zero = jnp.zeros((8, tq), F32)
        c = lax.fori_loop(0, jnp.right_shift(nk, 1),
                          lambda j, c: c + tile_count(2 * j) + tile_count(2 * j + 1), zero)
        c = c + lax.cond((nk & 1) == 1, lambda: tile_count(nk - 1), lambda: zero)
        return jnp.sum(c, axis=0, keepdims=True)

    thr = _kth_largest(lambda cand: count(lambda s: s >= cand), (1, tq), topk)
    c_gt = count(lambda s: s > thr)
    need = jnp.where(thr > KEY_NEG_INF, topk - c_gt, 0.0)

    def mask_body(kj, run):
        s = key_ref[kj]
        tie = s == thr
        tie_f = jnp.where(tie, 1.0, 0.0)
        pre = _dot(low_ref[...], tie_f.astype(BF16)) + run
        keep_tie = jnp.where(pre < need, 0.0, NEG)
        am_ref[kj] = jnp.where(s > thr, 0.0, jnp.where(tie, keep_tie, NEG))
        return run + jnp.sum(tie_f, axis=0, keepdims=True)

    lax.fori_loop(0, nk, mask_body, jnp.zeros((1, tq), F32))

    acc_ref[...] = jnp.zeros_like(acc_ref)

    def tile_fn(kj, band_d, ml):
        return _attend_tile(kj, band_d, ml, n_groups=H_A, dv=D_A, qT_ref=qT_ref, k_ref=k_ref,
                            vT_ref=vT_ref, band_ref=band_ref, am=am_ref[kj], acc_ref=acc_ref,
                            s_ref=s_ref, p_ref=p_ref, tq=tq)

    _, l_all = _attend_causal(i, tile_fn, H_A, tq)
    inv = 1.0 / l_all
    oT = jnp.concatenate([acc_ref[h * D_A:(h + 1) * D_A, :] * inv[h:h + 1, :] for h in range(H_A)],
                         axis=0)
    o_ref[0] = oT.T


def attn_a_prompt(qT, k, vT, iqT, ik, wT, band, low, topk):
    B, H, T, D = k.shape
    nq = T // TQ
    W = H * D
    return pl.pallas_call(
        functools.partial(_attn_a_kernel, topk=float(topk)),
        grid=(B, nq),
        in_specs=[
            pl.BlockSpec((1, W, TQ), lambda b, i: (b, 0, i)),
            pl.BlockSpec((1, H, T, D), lambda b, i: (b, 0, 0, 0)),
            pl.BlockSpec((1, nq, W, TQ), lambda b, i: (b, 0, 0, 0)),
            pl.BlockSpec((1, H_IDX, IDX_SPLIT, TQ), lambda b, i: (b, 0, 0, i)),
            pl.BlockSpec((1, T, IDX_SPLIT), lambda b, i: (b, 0, 0)),
            pl.BlockSpec((1, H_IDX, TQ), lambda b, i: (b, 0, i)),
            pl.BlockSpec((H, 2, TQ, TQ), lambda b, i: (0, 0, 0, 0)),
            pl.BlockSpec((TQ, TQ), lambda b, i: (0, 0)),
        ],
        out_specs=pl.BlockSpec((1, TQ, W), lambda b, i: (b, i, 0)),
        out_shape=jax.ShapeDtypeStruct((B, T, W), F32),
        scratch_shapes=[pltpu.VMEM((nq, TQ, TQ), jnp.int32),
                        pltpu.VMEM((nq, TQ, TQ), F32),
                        pltpu.VMEM((W, TQ), F32),
                        pltpu.VMEM((H, TQ, TQ), F32), pltpu.VMEM((H, TQ, TQ), BF16)],
        compiler_params=_cparams(("arbitrary", "arbitrary")),
        name="attn_a_prompt",
    )(qT, k, vT, iqT, ik, wT, band, low)


def _attn_b_kernel(qT_ref, k_ref, vT_ref, band_ref, lam_ref, g_ref, o_ref,
                   acc_ref, s_ref, p_ref, *, lam_init):
    i = pl.program_id(1)
    tq = qT_ref.shape[2]
    dv = 2 * D_B
    acc_ref[...] = jnp.zeros_like(acc_ref)

    def tile_fn(kj, band_d, ml):
        return _attend_tile(kj, band_d, ml, n_groups=2 * H_B, dv=dv, qT_ref=qT_ref, k_ref=k_ref,
                            vT_ref=vT_ref, band_ref=band_ref, am=None, acc_ref=acc_ref,
                            s_ref=s_ref, p_ref=p_ref, tq=tq)

    _, l_all = _attend_causal(i, tile_fn, 2 * H_B, tq)
    lam = _lambda(lam_ref[...], lam_init)
    inv = 1.0 / l_all
    parts = []
    for h in range(H_B):
        g0, g1 = 2 * h, 2 * h + 1
        o = (acc_ref[g0 * dv:(g0 + 1) * dv, :] * inv[g0:g0 + 1, :]
             - lam * (acc_ref[g1 * dv:(g1 + 1) * dv, :] * inv[g1:g1 + 1, :]))
        o = o * lax.rsqrt(jnp.mean(o * o, axis=0, keepdims=True) + SUBLN_EPS)
        parts.append(o * g_ref[...] * (1.0 - lam_init))
    o_ref[0] = jnp.concatenate(parts, axis=0).T


def attn_b_prompt(qT, k, vT, band, lamv, gcol, lam_init):
    B, G, T, D = k.shape
    nq = T // TQ
    W = G * D
    return pl.pallas_call(
        functools.partial(_attn_b_kernel, lam_init=lam_init),
        grid=(B, nq),
        in_specs=[
            pl.BlockSpec((1, W, TQ), lambda b, i: (b, 0, i)),
            pl.BlockSpec((1, G, T, D), lambda b, i: (b, 0, 0, 0)),
            pl.BlockSpec((1, nq, W, TQ), lambda b, i: (b, 0, 0, 0)),
            pl.BlockSpec((H_B, 2, TQ, TQ), lambda b, i: (0, 0, 0, 0)),
            pl.BlockSpec((4, D_B), lambda b, i: (0, 0)),
            pl.BlockSpec((2 * D_B, 1), lambda b, i: (0, 0)),
        ],
        out_specs=pl.BlockSpec((1, TQ, W), lambda b, i: (b, i, 0)),
        out_shape=jax.ShapeDtypeStruct((B, T, W), F32),
        scratch_shapes=[pltpu.VMEM((G * 2 * D_B, TQ), F32),
                        pltpu.VMEM((G, TQ, TQ), F32), pltpu.VMEM((G, TQ, TQ), BF16)],
        compiler_params=_cparams(("arbitrary", "arbitrary")),
        name="attn_b_prompt",
    )(qT, k, vT, band, lamv, gcol)


def _attn_m_kernel(q_ref, k_ref, v_ref, o_ref):
    for h in range(H_M):
        s = _dot_nt(q_ref[0, h], k_ref[0, h])
        m = jnp.max(s, axis=-1, keepdims=True)
        p = jnp.exp(s - m)
        l = jnp.sum(p, axis=-1, keepdims=True)
        o_ref[0, :, h * D_M:(h + 1) * D_M] = _dot(p.astype(BF16), v_ref[0, h]) / l


def attn_m_prompt(q, k, v):
    B, H, T, D = q.shape
    NM = k.shape[2]
    tq = 512
    return pl.pallas_call(
        _attn_m_kernel,
        grid=(B, T // tq),
        in_specs=[
            pl.BlockSpec((1, H, tq, D), lambda b, i: (b, 0, i, 0)),
            pl.BlockSpec((1, H, NM, D), lambda b, i: (b, 0, 0, 0)),
            pl.BlockSpec((1, H, NM, D), lambda b, i: (b, 0, 0, 0)),
        ],
        out_specs=pl.BlockSpec((1, tq, M_WIDTH), lambda b, i: (b, i, 0)),
        out_shape=jax.ShapeDtypeStruct((B, T, M_WIDTH), F32),
        compiler_params=_cparams(("arbitrary", "arbitrary")),
        name="attn_m_prompt",
    )(q, k, v)


def _merge_kernel(x_ref, oa_ref, ob_ref, om_ref, g0_ref, g1_ref, g2_ref,
                  wa_ref, wb_ref, wm_ref, wo_ref, lg_ref, lb_ref, o_ref, *, alpha):
    ya = _dot(oa_ref[...].astype(BF16), wa_ref[...])
    yb = _dot(ob_ref[...].astype(BF16), wb_ref[...])
    ym = _dot(om_ref[...].astype(BF16), wm_ref[...])
    hmix = (jax.nn.sigmoid(g0_ref[...]) * ya + jax.nn.sigmoid(g1_ref[...]) * yb
            + jax.nn.sigmoid(g2_ref[...]) * ym)
    mix = _dot(hmix.astype(BF16), wo_ref[...])
    o_ref[...] = _layer_norm(alpha * x_ref[...] + mix, lg_ref[...], lb_ref[...])


def merge(x, oa, ob, om, hproj, gate_col0, wa, wb, wm, wo, lg, lb, alpha, tm, name="merge"):
    M, D = x.shape
    row = lambda i: (i, 0)
    const = lambda i: (0, 0)
    gspec = [pl.BlockSpec((tm, D), (lambda i, c=c: (i, gate_col0 + c))) for c in range(3)]
    return pl.pallas_call(
        functools.partial(_merge_kernel, alpha=alpha),
        grid=(M // tm,),
        in_specs=[pl.BlockSpec((tm, D), row),
                  pl.BlockSpec((tm, HEAD_COLS), row), pl.BlockSpec((tm, HEAD_COLS), row),
                  pl.BlockSpec((tm, HEAD_COLS), row)] + gspec + [
                  pl.BlockSpec((HEAD_COLS, D), const), pl.BlockSpec((HEAD_COLS, D), const),
                  pl.BlockSpec((HEAD_COLS, D), const), pl.BlockSpec((D, D), const),
                  pl.BlockSpec((1, D), const), pl.BlockSpec((1, D), const)],
        out_specs=pl.BlockSpec((tm, D), row),
        out_shape=jax.ShapeDtypeStruct((M, D), F32),
        compiler_params=_cparams(("arbitrary",)),
        name=name,
    )(x, oa, ob, om, hproj, hproj, hproj, wa, wb, wm, wo, lg, lb)


def _ffn_kernel(x_ref, rw_ref, wg_ref, wu_ref, wd_ref, lg_ref, lb_ref, o_ref,
                acc_ref, gate_ref, *, alpha, routed, n_experts):
    e = pl.program_id(1)
    f = pl.program_id(2)
    first = jnp.logical_and(e == 0, f == 0)
    last = jnp.logical_and(e == pl.num_programs(1) - 1, f == pl.num_programs(2) - 1)
    x = x_ref[...]

    @pl.when(first)
    def _():
        acc_ref[...] = jnp.zeros_like(acc_ref)
        if routed:
            logits = _dot(x.astype(BF16), rw_ref[...])
            i1, i2, w1, w2 = _top2(logits, n_experts)
            lane = lax.broadcasted_iota(jnp.int32, logits.shape, 1)
            gate_ref[...] = jnp.where(lane == i1, w1, 0.0) + jnp.where(lane == i2, w2, 0.0)

    xin = x.astype(BF16)
    hg = _dot(xin, wg_ref[0])
    hu = _dot(xin, wu_ref[0])
    hh = hg * jax.nn.sigmoid(hg) * hu
    part = _dot(hh.astype(BF16), wd_ref[0])
    if routed:
        gate = gate_ref[...]
        lane = lax.broadcasted_iota(jnp.int32, gate.shape, 1)
        part = part * jnp.sum(jnp.where(lane == e, gate, 0.0), axis=-1, keepdims=True)
    acc_ref[...] += part

    @pl.when(last)
    def _():
        o_ref[...] = _layer_norm(alpha * x + acc_ref[...], lg_ref[...], lb_ref[...])


def ffn(x, rw, wg, wu, wd, lg, lb, alpha, tm, tf, routed, name="ffn"):
    M, D = x.shape
    E, _, F = wg.shape
    assert M % tm == 0 and F % tf == 0
    return pl.pallas_call(
        functools.partial(_ffn_kernel, alpha=alpha, routed=routed, n_experts=E),
        grid=(M // tm, E, F // tf),
        in_specs=[pl.BlockSpec((tm, D), lambda i, e, f: (i, 0)),
                  pl.BlockSpec(rw.shape, lambda i, e, f: (0, 0)),
                  pl.BlockSpec((1, D, tf), lambda i, e, f: (e, 0, f)),
                  pl.BlockSpec((1, D, tf), lambda i, e, f: (e, 0, f)),
                  pl.BlockSpec((1, tf, D), lambda i, e, f: (e, f, 0)),
                  pl.BlockSpec((1, D), lambda i, e, f: (0, 0)),
                  pl.BlockSpec((1, D), lambda i, e, f: (0, 0))],
        out_specs=pl.BlockSpec((tm, D), lambda i, e, f: (i, 0)),
        out_shape=jax.ShapeDtypeStruct((M, D), F32),
        scratch_shapes=[pltpu.VMEM((tm, D), F32), pltpu.VMEM((tm, rw.shape[1]), F32)],
        compiler_params=_cparams(("arbitrary", "arbitrary", "arbitrary")),
        name=name,
    )(x, rw, wg, wu, wd, lg, lb)


MOE_ROW_TILE = 512
FF_TILE_MAX = 1408


def _ff_tile(d_ff):
    return max(t for t in range(128, min(d_ff, FF_TILE_MAX) + 1, 128) if d_ff % t == 0)
SC_WINDOW = 128
SC_ROW = 128


def _top2(logits, n_experts):
    ne = logits.shape[-1]
    lane = lax.broadcasted_iota(jnp.int32, logits.shape, 1)
    logits = jnp.where(lane < n_experts, logits, -jnp.inf)
    m1 = jnp.max(logits, axis=-1, keepdims=True)
    i1 = jnp.min(jnp.where(logits == m1, lane, ne), axis=-1, keepdims=True)
    rest = jnp.where(lane == i1, -jnp.inf, logits)
    m2 = jnp.max(rest, axis=-1, keepdims=True)
    i2 = jnp.min(jnp.where(rest == m2, lane, ne), axis=-1, keepdims=True)
    e2 = jnp.exp(m2 - m1)
    return i1, i2, 1.0 / (1.0 + e2), e2 / (1.0 + e2)


def _route_kernel(x_ref, rw_ref, low_ref, e_ref, r_ref, w_ref, cnt_ref, run_ref, *, n_experts):
    @pl.when(pl.program_id(0) == 0)
    def _():
        run_ref[...] = jnp.zeros_like(run_ref)

    logits = _dot(x_ref[...].astype(BF16), rw_ref[...])
    i1, i2, w1, w2 = _top2(logits, n_experts)
    lane = lax.broadcasted_iota(jnp.int32, logits.shape, 1)
    oh1 = jnp.where(lane == i1, 1.0, 0.0)
    oh2 = jnp.where(lane == i2, 1.0, 0.0)
    oh = oh1 + oh2
    pre = _dot(low_ref[...], oh.astype(BF16)) + run_ref[...]
    r1 = jnp.sum(oh1 * pre, axis=-1, keepdims=True)
    r2 = jnp.sum(oh2 * pre, axis=-1, keepdims=True)
    run_ref[...] = run_ref[...] + jnp.sum(oh, axis=0, keepdims=True)
    first = lax.broadcasted_iota(jnp.int32, e_ref.shape, 1) == 0
    e_ref[...] = jnp.where(first, i1, i2)
    r_ref[...] = jnp.where(first, r1, r2).astype(jnp.int32)
    w_ref[...] = jnp.where(first, w1, w2)
    cnt_ref[...] = run_ref[...]


def moe_route(x, rw, n_experts, tm=512):
    M, D = x.shape
    low = jnp.asarray(np.tril(np.ones((tm, tm), np.float32), -1), BF16)
    pair = pl.BlockSpec((tm, 2), lambda i: (i, 0))
    return pl.pallas_call(
        functools.partial(_route_kernel, n_experts=n_experts),
        grid=(M // tm,),
        in_specs=[pl.BlockSpec((tm, D), lambda i: (i, 0)),
                  pl.BlockSpec(rw.shape, lambda i: (0, 0)),
                  pl.BlockSpec((tm, tm), lambda i: (0, 0))],
        out_specs=[pair, pair, pair, pl.BlockSpec((1, 128), lambda i: (0, 0))],
        out_shape=[jax.ShapeDtypeStruct((M, 2), jnp.int32), jax.ShapeDtypeStruct((M, 2), jnp.int32),
                   jax.ShapeDtypeStruct((M, 2), F32), jax.ShapeDtypeStruct((1, 128), F32)],
        scratch_shapes=[pltpu.VMEM((1, 128), F32)],
        compiler_params=_cparams(("arbitrary",)),
        name="moe_route",
    )(x, rw, low)


def _gmm_kernel(te_ref, nu_ref, x_ref, wg_ref, wu_ref, wd_ref, o_ref, acc_ref):
    i = pl.program_id(0)
    f = pl.program_id(1)

    @pl.when(i < nu_ref[0])
    def _():
        @pl.when(f == 0)
        def _():
            acc_ref[...] = jnp.zeros_like(acc_ref)

        x = x_ref[...].astype(BF16)
        hg = _dot(x, wg_ref[0])
        hu = _dot(x, wu_ref[0])
        hh = hg * jax.nn.sigmoid(hg) * hu
        acc_ref[...] += _dot(hh.astype(BF16), wd_ref[0])

        @pl.when(f == pl.num_programs(1) - 1)
        def _():
            o_ref[...] = acc_ref[...]


def moe_gmm(tile_expert, n_used, xs, wg, wu, wd, tf):
    R, D = xs.shape
    F = wg.shape[2]
    tr = MOE_ROW_TILE
    grid_spec = pltpu.PrefetchScalarGridSpec(
        num_scalar_prefetch=2,
        grid=(R // tr, F // tf),
        in_specs=[pl.BlockSpec((tr, D), lambda i, f, te, nu: (i, 0)),
                  pl.BlockSpec((1, D, tf), lambda i, f, te, nu: (te[i], 0, f)),
                  pl.BlockSpec((1, D, tf), lambda i, f, te, nu: (te[i], 0, f)),
                  pl.BlockSpec((1, tf, D), lambda i, f, te, nu: (te[i], f, 0))],
        out_specs=pl.BlockSpec((tr, D), lambda i, f, te, nu: (i, 0)),
        scratch_shapes=[pltpu.VMEM((tr, D), F32)],
    )
    return pl.pallas_call(
        _gmm_kernel,
        grid_spec=grid_spec,
        out_shape=jax.ShapeDtypeStruct((R, D), F32),
        compiler_params=_cparams(("arbitrary", "arbitrary")),
        name="moe_gmm",
    )(tile_expert, n_used, xs, wg, wu, wd)


def _combine_kernel(x_ref, o1_ref, o2_ref, w_ref, lg_ref, lb_ref, o_ref, *, alpha):
    w = w_ref[...]
    y = w[:, 0:1] * o1_ref[...] + w[:, 1:2] * o2_ref[...]
    o_ref[...] = _layer_norm(alpha * x_ref[...] + y, lg_ref[...], lb_ref[...])


def moe_combine(x, og, w12, lg, lb, alpha, tm=512):
    M, D = x.shape
    nb = M // tm
    return pl.pallas_call(
        functools.partial(_combine_kernel, alpha=alpha),
        grid=(nb,),
        in_specs=[pl.BlockSpec((tm, D), lambda i: (i, 0)),
                  pl.BlockSpec((tm, D), lambda i: (i, 0)),
                  pl.BlockSpec((tm, D), lambda i: (i + nb, 0)),
                  pl.BlockSpec((tm, 2), lambda i: (i, 0)),
                  pl.BlockSpec((1, D), lambda i: (0, 0)),
                  pl.BlockSpec((1, D), lambda i: (0, 0))],
        out_specs=pl.BlockSpec((tm, D), lambda i: (i, 0)),
        out_shape=jax.ShapeDtypeStruct((M, D), F32),
        compiler_params=_cparams(("arbitrary",)),
        name="moe_combine",
    )(x, og, og, w12, lg, lb)


def _sc_mesh():
    return plsc.VectorSubcoreMesh(core_axis_name="c", subcore_axis_name="s")


def sc_scatter_rows(src, idx, n_out_rows):
    S = src.shape[0]
    M = idx.shape[0]
    nb = S // SC_WINDOW
    assert S % SC_WINDOW == 0 and M % S == 0

    @pl.kernel(out_type=jax.ShapeDtypeStruct((n_out_rows, SC_ROW), src.dtype), mesh=_sc_mesh(),
               scratch_types=[])
    def scatter_kernel(x_hbm, i_hbm, o_hbm):
        def body(x_vmem, i_vmem):
            pltpu.sync_copy(x_vmem, o_hbm.at[i_vmem.at[0]])

        pltpu.emit_pipeline(
            body,
            grid=(M // SC_WINDOW,),
            in_specs=[pl.BlockSpec((SC_WINDOW, SC_ROW), lambda i: (lax.rem(i, nb), 0)),
                      pl.BlockSpec((1, SC_WINDOW), lambda i: (0, i))],
            out_specs=[],
            core_axis_name=("c", "s"),
            dimension_semantics=(pltpu.PARALLEL,),
        )(x_hbm, i_hbm)

    return scatter_kernel(src, idx.reshape(1, M))


def sc_gather_rows(table, idx):
    M = idx.shape[0]
    assert M % SC_WINDOW == 0

    @pl.kernel(out_type=jax.ShapeDtypeStruct((M, SC_ROW), table.dtype), mesh=_sc_mesh(),
               scratch_types=[])
    def gather_kernel(t_hbm, i_hbm, o_hbm):
        def body(i_vmem, o_vmem):
            pltpu.sync_copy(t_hbm.at[i_vmem.at[0]], o_vmem)

        pltpu.emit_pipeline(
            body,
            grid=(M // SC_WINDOW,),
            in_specs=[pl.BlockSpec((1, SC_WINDOW), lambda i: (0, i))],
            out_specs=[pl.BlockSpec((SC_WINDOW, SC_ROW), lambda i: (i, 0))],
            core_axis_name=("c", "s"),
            dimension_semantics=(pltpu.PARALLEL,),
        )(i_hbm, o_hbm)

    return gather_kernel(table, idx.reshape(1, M))


def moe_dispatch(x, rw, n_experts):
    N, D = x.shape
    tr = MOE_ROW_TILE
    e12, r12, w12, counts = moe_route(x, rw, n_experts)
    cnt = counts[0, :n_experts].astype(jnp.int32)
    cnt_pad = (cnt + tr - 1) // tr * tr
    off_end = jnp.cumsum(cnt_pad)
    off = off_end - cnt_pad
    R = 2 * N + n_experts * tr
    eids = jnp.arange(n_experts, dtype=jnp.int32)
    pos = jnp.sum(jnp.where(e12[..., None] == eids, off, 0), axis=-1) + r12
    tile_start = jnp.arange(R // tr, dtype=jnp.int32) * tr
    tile_expert = jnp.minimum(jnp.sum(tile_start[:, None] >= off_end[None, :], axis=1),
                              n_experts - 1).astype(jnp.int32)
    n_used = (off_end[-1:] // tr).astype(jnp.int32)
    pos_flat = pos.T.reshape(2 * N)
    idx_s = _chunk_index(pos_flat, D)
    xs = _from_chunks(sc_scatter_rows(_as_chunks(x), idx_s, R * (D // SC_ROW)), R, D)
    return xs, tile_expert, n_used, idx_s, w12


def moe_gather(o_sorted, idx_s, n_tokens):
    return _from_chunks(sc_gather_rows(_as_chunks(o_sorted), idx_s), 2 * n_tokens, o_sorted.shape[1])


def _as_chunks(a):
    rows, d = a.shape
    return a.reshape(rows // 8, 8, d // SC_ROW, SC_ROW).transpose(0, 2, 1, 3).reshape(-1, SC_ROW)


def _from_chunks(c, rows, d):
    return c.reshape(rows // 8, d // SC_ROW, 8, SC_ROW).transpose(0, 2, 1, 3).reshape(rows, d)


def _chunk_index(row_of, d):
    nk = d // SC_ROW
    r = row_of.reshape(-1, 1, 8)
    k = jnp.arange(nk, dtype=jnp.int32).reshape(1, nk, 1)
    return ((r // 8) * (8 * nk) + k * 8 + r % 8).reshape(-1)


IDX_PAGES_PER_STEP = 32
ATTN_PAGES_PER_STEP = 16


def _page_specs(layer, n, rows, P):
    return [pl.BlockSpec((1, 1, rows, PAGE), (lambda b, s, pt, j=j: (layer, pt[b, s * P + j], 0, 0)))
            for j in range(n)]


def _idx_score_kernel(pt_ref, iqT_ref, iq3_ref, iw_ref, iwc_ref, iknew_ref, *refs, P):
    pages = refs[:P]
    key_ref, knew_ref = refs[P:]
    step = pl.program_id(1)
    wic = iwc_ref[0] * (H_IDX ** -0.5)
    iq3 = iq3_ref[0]
    for j in range(P):
        kt = pages[j][0, 0]
        hi = kt.astype(BF16)
        lo = (kt - hi.astype(F32)).astype(BF16)
        d = _dot(iq3, jnp.concatenate([hi, lo, hi], axis=0))
        sc = jnp.sum(wic * jnp.maximum(d, 0.0), axis=0, keepdims=True)
        key_ref[0, pl.ds(step * P + j, 1), :] = _orderable(sc + 0.0)

    @pl.when(step == pl.num_programs(1) - 1)
    def _():
        iqT = iqT_ref[0] * (D_IDX ** -0.5)
        wi = iw_ref[0] * (H_IDX ** -0.5)
        dn = jnp.sum(iqT * iknew_ref[0], axis=0, keepdims=True)
        sn = jnp.sum(wi * jnp.maximum(dn, 0.0), axis=-1, keepdims=True) + 0.0
        knew_ref[0] = jnp.broadcast_to(_orderable(sn), knew_ref.shape[1:])


def _idx_select_kernel(key_ref, knew_ref, tri_ref, low_ref, am_ref, amnew_ref, *, topk, n_samples):
    s = key_ref[...]
    npg = s.shape[0] // n_samples
    kn = knew_ref[...][:, 0:1]

    def per_sample(x):
        return jnp.sum(jnp.sum(x.reshape(n_samples, npg, x.shape[-1]), axis=1), axis=-1, keepdims=True)

    def count_ge(cand):
        t = jnp.where(s >= _expand_rows(cand, npg), 1.0, 0.0)
        return per_sample(t) + jnp.where(kn >= cand, 1.0, 0.0)

    thr = _kth_largest(count_ge, (n_samples, 1), topk)
    thr_r = _expand_rows(thr, npg)
    c_gt = per_sample(jnp.where(s > thr_r, 1.0, 0.0)) + jnp.where(kn > thr, 1.0, 0.0)
    need = topk - c_gt
    tie = s == thr_r
    tie_f = jnp.where(tie, 1.0, 0.0)
    in_row = _dot(tie_f.astype(BF16), tri_ref[...])
    row_tot = jnp.broadcast_to(jnp.sum(tie_f, axis=-1, keepdims=True), tie_f.shape)
    before = jnp.einsum('bpq,bql->bpl', low_ref[...],
                        row_tot.astype(BF16).reshape(n_samples, npg, PAGE),
                        preferred_element_type=F32).reshape(s.shape)
    keep_tie = jnp.where(in_row + before < _expand_rows(need, npg), 0.0, NEG)
    am_ref[...] = jnp.where(s > thr_r, 0.0, jnp.where(tie, keep_tie, NEG))
    keep_new = jnp.where(per_sample(tie_f) < need, 0.0, NEG)
    amn = jnp.where(kn > thr, 0.0, jnp.where(kn == thr, keep_new, NEG))
    amnew_ref[...] = jnp.broadcast_to(amn, amnew_ref.shape)


def idx_sample(page_table, layer, iq, iw, ik_new, poolT, topk):
    DB, npg = page_table.shape
    iqT = jnp.transpose(iq, (0, 2, 1))
    iq3 = _split3(iq * (D_IDX ** -0.5), (0, 0, 1))
    iwc = iw.reshape(DB, H_IDX, 1)
    iw = iw.reshape(DB, 1, H_IDX)
    ik_new = ik_new.reshape(DB, D_IDX, 1)
    P = min(IDX_PAGES_PER_STEP, npg)
    assert npg % P == 0
    per_b = lambda b, s, pt: (b, 0, 0)
    grid_spec = pltpu.PrefetchScalarGridSpec(
        num_scalar_prefetch=1,
        grid=(DB, npg // P),
        in_specs=[pl.BlockSpec((1, D_IDX, H_IDX), per_b),
                  pl.BlockSpec((1, H_IDX, IDX_SPLIT), per_b),
                  pl.BlockSpec((1, 1, H_IDX), per_b),
                  pl.BlockSpec((1, H_IDX, 1), per_b),
                  pl.BlockSpec((1, D_IDX, 1), per_b)]
                 + _page_specs(layer, P, D_IDX, P),
        out_specs=[pl.BlockSpec((1, npg, PAGE), per_b),
                   pl.BlockSpec((1, 1, PAGE), per_b)],
    )
    keys, knew = pl.pallas_call(
        functools.partial(_idx_score_kernel, P=P),
        grid_spec=grid_spec,
        out_shape=[jax.ShapeDtypeStruct((DB, npg, PAGE), jnp.int32),
                   jax.ShapeDtypeStruct((DB, 1, PAGE), jnp.int32)],
        compiler_params=_cparams(("arbitrary", "arbitrary")),
        name="idx_score",
    )(page_table, iqT, iq3, iw, iwc, ik_new, *([poolT] * P))
    tri = jnp.asarray(np.triu(np.ones((PAGE, PAGE), np.float32), 1), BF16)
    low = jnp.asarray(np.broadcast_to(np.tril(np.ones((npg, npg), np.float32), -1), (DB, npg, npg)), BF16)
    whole = lambda shape: pl.BlockSpec(shape, lambda i: (0,) * len(shape))
    am, amn = pl.pallas_call(
        functools.partial(_idx_select_kernel, topk=float(topk), n_samples=DB),
        grid=(1,),
        in_specs=[whole((DB * npg, PAGE)), whole((DB, PAGE)), whole((PAGE, PAGE)), whole((DB, npg, npg))],
        out_specs=[whole((DB * npg, PAGE)), whole((DB, PAGE))],
        out_shape=[jax.ShapeDtypeStruct((DB * npg, PAGE), F32), jax.ShapeDtypeStruct((DB, PAGE), F32)],
        compiler_params=_cparams(("arbitrary",)),
        name="idx_select",
    )(keys.reshape(DB * npg, PAGE), knew.reshape(DB, PAGE), tri, low)
    return am.reshape(DB, npg, PAGE), amn.reshape(DB, 1, PAGE)


def _expand_rows(x, rep):
    g, n = x.shape
    return jnp.broadcast_to(x[:, None, :], (g, rep, n)).reshape(g * rep, n)


def _group_logits(kt, qb):
    w, n = kt.shape
    return jnp.sum((kt * qb).reshape(N_GROUPS, w // N_GROUPS, n), axis=1)


def _dec_a_kernel(pt_ref, q_ref, am_ref, amnew_ref, knew_ref, vnew_ref, blast_ref, bnew_ref,
                  *refs, scale, P):
    kp = refs[:P]
    vp = refs[P:2 * P]
    o_ref, qb_ref, m_ref, l_ref, acc_ref = refs[2 * P:]
    step = pl.program_id(1)
    is_last = step == pl.num_programs(1) - 1
    W = qb_ref.shape[0]
    rep = W // N_GROUPS

    @pl.when(step == 0)
    def _():
        qb_ref[...] = jnp.broadcast_to(q_ref[0] * scale, qb_ref.shape)
        m_ref[...] = jnp.full_like(m_ref, NEG)
        l_ref[...] = jnp.zeros_like(l_ref)
        acc_ref[...] = jnp.zeros_like(acc_ref)

    last_f = jnp.where(is_last, 1.0, 0.0)
    for j in range(P):
        lg = _group_logits(kp[j][0, 0], qb_ref[...]) + am_ref[0, j:j + 1, :]
        if j == P - 1:
            lg = lg + last_f * blast_ref[...]
        m_old = m_ref[...]
        m_new = jnp.maximum(m_old, lg)
        alpha = jnp.exp(m_old - m_new)
        p = jnp.exp(lg - m_new)
        m_ref[...] = m_new
        l_ref[...] = l_ref[...] * alpha + p
        acc_ref[...] = acc_ref[...] * _expand_rows(alpha, rep) + _expand_rows(p, rep) * vp[j][0, 0]

    @pl.when(is_last)
    def _():
        q = q_ref[0] * scale
        lgn = (jnp.sum((knew_ref[0] * q).reshape(N_GROUPS, rep, 1), axis=1)
               + bnew_ref[...] + amnew_ref[0][:, 0:1])
        m = m_ref[...]
        mx = jnp.maximum(jnp.max(m, axis=-1, keepdims=True), lgn)
        w = jnp.exp(m - mx)
        pn = jnp.exp(lgn - mx)
        l = jnp.sum(l_ref[...] * w, axis=-1, keepdims=True) + pn
        o = (jnp.sum(acc_ref[...] * _expand_rows(w, rep), axis=-1, keepdims=True)
             + _expand_rows(pn, rep) * vnew_ref[0])
        o_ref[0] = o / _expand_rows(l, rep)


def decode_a(page_table, layer, qcol, am, amnew, knew, vnew, blast, bnew, kT, vT, scale):
    DB, npg = page_table.shape
    W = qcol.shape[1]
    P = min(ATTN_PAGES_PER_STEP, npg)
    assert npg % P == 0
    per_b = lambda b, s, pt: (b, 0, 0)
    c2 = lambda b, s, pt: (0, 0)
    grid_spec = pltpu.PrefetchScalarGridSpec(
        num_scalar_prefetch=1,
        grid=(DB, npg // P),
        in_specs=[pl.BlockSpec((1, W, 1), per_b),
                  pl.BlockSpec((1, P, PAGE), lambda b, s, pt: (b, s, 0)),
                  pl.BlockSpec((1, 1, PAGE), per_b),
                  pl.BlockSpec((1, W, 1), per_b),
                  pl.BlockSpec((1, W, 1), per_b),
                  pl.BlockSpec((N_GROUPS, PAGE), c2),
                  pl.BlockSpec((N_GROUPS, 1), c2)]
                 + _page_specs(layer, P, W, P) + _page_specs(layer, P, W, P),
        out_specs=pl.BlockSpec((1, W, 1), per_b),
        scratch_shapes=[pltpu.VMEM((W, PAGE), F32), pltpu.VMEM((N_GROUPS, PAGE), F32),
                        pltpu.VMEM((N_GROUPS, PAGE), F32), pltpu.VMEM((W, PAGE), F32)],
    )
    return pl.pallas_call(
        functools.partial(_dec_a_kernel, scale=scale, P=P),
        grid_spec=grid_spec,
        out_shape=jax.ShapeDtypeStruct((DB, W, 1), F32),
        compiler_params=_cparams(("arbitrary", "arbitrary")),
        name="decode_a",
    )(page_table, qcol, am, amnew, knew, vnew, blast, bnew, *([kT] * P), *([vT] * P))


def _dec_b_kernel(pt_ref, q_ref, knew_ref, vnew_ref, blast_ref, bnew_ref, r_ref, pm_ref,
                  lam_ref, g_ref, *refs, scale, lam_init, P):
    kp = refs[:P]
    vp = refs[P:2 * P]
    o_ref, qb_ref, m_ref, l_ref, acc_ref = refs[2 * P:]
    step = pl.program_id(1)
    is_last = step == pl.num_programs(1) - 1
    W = qb_ref.shape[0]
    rep = W // N_GROUPS

    @pl.when(step == 0)
    def _():
        qb_ref[...] = jnp.broadcast_to(q_ref[0] * scale, qb_ref.shape)
        m_ref[...] = jnp.full_like(m_ref, NEG)
        l_ref[...] = jnp.zeros_like(l_ref)
        acc_ref[...] = jnp.zeros_like(acc_ref)

    last_f = jnp.where(is_last, 1.0, 0.0)
    lgs = [_group_logits(kp[j][0, 0], qb_ref[...]) for j in range(P)]
    lgs[P - 1] = lgs[P - 1] + last_f * blast_ref[...]
    m_old = m_ref[...]
    m_new = m_old
    for lg in lgs:
        m_new = jnp.maximum(m_new, jnp.max(lg, axis=-1, keepdims=True))
    alpha = jnp.exp(m_old - m_new)
    ps = [jnp.exp(lg - m_new) for lg in lgs]
    m_ref[...] = m_new
    l_ref[...] = l_ref[...] * alpha + sum(jnp.sum(p, axis=-1, keepdims=True) for p in ps)
    pr = _dot(jnp.concatenate(ps, axis=0).astype(BF16), r_ref[...])
    o = jnp.zeros(acc_ref.shape, F32)
    for j in range(P):
        prj = (pr[j * N_GROUPS:(j + 1) * N_GROUPS] * pm_ref[...]).astype(BF16)
        o = o + _dot(prj, vp[j][0, 0].astype(BF16))
    acc_ref[...] = acc_ref[...] * alpha + o

    @pl.when(is_last)
    def _():
        q = q_ref[0] * scale
        lgn = jnp.sum((knew_ref[0] * q).reshape(N_GROUPS, rep, 1), axis=1) + bnew_ref[...]
        m = m_ref[...]
        mx = jnp.maximum(m, lgn)
        a2 = jnp.exp(m - mx)
        pn = jnp.exp(lgn - mx)
        l = l_ref[...] * a2 + pn
        on = (acc_ref[...] * a2 + pn * vnew_ref[0]) / l
        lam = _lambda(lam_ref[...], lam_init)
        r = lax.broadcasted_iota(jnp.int32, (N_GROUPS, N_GROUPS), 0)
        c = lax.broadcasted_iota(jnp.int32, (N_GROUPS, N_GROUPS), 1)
        comb = jnp.where(c == 2 * r, 1.0, 0.0) - lam * jnp.where(c == 2 * r + 1, 1.0, 0.0)
        o = _dot(comb, on, True)
        o = o * lax.rsqrt(jnp.mean(o * o, axis=-1, keepdims=True) + SUBLN_EPS)
        o_ref[0] = o * g_ref[...] * (1.0 - lam_init)


def decode_b(page_table, layer, qcol, knew, vnew_g, blast, bnew, rmat, pmask, lamv, g, kT, vrows,
             scale, lam_init):
    DB, npg = page_table.shape
    W = qcol.shape[1]
    P = min(ATTN_PAGES_PER_STEP, npg)
    dv = 2 * D_B
    assert npg % P == 0
    per_b = lambda b, s, pt: (b, 0, 0)
    c2 = lambda b, s, pt: (0, 0)
    grid_spec = pltpu.PrefetchScalarGridSpec(
        num_scalar_prefetch=1,
        grid=(DB, npg // P),
        in_specs=[pl.BlockSpec((1, W, 1), per_b),
                  pl.BlockSpec((1, W, 1), per_b),
                  pl.BlockSpec((1, N_GROUPS, dv), per_b),
                  pl.BlockSpec((N_GROUPS, PAGE), c2),
                  pl.BlockSpec((N_GROUPS, 1), c2),
                  pl.BlockSpec((PAGE, W), c2),
                  pl.BlockSpec((N_GROUPS, W), c2),
                  pl.BlockSpec((4, D_B), c2),
                  pl.BlockSpec((1, dv), c2)]
                 + _page_specs(layer, P, W, P) + _page_specs(layer, P, W, P),
        out_specs=pl.BlockSpec((1, N_GROUPS, dv), per_b),
        scratch_shapes=[pltpu.VMEM((W, PAGE), F32), pltpu.VMEM((N_GROUPS, 1), F32),
                        pltpu.VMEM((N_GROUPS, 1), F32), pltpu.VMEM((N_GROUPS, dv), F32)],
    )
    return pl.pallas_call(
        functools.partial(_dec_b_kernel, scale=scale, lam_init=lam_init, P=P),
        grid_spec=grid_spec,
        out_shape=jax.ShapeDtypeStruct((DB, N_GROUPS, dv), F32),
        compiler_params=_cparams(("arbitrary", "arbitrary")),
        name="decode_b",
    )(page_table, qcol, knew, vnew_g, blast, bnew, rmat, pmask, lamv, g,
      *([kT] * P), *([vrows] * P))


def _dec_m_kernel(q_ref, k_ref, v_ref, o_ref):
    n = k_ref.shape[2] // 8
    q = q_ref[0]
    k3 = k_ref[0, 0].reshape(n, 8, D_M)
    lg = jnp.sum(k3 * q[None], axis=-1, keepdims=True)
    m8 = jnp.max(lg, axis=0)
    m4 = jnp.maximum(m8[:H_M], m8[H_M:])
    p = jnp.exp(lg - jnp.concatenate([m4, m4], axis=0)[None])
    l8 = jnp.sum(p, axis=0)
    o8 = jnp.sum(p * v_ref[0, 0].reshape(n, 8, D_M), axis=0)
    o_ref[0] = (o8[:H_M] + o8[H_M:]) / (l8[:H_M] + l8[H_M:])


def decode_m(layer, q8, mem_k, mem_v):
    DB = q8.shape[0]
    rows = mem_k.shape[2]
    return pl.pallas_call(
        _dec_m_kernel,
        grid=(DB,),
        in_specs=[pl.BlockSpec((1, 8, D_M), lambda b: (b, 0, 0)),
                  pl.BlockSpec((1, 1, rows, D_M), lambda b: (layer, b, 0, 0)),
                  pl.BlockSpec((1, 1, rows, D_M), lambda b: (layer, b, 0, 0))],
        out_specs=pl.BlockSpec((1, H_M, D_M), lambda b: (b, 0, 0)),
        out_shape=jax.ShapeDtypeStruct((DB, H_M, D_M), F32),
        compiler_params=_cparams(("arbitrary",)),
        name="decode_m",
    )(q8, mem_k, mem_v)


def _t5_bucket_np(rel):
    n = np.maximum(rel, 0)
    max_exact = NUM_BUCKETS // 2
    nf = np.maximum(n, 1).astype(np.float32)
    large = max_exact + (np.log(nf / np.float32(max_exact))
                         / np.float32(math.log(MAX_DISTANCE / max_exact))
                         * np.float32(NUM_BUCKETS - max_exact)).astype(np.int32)
    large = np.minimum(large, NUM_BUCKETS - 1)
    return np.where(n < max_exact, n, large).astype(np.int32)


def _far_bucket(first_far, last_far):
    b = _t5_bucket_np(np.arange(first_far, last_far + 1))
    assert (b == b[0]).all(), "relative-position bias must be constant beyond one tile"
    return int(b[0])


def _bias_lookup(rel_bias, buckets):
    onehot = jax.nn.one_hot(jnp.asarray(buckets, jnp.int32), NUM_BUCKETS, dtype=F32)
    return jnp.einsum('...b,bh->...h', onehot, rel_bias, precision=HIGHEST)


def _band_tables(rel_bias, heads, t_max):
    far = _far_bucket(TQ + 1, max(t_max, TQ + 1))
    r = np.arange(TQ)[None, :] - np.arange(TQ)[:, None]
    tabs = []
    for d in range(2):
        rel = r + d * TQ
        vals = _bias_lookup(rel_bias, _t5_bucket_np(rel))[..., heads] - rel_bias[far][heads]
        vals = jnp.where((rel >= 0)[..., None], vals, NEG)
        tabs.append(jnp.transpose(vals, (2, 0, 1)))
    return jnp.stack(tabs, axis=1).astype(F32)


def _split3(x, order):
    hi = x.astype(BF16)
    lo = (x - hi.astype(F32)).astype(BF16)
    return jnp.concatenate([(hi, lo)[o] for o in order], axis=-1)


def kernel(x_prompt, x_sample, cache_a_k, cache_a_v, cache_idx_k, cache_b_k, cache_b_v, cache_mem_k, cache_mem_v, page_table, mem_prompt, w_in, w_mem_kv, lambda_q1, lambda_k1, lambda_q2, lambda_k2, subln_g, w_branch_a, w_branch_b, w_branch_m, w_out, rel_bias, ln1_g, ln1_b, ln2_g, ln2_b, ffn_w_gate, ffn_w_up, ffn_w_down, router_w, expert_w_gate, expert_w_up, expert_w_down):
    B, T, D = x_prompt.shape
    DB, TS, _ = x_sample.shape
    depth = w_in.shape[0]
    n_pool = cache_a_k.shape[1]
    npg = page_table.shape[1]
    past = npg * PAGE
    n_mem = mem_prompt.shape[1]
    assert TS == 1 and T % TQ == 0 and cache_a_k.shape[2] == PAGE
    alpha = (2 * depth) ** 0.25
    topk_p = min(TOPK_MAX, T // 4)
    topk_s = min(TOPK_MAX, (past + TS) // 4)
    W = HEAD_COLS
    nq = T // TQ

    c_ik = 3 * A_WIDTH + H_IDX * D_IDX
    c_b = c_ik + D_IDX + H_IDX
    n_main = 8 * W + 3 * D
    tail_pad = 128 - D_IDX - H_IDX

    low = jnp.asarray(np.tril(np.ones((TQ, TQ), np.float32), -1), BF16)
    band_a = _band_tables(rel_bias, np.arange(H_A), T)
    band_b = _band_tables(rel_bias, H_A + np.arange(H_B), T)

    far_s = _far_bucket(PAGE + 1, past)
    b_last = (_bias_lookup(rel_bias, _t5_bucket_np(PAGE - np.arange(PAGE))) - rel_bias[far_s]).T
    b_new = (rel_bias[0] - rel_bias[far_s])[:, None]
    blast_a, bnew_a = b_last[:H_A], b_new[:H_A]
    blast_b, bnew_b = jnp.repeat(b_last[H_A:], 2, axis=0), jnp.repeat(b_new[H_A:], 2, axis=0)
    lane = np.arange(W)
    rmat = jnp.asarray(lane[None, :] // H_B == np.arange(PAGE)[:, None], BF16)
    pmask = jnp.asarray(lane[None, :] % H_B == np.arange(N_GROUPS)[:, None] // 2, F32)

    idxT = jnp.transpose(cache_idx_k, (0, 1, 3, 2))
    akT = jnp.transpose(cache_a_k, (0, 1, 3, 4, 2)).reshape(depth, n_pool, W, PAGE)
    avT = jnp.transpose(cache_a_v, (0, 1, 3, 4, 2)).reshape(depth, n_pool, W, PAGE)
    bkT = jnp.transpose(cache_b_k, (0, 1, 3, 4, 5, 2)).reshape(depth, n_pool, W, PAGE)
    bv_rows = cache_b_v.reshape(depth, n_pool, PAGE * H_B, 2 * D_B)
    memk_rows = cache_mem_k.reshape(depth, DB, n_mem * H_M, D_M)
    memv_rows = cache_mem_v.reshape(depth, DB, n_mem * H_M, D_M)

    outs = {n: [] for n in ('akp', 'avp', 'ikp', 'bkp', 'bvp', 'mkp', 'mvp',
                            'aks', 'avs', 'iks', 'bks', 'bvs')}
    x_p = x_prompt.reshape(B * T, D)
    x_s = x_sample.reshape(DB, D)
    for l in range(depth):
        lam_init = 0.8 - 0.6 * math.exp(-0.3 * l)
        lamv = jnp.stack([lambda_q1[l], lambda_k1[l], lambda_q2[l], lambda_k2[l]])
        wl = w_in[l]
        w_perm = jnp.concatenate([wl[:, :c_ik], wl[:, c_b:], wl[:, c_ik:c_b],
                                  jnp.zeros((D, tail_pad), F32)], axis=1)
        w_main_bf = w_perm[:, :n_main].astype(BF16)
        w_tail_bf = w_perm[:, n_main:].astype(BF16)
        wa_bf, wb_bf, wm_bf, wo_bf = (w[l].astype(BF16) for w in (w_branch_a, w_branch_b, w_branch_m, w_out))
        lg1, lb1 = ln1_g[l][None], ln1_b[l][None]
        lg2, lb2 = ln2_g[l][None], ln2_b[l][None]
        g_sub = subln_g[l][None]
        j = l // 2
        if l % 2 == 0:
            rw = jnp.zeros((D, 128), BF16)
            wg, wu, wd = ffn_w_gate[j][None], ffn_w_up[j][None], ffn_w_down[j][None]
            routed = False
        else:
            rw = jnp.pad(router_w[j], ((0, 0), (0, 128 - N_EXPERTS))).astype(BF16)
            wg, wu, wd = expert_w_gate[j], expert_w_up[j], expert_w_down[j]
            routed = True

        h = matmul(x_p, w_main_bf, 1024 if B * T % 1024 == 0 else 512, 1024,
                   name="proj_prompt")
        tail = matmul(x_p, w_tail_bf, 512, 128, name="proj_tail")
        h3 = h.reshape(B, T, n_main)
        col = lambda c: h3[:, :, c * W:(c + 1) * W]

        def heads(c, nh, dh, dt):
            return col(c).reshape(B, T, nh, dh).transpose(0, 2, 1, 3).astype(dt)

        def q_t(c, scale):
            return jnp.transpose(col(c) * scale, (0, 2, 1)).astype(BF16)

        def v_t(c):
            return col(c).reshape(B, nq, TQ, W).transpose(0, 1, 3, 2).astype(BF16)

        a_k, a_v, b_k, b_v = col(1), col(2), col(5), col(6)
        i_k = tail[:, :D_IDX].reshape(B, T, D_IDX)
        i_w = tail[:, D_IDX:D_IDX + H_IDX].reshape(B, T, H_IDX)
        iq = col(3).reshape(B, T, H_IDX, D_IDX) * (D_IDX ** -0.5)
        iqT = jnp.transpose(_split3(iq, (0, 0, 1)), (0, 2, 3, 1))
        ik3 = _split3(i_k, (0, 1, 0))
        wT = jnp.transpose(i_w * (H_IDX ** -0.5), (0, 2, 1))
        o_a = attn_a_prompt(q_t(0, D_A ** -0.5), heads(1, H_A, D_A, BF16), v_t(2), iqT, ik3, wT,
                            band_a, low, topk_p)
        o_b = attn_b_prompt(q_t(4, D_B ** -0.5), heads(5, 2 * H_B, D_B, BF16), v_t(6),
                            band_b, lamv, g_sub.T, lam_init)
        kv = matmul(mem_prompt.reshape(B * n_mem, D), w_mem_kv[l].astype(BF16), 512, 1024,
                    name="proj_mem")
        mk = kv[:, :M_WIDTH].reshape(B, n_mem, H_M, D_M)
        mv = kv[:, M_WIDTH:].reshape(B, n_mem, H_M, D_M)
        qm = (heads(7, H_M, D_M, F32) * (D_M ** -0.5)).astype(BF16)
        o_m = attn_m_prompt(qm, mk.transpose(0, 2, 1, 3).astype(BF16),
                            mv.transpose(0, 2, 1, 3).astype(BF16))
        x_p = merge(x_p, o_a.reshape(B * T, W), o_b.reshape(B * T, W), o_m.reshape(B * T, W),
                    h, 8 * W // D, wa_bf, wb_bf, wm_bf, wo_bf, lg1, lb1, alpha, 512, name="merge_prompt")
        x_p_mid = x_p
        if routed:
            xs, tile_expert, n_used, idx_s, w12 = moe_dispatch(x_p_mid, rw, N_EXPERTS)
        outs['akp'].append(a_k.reshape(B, T, H_A, D_A))
        outs['avp'].append(a_v.reshape(B, T, H_A, D_A))
        outs['ikp'].append(i_k)
        outs['bkp'].append(b_k.reshape(B, T, H_B, 2, D_B))
        outs['bvp'].append(b_v.reshape(B, T, H_B, 2 * D_B))
        outs['mkp'].append(mk)
        outs['mvp'].append(mv)

        hs = matmul(x_s, w_main_bf, DB, 1024, name="proj_sample")
        ts = matmul(x_s, w_tail_bf, DB, 128, name="proj_tail_sample")
        sa_q, sa_k, sa_v, s_iq = (hs[:, c * W:(c + 1) * W] for c in range(4))
        sb_q, sb_k, sb_v, sm_q = (hs[:, c * W:(c + 1) * W] for c in range(4, 8))
        s_ik = ts[:, :D_IDX]
        s_iw = ts[:, D_IDX:D_IDX + H_IDX]
        colv = lambda t: t.reshape(DB, -1, 1)
        am, amn = idx_sample(page_table, l, s_iq.reshape(DB, H_IDX, D_IDX), s_iw, s_ik, idxT, topk_s)
        o_a = decode_a(page_table, l, colv(sa_q), am, amn, colv(sa_k), colv(sa_v),
                       blast_a, bnew_a, akT, avT, D_A ** -0.5).reshape(DB, W)
        vnew_g = jnp.repeat(sb_v.reshape(DB, H_B, 2 * D_B), 2, axis=1)
        o_b = decode_b(page_table, l, colv(sb_q), colv(sb_k), vnew_g, blast_b, bnew_b, rmat, pmask,
                       lamv, g_sub, bkT, bv_rows, D_B ** -0.5, lam_init)[:, :H_B].reshape(DB, W)
        q8 = jnp.tile(sm_q.reshape(DB, H_M, D_M) * (D_M ** -0.5), (1, 2, 1))
        o_m = decode_m(l, q8, memk_rows, memv_rows).reshape(DB, W)
        wg_bf, wu_bf, wd_bf = wg.astype(BF16), wu.astype(BF16), wd.astype(BF16)
        if routed:
            o_sorted = moe_gmm(tile_expert, n_used, xs, wg_bf, wu_bf, wd_bf, _ff_tile(wg.shape[2]))
            og = moe_gather(o_sorted, idx_s, B * T)
        x_s = merge(x_s, o_a, o_b, o_m, hs, 8 * W // D, wa_bf, wb_bf, wm_bf, wo_bf, lg1, lb1, alpha,
                    DB, name="merge_sample")
        x_s = ffn(x_s, rw, wg_bf, wu_bf, wd_bf, lg2, lb2, alpha, DB, _ff_tile(wg.shape[2]), routed,
                  name="ffn_sample")
        if routed:
            x_p = moe_combine(x_p_mid, og, w12, lg2, lb2, alpha)
        else:
            x_p = ffn(x_p_mid, rw, wg_bf, wu_bf, wd_bf, lg2, lb2, alpha,
                      512, _ff_tile(wg.shape[2]), False, name="ffn_prompt")
        outs['aks'].append(sa_k.reshape(DB, TS, H_A, D_A))
        outs['avs'].append(sa_v.reshape(DB, TS, H_A, D_A))
        outs['iks'].append(s_ik.reshape(DB, TS, D_IDX))
        outs['bks'].append(sb_k.reshape(DB, TS, H_B, 2, D_B))
        outs['bvs'].append(sb_v.reshape(DB, TS, H_B, 2 * D_B))

    st = {n: jnp.stack(v) for n, v in outs.items()}
    return (x_p.reshape(B, T, D), x_s.reshape(DB, TS, D),
            st['akp'], st['avp'], st['ikp'], st['bkp'], st['bvp'], st['mkp'], st['mvp'],
            st['aks'], st['avs'], st['iks'], st['bks'], st['bvs'])
```

```python
import functools
import math

import jax
import jax.numpy as jnp
import numpy as np
from jax import lax
from jax.experimental import pallas as pl
from jax.experimental.pallas import tpu as pltpu
from jax.experimental.pallas import tpu_sc as plsc

F32 = jnp.float32
BF16 = jnp.bfloat16
HIGHEST = lax.Precision.HIGHEST

H_A, D_A = 8, 64
H_IDX, D_IDX = 8, 64
TOPK_MAX = 256
H_B, D_B = 4, 64
H_M, D_M = 4, 128
NUM_BUCKETS, MAX_DISTANCE = 32, 128
N_EXPERTS = 8
LN_EPS = 1e-5
SUBLN_EPS = 1e-5
A_WIDTH = H_A * D_A
B_WIDTH = H_B * 2 * D_B
M_WIDTH = H_M * D_M
HEAD_COLS = 512
PAGE = 128
N_GROUPS = 8

NEG = -1e30
KEY_NEG_INF = -2139095041
INT_MIN = -2147483648

VMEM_LIMIT = 56 * 1024 * 1024

TQ = 256
IDX_SPLIT = 3 * D_IDX


def _cparams(sem):
    return pltpu.CompilerParams(dimension_semantics=sem, vmem_limit_bytes=VMEM_LIMIT)


def _dot(a, b, precise=False):
    if precise:
        return jnp.dot(a, b, preferred_element_type=F32, precision=HIGHEST)
    return jnp.dot(a, b, preferred_element_type=F32)


def _dot_nt(a, b):
    return lax.dot_general(a, b, (((1,), (1,)), ((), ())), preferred_element_type=F32)


def _orderable(s):
    b = lax.bitcast_convert_type(s, jnp.int32)
    return jnp.where(b < 0, b ^ jnp.int32(0x7FFFFFFF), b)


def _layer_norm(v, g, b):
    mu = jnp.mean(v, axis=-1, keepdims=True)
    d = v - mu
    var = jnp.mean(d * d, axis=-1, keepdims=True)
    return d * lax.rsqrt(var + LN_EPS) * g + b


def _kth_largest(count_ge, shape, topk):
    t0 = jnp.full(shape, INT_MIN, jnp.int32)
    t0 = jnp.where(count_ge(jnp.zeros(shape, jnp.int32)) >= topk, jnp.zeros(shape, jnp.int32), t0)

    def bit_body(b, t):
        cand = t | jnp.left_shift(jnp.int32(1), 30 - b)
        return jnp.where(count_ge(cand) >= topk, cand, t)

    return lax.fori_loop(0, 31, bit_body, t0)


def _lambda(lv, lam_init):
    return (jnp.exp(jnp.sum(lv[0:1] * lv[1:2], axis=-1, keepdims=True))
            - jnp.exp(jnp.sum(lv[2:3] * lv[3:4], axis=-1, keepdims=True)) + lam_init)


def _mm_kernel(x_ref, w_ref, o_ref):
    o_ref[...] = _dot(x_ref[...].astype(BF16), w_ref[...])


def matmul(x, w, tm, tn, name="matmul"):
    M, K = x.shape
    N = w.shape[1]
    assert M % tm == 0 and N % tn == 0
    return pl.pallas_call(
        _mm_kernel,
        grid=(N // tn, M // tm),
        in_specs=[pl.BlockSpec((tm, K), lambda j, i: (i, 0)),
                  pl.BlockSpec((K, tn), lambda j, i: (0, j))],
        out_specs=pl.BlockSpec((tm, tn), lambda j, i: (i, j)),
        out_shape=jax.ShapeDtypeStruct((M, N), F32),
        compiler_params=_cparams(("arbitrary", "arbitrary")),
        name=name,
    )(x, w)


def _set_row(full, g, row):
    r = lax.broadcasted_iota(jnp.int32, full.shape, 0)
    return jnp.where(r == g, row, full)


def _attend_tile(kj, band_d, ml, *, n_groups, dv, qT_ref, k_ref, vT_ref, band_ref, am, acc_ref,
                 s_ref, p_ref, tq):
    m_all, l_all = ml
    ksl = pl.ds(pl.multiple_of(kj * tq, tq), tq)
    gpb = n_groups // band_ref.shape[0]
    gpv = (n_groups * dv) // vT_ref.shape[2]
    dq = qT_ref.shape[1] // n_groups
    m_out, l_out = m_all, l_all
    for g in range(n_groups):
        s_ref[g] = _dot(k_ref[0, g, ksl, :], qT_ref[0, g * dq:(g + 1) * dq, :])
    alphas = []
    for g in range(n_groups):
        s = s_ref[g]
        if am is not None:
            s = s + am
        if band_d is not None:
            s = s + band_ref[g // gpb, band_d]
        m_old = m_all[g:g + 1, :]
        m_new = jnp.maximum(m_old, jnp.max(s, axis=0, keepdims=True))
        alpha = jnp.exp(m_old - m_new)
        p = jnp.exp(s - m_new)
        p_ref[g] = p.astype(BF16)
        alphas.append(alpha)
        m_out = _set_row(m_out, g, m_new)
        l_out = _set_row(l_out, g, alpha * l_all[g:g + 1, :] + jnp.sum(p, axis=0, keepdims=True))
    for g in range(n_groups):
        vrows = slice((g // gpv) * dv, (g // gpv + 1) * dv)
        arows = slice(g * dv, (g + 1) * dv)
        acc_ref[arows, :] = alphas[g] * acc_ref[arows, :] + _dot(vT_ref[0, kj, vrows, :], p_ref[g])
    return m_out, l_out


def _attend_causal(i, tile_fn, n_groups, tq):
    ml = (jnp.full((n_groups, tq), NEG, F32), jnp.zeros((n_groups, tq), F32))
    ml = lax.fori_loop(0, jnp.maximum(i - 1, 0), lambda kj, c: tile_fn(kj, None, c), ml)
    ml = lax.cond(i >= 1, lambda c: tile_fn(i - 1, 1, c), lambda c: c, ml)
    return tile_fn(i, 0, ml)


def _attn_a_kernel(qT_ref, k_ref, vT_ref, iqT_ref, ik_ref, wT_ref, band_ref, low_ref,
                   o_ref, key_ref, am_ref, acc_ref, s_ref, p_ref, *, topk):
    i = pl.program_id(1)
    nk = i + 1
    tq = qT_ref.shape[2]
    qpos = i * tq + lax.broadcasted_iota(jnp.int32, (tq, tq), 1)
    krow = lax.broadcasted_iota(jnp.int32, (tq, tq), 0)

    def score_body(kj, c):
        ikt = ik_ref[0, pl.ds(pl.multiple_of(kj * tq, tq), tq), :]
        s = jnp.zeros((tq, tq), F32)
        for h in range(H_IDX):
            s = s + wT_ref[0, h:h + 1, :] * jnp.maximum(_dot(ikt, iqT_ref[0, h]), 0.0)
        s = jnp.where(kj * tq + krow <= qpos, s, -jnp.inf)
        key_ref[kj] = _orderable(s)
        return c

    lax.fori_loop(0, nk, score_body, 0)

    def count(pred):
        def tile_count(kj):
            t = jnp.where(pred(key_ref[kj]), 1.0, 0.0)
            return jnp.sum(t.reshape(tq // 8, 8, tq), axis=0)

        zero = jnp.zeros((8, tq), F32)
        c = lax.fori_loop(0, jnp.right_shift(nk, 1),
                          lambda j, c: c + tile_count(2 * j) + tile_count(2 * j + 1), zero)
        c = c + lax.cond((nk & 1) == 1, lambda: tile_count(nk - 1), lambda: zero)
        return jnp.sum(c, axis=0, keepdims=True)

    thr = _kth_largest(lambda cand: count(lambda s: s >= cand), (1, tq), topk)
    c_gt = count(lambda s: s > thr)
    need = jnp.where(thr > KEY_NEG_INF, topk - c_gt, 0.0)

    def mask_body(kj, run):
        s = key_ref[kj]
        tie = s == thr
        tie_f = jnp.where(tie, 1.0, 0.0)
        pre = _dot(low_ref[...], tie_f.astype(BF16)) + run
        keep_tie = jnp.where(pre < need, 0.0, NEG)
        am_ref[kj] = jnp.where(s > thr, 0.0, jnp.where(tie, keep_tie, NEG))
        return run + jnp.sum(tie_f, axis=0, keepdims=True)

    lax.fori_loop(0, nk, mask_body, jnp.zeros((1, tq), F32))

    acc_ref[...] = jnp.zeros_like(acc_ref)

    def tile_fn(kj, band_d, ml):
        return _attend_tile(kj, band_d, ml, n_groups=H_A, dv=D_A, qT_ref=qT_ref, k_ref=k_ref,
                            vT_ref=vT_ref, band_ref=band_ref, am=am_ref[kj], acc_ref=acc_ref,
                            s_ref=s_ref, p_ref=p_ref, tq=tq)

    _, l_all = _attend_causal(i, tile_fn, H_A, tq)
    inv = 1.0 / l_all
    oT = jnp.concatenate([acc_ref[h * D_A:(h + 1) * D_A, :] * inv[h:h + 1, :] for h in range(H_A)],
                         axis=0)
    o_ref[0] = oT.T


def attn_a_prompt(qT, k, vT, iqT, ik, wT, band, low, topk):
    B, H, T, D = k.shape
    nq = T // TQ
    W = H * D
    return pl.pallas_call(
        functools.partial(_attn_a_kernel, topk=float(topk)),
        grid=(B, nq),
        in_specs=[
            pl.BlockSpec((1, W, TQ), lambda b, i: (b, 0, i)),
            pl.BlockSpec((1, H, T, D), lambda b, i: (b, 0, 0, 0)),
            pl.BlockSpec((1, nq, W, TQ), lambda b, i: (b, 0, 0, 0)),
            pl.BlockSpec((1, H_IDX, IDX_SPLIT, TQ), lambda b, i: (b, 0, 0, i)),
            pl.BlockSpec((1, T, IDX_SPLIT), lambda b, i: (b, 0, 0)),
            pl.BlockSpec((1, H_IDX, TQ), lambda b, i: (b, 0, i)),
            pl.BlockSpec((H, 2, TQ, TQ), lambda b, i: (0, 0, 0, 0)),
            pl.BlockSpec((TQ, TQ), lambda b, i: (0, 0)),
        ],
        out_specs=pl.BlockSpec((1, TQ, W), lambda b, i: (b, i, 0)),
        out_shape=jax.ShapeDtypeStruct((B, T, W), F32),
        scratch_shapes=[pltpu.VMEM((nq, TQ, TQ), jnp.int32),
                        pltpu.VMEM((nq, TQ, TQ), F32),
                        pltpu.VMEM((W, TQ), F32),
                        pltpu.VMEM((H, TQ, TQ), F32), pltpu.VMEM((H, TQ, TQ), BF16)],
        compiler_params=_cparams(("arbitrary", "arbitrary")),
        name="attn_a_prompt",
    )(qT, k, vT, iqT, ik, wT, band, low)


def _attn_b_kernel(qT_ref, k_ref, vT_ref, band_ref, lam_ref, g_ref, o_ref,
                   acc_ref, s_ref, p_ref, *, lam_init):
    i = pl.program_id(1)
    tq = qT_ref.shape[2]
    dv = 2 * D_B
    acc_ref[...] = jnp.zeros_like(acc_ref)

    def tile_fn(kj, band_d, ml):
        return _attend_tile(kj, band_d, ml, n_groups=2 * H_B, dv=dv, qT_ref=qT_ref, k_ref=k_ref,
                            vT_ref=vT_ref, band_ref=band_ref, am=None, acc_ref=acc_ref,
                            s_ref=s_ref, p_ref=p_ref, tq=tq)

    _, l_all = _attend_causal(i, tile_fn, 2 * H_B, tq)
    lam = _lambda(lam_ref[...], lam_init)
    inv = 1.0 / l_all
    parts = []
    for h in range(H_B):
        g0, g1 = 2 * h, 2 * h + 1
        o = (acc_ref[g0 * dv:(g0 + 1) * dv, :] * inv[g0:g0 + 1, :]
             - lam * (acc_ref[g1 * dv:(g1 + 1) * dv, :] * inv[g1:g1 + 1, :]))
        o = o * lax.rsqrt(jnp.mean(o * o, axis=0, keepdims=True) + SUBLN_EPS)
        parts.append(o * g_ref[...] * (1.0 - lam_init))
    o_ref[0] = jnp.concatenate(parts, axis=0).T


def attn_b_prompt(qT, k, vT, band, lamv, gcol, lam_init):
    B, G, T, D = k.shape
    nq = T // TQ
    W = G * D
    return pl.pallas_call(
        functools.partial(_attn_b_kernel, lam_init=lam_init),
        grid=(B, nq),
        in_specs=[
            pl.BlockSpec((1, W, TQ), lambda b, i: (b, 0, i)),
            pl.BlockSpec((1, G, T, D), lambda b, i: (b, 0, 0, 0)),
            pl.BlockSpec((1, nq, W, TQ), lambda b, i: (b, 0, 0, 0)),
            pl.BlockSpec((H_B, 2, TQ, TQ), lambda b, i: (0, 0, 0, 0)),
            pl.BlockSpec((4, D_B), lambda b, i: (0, 0)),
            pl.BlockSpec((2 * D_B, 1), lambda b, i: (0, 0)),
        ],
        out_specs=pl.BlockSpec((1, TQ, W), lambda b, i: (b, i, 0)),
        out_shape=jax.ShapeDtypeStruct((B, T, W), F32),
        scratch_shapes=[pltpu.VMEM((G * 2 * D_B, TQ), F32),
                        pltpu.VMEM((G, TQ, TQ), F32), pltpu.VMEM((G, TQ, TQ), BF16)],
        compiler_params=_cparams(("arbitrary", "arbitrary")),
        name="attn_b_prompt",
    )(qT, k, vT, band, lamv, gcol)


def _attn_m_kernel(q_ref, k_ref, v_ref, o_ref):
    for h in range(H_M):
        s = _dot_nt(q_ref[0, h], k_ref[0, h])
        m = jnp.max(s, axis=-1, keepdims=True)
        p = jnp.exp(s - m)
        l = jnp.sum(p, axis=-1, keepdims=True)
        o_ref[0, :, h * D_M:(h + 1) * D_M] = _dot(p.astype(BF16), v_ref[0, h]) / l


def attn_m_prompt(q, k, v):
    B, H, T, D = q.shape
    NM = k.shape[2]
    tq = 512
    return pl.pallas_call(
        _attn_m_kernel,
        grid=(B, T // tq),
        in_specs=[
            pl.BlockSpec((1, H, tq, D), lambda b, i: (b, 0, i, 0)),
            pl.BlockSpec((1, H, NM, D), lambda b, i: (b, 0, 0, 0)),
            pl.BlockSpec((1, H, NM, D), lambda b, i: (b, 0, 0, 0)),
        ],
        out_specs=pl.BlockSpec((1, tq, M_WIDTH), lambda b, i: (b, i, 0)),
        out_shape=jax.ShapeDtypeStruct((B, T, M_WIDTH), F32),
        compiler_params=_cparams(("arbitrary", "arbitrary")),
        name="attn_m_prompt",
    )(q, k, v)


def _merge_kernel(x_ref, oa_ref, ob_ref, om_ref, g0_ref, g1_ref, g2_ref,
                  wa_ref, wb_ref, wm_ref, wo_ref, lg_ref, lb_ref, o_ref, *, alpha):
    ya = _dot(oa_ref[...].astype(BF16), wa_ref[...])
    yb = _dot(ob_ref[...].astype(BF16), wb_ref[...])
    ym = _dot(om_ref[...].astype(BF16), wm_ref[...])
    hmix = (jax.nn.sigmoid(g0_ref[...]) * ya + jax.nn.sigmoid(g1_ref[...]) * yb
            + jax.nn.sigmoid(g2_ref[...]) * ym)
    mix = _dot(hmix.astype(BF16), wo_ref[...])
    o_ref[...] = _layer_norm(alpha * x_ref[...] + mix, lg_ref[...], lb_ref[...])


def merge(x, oa, ob, om, hproj, gate_col0, wa, wb, wm, wo, lg, lb, alpha, tm, name="merge"):
    M, D = x.shape
    row = lambda i: (i, 0)
    const = lambda i: (0, 0)
    gspec = [pl.BlockSpec((tm, D), (lambda i, c=c: (i, gate_col0 + c))) for c in range(3)]
    return pl.pallas_call(
        functools.partial(_merge_kernel, alpha=alpha),
        grid=(M // tm,),
        in_specs=[pl.BlockSpec((tm, D), row),
                  pl.BlockSpec((tm, HEAD_COLS), row), pl.BlockSpec((tm, HEAD_COLS), row),
                  pl.BlockSpec((tm, HEAD_COLS), row)] + gspec + [
                  pl.BlockSpec((HEAD_COLS, D), const), pl.BlockSpec((HEAD_COLS, D), const),
                  pl.BlockSpec((HEAD_COLS, D), const), pl.BlockSpec((D, D), const),
                  pl.BlockSpec((1, D), const), pl.BlockSpec((1, D), const)],
        out_specs=pl.BlockSpec((tm, D), row),
        out_shape=jax.ShapeDtypeStruct((M, D), F32),
        compiler_params=_cparams(("arbitrary",)),
        name=name,
    )(x, oa, ob, om, hproj, hproj, hproj, wa, wb, wm, wo, lg, lb)


def _ffn_kernel(x_ref, rw_ref, wg_ref, wu_ref, wd_ref, lg_ref, lb_ref, o_ref,
                acc_ref, gate_ref, *, alpha, routed, n_experts):
    e = pl.program_id(1)
    f = pl.program_id(2)
    first = jnp.logical_and(e == 0, f == 0)
    last = jnp.logical_and(e == pl.num_programs(1) - 1, f == pl.num_programs(2) - 1)
    x = x_ref[...]

    @pl.when(first)
    def _():
        acc_ref[...] = jnp.zeros_like(acc_ref)
        if routed:
            logits = _dot(x.astype(BF16), rw_ref[...])
            i1, i2, w1, w2 = _top2(logits, n_experts)
            lane = lax.broadcasted_iota(jnp.int32, logits.shape, 1)
            gate_ref[...] = jnp.where(lane == i1, w1, 0.0) + jnp.where(lane == i2, w2, 0.0)

    xin = x.astype(BF16)
    hg = _dot(xin, wg_ref[0])
    hu = _dot(xin, wu_ref[0])
    hh = hg * jax.nn.sigmoid(hg) * hu
    part = _dot(hh.astype(BF16), wd_ref[0])
    if routed:
        gate = gate_ref[...]
        lane = lax.broadcasted_iota(jnp.int32, gate.shape, 1)
        part = part * jnp.sum(jnp.where(lane == e, gate, 0.0), axis=-1, keepdims=True)
    acc_ref[...] += part

    @pl.when(last)
    def _():
        o_ref[...] = _layer_norm(alpha * x + acc_ref[...], lg_ref[...], lb_ref[...])


def ffn(x, rw, wg, wu, wd, lg, lb, alpha, tm, tf, routed, name="ffn"):
    M, D = x.shape
    E, _, F = wg.shape
    assert M % tm == 0 and F % tf == 0
    return pl.pallas_call(
        functools.partial(_ffn_kernel, alpha=alpha, routed=routed, n_experts=E),
        grid=(M // tm, E, F // tf),
        in_specs=[pl.BlockSpec((tm, D), lambda i, e, f: (i, 0)),
                  pl.BlockSpec(rw.shape, lambda i, e, f: (0, 0)),
                  pl.BlockSpec((1, D, tf), lambda i, e, f: (e, 0, f)),
                  pl.BlockSpec((1, D, tf), lambda i, e, f: (e, 0, f)),
                  pl.BlockSpec((1, tf, D), lambda i, e, f: (e, f, 0)),
                  pl.BlockSpec((1, D), lambda i, e, f: (0, 0)),
                  pl.BlockSpec((1, D), lambda i, e, f: (0, 0))],
        out_specs=pl.BlockSpec((tm, D), lambda i, e, f: (i, 0)),
        out_shape=jax.ShapeDtypeStruct((M, D), F32),
        scratch_shapes=[pltpu.VMEM((tm, D), F32), pltpu.VMEM((tm, rw.shape[1]), F32)],
        compiler_params=_cparams(("arbitrary", "arbitrary", "arbitrary")),
        name=name,
    )(x, rw, wg, wu, wd, lg, lb)


MOE_ROW_TILE = 512
FF_TILE_MAX = 1408


def _ff_tile(d_ff):
    return max(t for t in range(128, min(d_ff, FF_TILE_MAX) + 1, 128) if d_ff % t == 0)
SC_WINDOW = 128
SC_ROW = 128


def _top2(logits, n_experts):
    ne = logits.shape[-1]
    lane = lax.broadcasted_iota(jnp.int32, logits.shape, 1)
    logits = jnp.where(lane < n_experts, logits, -jnp.inf)
    m1 = jnp.max(logits, axis=-1, keepdims=True)
    i1 = jnp.min(jnp.where(logits == m1, lane, ne), axis=-1, keepdims=True)
    rest = jnp.where(lane == i1, -jnp.inf, logits)
    m2 = jnp.max(rest, axis=-1, keepdims=True)
    i2 = jnp.min(jnp.where(rest == m2, lane, ne), axis=-1, keepdims=True)
    e2 = jnp.exp(m2 - m1)
    return i1, i2, 1.0 / (1.0 + e2), e2 / (1.0 + e2)


def _route_kernel(x_ref, rw_ref, low_ref, e_ref, r_ref, w_ref, cnt_ref, run_ref, *, n_experts):
    @pl.when(pl.program_id(0) == 0)
    def _():
        run_ref[...] = jnp.zeros_like(run_ref)

    logits = _dot(x_ref[...].astype(BF16), rw_ref[...])
    i1, i2, w1, w2 = _top2(logits, n_experts)
    lane = lax.broadcasted_iota(jnp.int32, logits.shape, 1)
    oh1 = jnp.where(lane == i1, 1.0, 0.0)
    oh2 = jnp.where(lane == i2, 1.0, 0.0)
    oh = oh1 + oh2
    pre = _dot(low_ref[...], oh.astype(BF16)) + run_ref[...]
    r1 = jnp.sum(oh1 * pre, axis=-1, keepdims=True)
    r2 = jnp.sum(oh2 * pre, axis=-1, keepdims=True)
    run_ref[...] = run_ref[...] + jnp.sum(oh, axis=0, keepdims=True)
    first = lax.broadcasted_iota(jnp.int32, e_ref.shape, 1) == 0
    e_ref[...] = jnp.where(first, i1, i2)
    r_ref[...] = jnp.where(first, r1, r2).astype(jnp.int32)
    w_ref[...] = jnp.where(first, w1, w2)
    cnt_ref[...] = run_ref[...]


def moe_route(x, rw, n_experts, tm=512):
    M, D = x.shape
    low = jnp.asarray(np.tril(np.ones((tm, tm), np.float32), -1), BF16)
    pair = pl.BlockSpec((tm, 2), lambda i: (i, 0))
    return pl.pallas_call(
        functools.partial(_route_kernel, n_experts=n_experts),
        grid=(M // tm,),
        in_specs=[pl.BlockSpec((tm, D), lambda i: (i, 0)),
                  pl.BlockSpec(rw.shape, lambda i: (0, 0)),
                  pl.BlockSpec((tm, tm), lambda i: (0, 0))],
        out_specs=[pair, pair, pair, pl.BlockSpec((1, 128), lambda i: (0, 0))],
        out_shape=[jax.ShapeDtypeStruct((M, 2), jnp.int32), jax.ShapeDtypeStruct((M, 2), jnp.int32),
                   jax.ShapeDtypeStruct((M, 2), F32), jax.ShapeDtypeStruct((1, 128), F32)],
        scratch_shapes=[pltpu.VMEM((1, 128), F32)],
        compiler_params=_cparams(("arbitrary",)),
        name="moe_route",
    )(x, rw, low)


def _gmm_kernel(te_ref, nu_ref, x_ref, wg_ref, wu_ref, wd_ref, o_ref, acc_ref):
    i = pl.program_id(0)
    f = pl.program_id(1)

    @pl.when(i < nu_ref[0])
    def _():
        @pl.when(f == 0)
        def _():
            acc_ref[...] = jnp.zeros_like(acc_ref)

        x = x_ref[...].astype(BF16)
        hg = _dot(x, wg_ref[0])
        hu = _dot(x, wu_ref[0])
        hh = hg * jax.nn.sigmoid(hg) * hu
        acc_ref[...] += _dot(hh.astype(BF16), wd_ref[0])

        @pl.when(f == pl.num_programs(1) - 1)
        def _():
            o_ref[...] = acc_ref[...]


def moe_gmm(tile_expert, n_used, xs, wg, wu, wd, tf):
    R, D = xs.shape
    F = wg.shape[2]
    tr = MOE_ROW_TILE
    grid_spec = pltpu.PrefetchScalarGridSpec(
        num_scalar_prefetch=2,
        grid=(R // tr, F // tf),
        in_specs=[pl.BlockSpec((tr, D), lambda i, f, te, nu: (i, 0)),
                  pl.BlockSpec((1, D, tf), lambda i, f, te, nu: (te[i], 0, f)),
                  pl.BlockSpec((1, D, tf), lambda i, f, te, nu: (te[i], 0, f)),
                  pl.BlockSpec((1, tf, D), lambda i, f, te, nu: (te[i], f, 0))],
        out_specs=pl.BlockSpec((tr, D), lambda i, f, te, nu: (i, 0)),
        scratch_shapes=[pltpu.VMEM((tr, D), F32)],
    )
    return pl.pallas_call(
        _gmm_kernel,
        grid_spec=grid_spec,
        out_shape=jax.ShapeDtypeStruct((R, D), F32),
        compiler_params=_cparams(("arbitrary", "arbitrary")),
        name="moe_gmm",
    )(tile_expert, n_used, xs, wg, wu, wd)


def _combine_kernel(x_ref, o1_ref, o2_ref, w_ref, lg_ref, lb_ref, o_ref, *, alpha):
    w = w_ref[...]
    y = w[:, 0:1] * o1_ref[...] + w[:, 1:2] * o2_ref[...]
    o_ref[...] = _layer_norm(alpha * x_ref[...] + y, lg_ref[...], lb_ref[...])


def moe_combine(x, og, w12, lg, lb, alpha, tm=512):
    M, D = x.shape
    nb = M // tm
    return pl.pallas_call(
        functools.partial(_combine_kernel, alpha=alpha),
        grid=(nb,),
        in_specs=[pl.BlockSpec((tm, D), lambda i: (i, 0)),
                  pl.BlockSpec((tm, D), lambda i: (i, 0)),
                  pl.BlockSpec((tm, D), lambda i: (i + nb, 0)),
                  pl.BlockSpec((tm, 2), lambda i: (i, 0)),
                  pl.BlockSpec((1, D), lambda i: (0, 0)),
                  pl.BlockSpec((1, D), lambda i: (0, 0))],
        out_specs=pl.BlockSpec((tm, D), lambda i: (i, 0)),
        out_shape=jax.ShapeDtypeStruct((M, D), F32),
        compiler_params=_cparams(("arbitrary",)),
        name="moe_combine",
    )(x, og, og, w12, lg, lb)


def _sc_mesh():
    return plsc.VectorSubcoreMesh(core_axis_name="c", subcore_axis_name="s")


def sc_scatter_rows(src, idx, n_out_rows):
    S = src.shape[0]
    M = idx.shape[0]
    nb = S // SC_WINDOW
    assert S % SC_WINDOW == 0 and M % S == 0

    @pl.kernel(out_type=jax.ShapeDtypeStruct((n_out_rows, SC_ROW), src.dtype), mesh=_sc_mesh(),
               scratch_types=[])
    def scatter_kernel(x_hbm, i_hbm, o_hbm):
        def body(x_vmem, i_vmem):
            pltpu.sync_copy(x_vmem, o_hbm.at[i_vmem.at[0]])

        pltpu.emit_pipeline(
            body,
            grid=(M // SC_WINDOW,),
            in_specs=[pl.BlockSpec((SC_WINDOW, SC_ROW), lambda i: (lax.rem(i, nb), 0)),
                      pl.BlockSpec((1, SC_WINDOW), lambda i: (0, i))],
            out_specs=[],
            core_axis_name=("c", "s"),
            dimension_semantics=(pltpu.PARALLEL,),
        )(x_hbm, i_hbm)

    return scatter_kernel(src, idx.reshape(1, M))


def sc_gather_rows(table, idx):
    M = idx.shape[0]
    assert M % SC_WINDOW == 0

    @pl.kernel(out_type=jax.ShapeDtypeStruct((M, SC_ROW), table.dtype), mesh=_sc_mesh(),
               scratch_types=[])
    def gather_kernel(t_hbm, i_hbm, o_hbm):
        def body(i_vmem, o_vmem):
            pltpu.sync_copy(t_hbm.at[i_vmem.at[0]], o_vmem)

        pltpu.emit_pipeline(
            body,
            grid=(M // SC_WINDOW,),
            in_specs=[pl.BlockSpec((1, SC_WINDOW), lambda i: (0, i))],
            out_specs=[pl.BlockSpec((SC_WINDOW, SC_ROW), lambda i: (i, 0))],
            core_axis_name=("c", "s"),
            dimension_semantics=(pltpu.PARALLEL,),
        )(i_hbm, o_hbm)

    return gather_kernel(table, idx.reshape(1, M))


def moe_dispatch(x, rw, n_experts):
    N, D = x.shape
    tr = MOE_ROW_TILE
    e12, r12, w12, counts = moe_route(x, rw, n_experts)
    cnt = counts[0, :n_experts].astype(jnp.int32)
    cnt_pad = (cnt + tr - 1) // tr * tr
    off_end = jnp.cumsum(cnt_pad)
    off = off_end - cnt_pad
    R = 2 * N + n_experts * tr
    eids = jnp.arange(n_experts, dtype=jnp.int32)
    pos = jnp.sum(jnp.where(e12[..., None] == eids, off, 0), axis=-1) + r12
    tile_start = jnp.arange(R // tr, dtype=jnp.int32) * tr
    tile_expert = jnp.minimum(jnp.sum(tile_start[:, None] >= off_end[None, :], axis=1),
                              n_experts - 1).astype(jnp.int32)
    n_used = (off_end[-1:] // tr).astype(jnp.int32)
    pos_flat = pos.T.reshape(2 * N)
    idx_s = _chunk_index(pos_flat, D)
    xs = _from_chunks(sc_scatter_rows(_as_chunks(x), idx_s, R * (D // SC_ROW)), R, D)
    return xs, tile_expert, n_used, idx_s, w12


def moe_gather(o_sorted, idx_s, n_tokens):
    return _from_chunks(sc_gather_rows(_as_chunks(o_sorted), idx_s), 2 * n_tokens, o_sorted.shape[1])


def _as_chunks(a):
    rows, d = a.shape
    return a.reshape(rows // 8, 8, d // SC_ROW, SC_ROW).transpose(0, 2, 1, 3).reshape(-1, SC_ROW)


def _from_chunks(c, rows, d):
    return c.reshape(rows // 8, d // SC_ROW, 8, SC_ROW).transpose(0, 2, 1, 3).reshape(rows, d)


def _chunk_index(row_of, d):
    nk = d // SC_ROW
    r = row_of.reshape(-1, 1, 8)
    k = jnp.arange(nk, dtype=jnp.int32).reshape(1, nk, 1)
    return ((r // 8) * (8 * nk) + k * 8 + r % 8).reshape(-1)


IDX_PAGES_PER_STEP = 32
ATTN_PAGES_PER_STEP = 32


def _page_specs(layer, n, rows, P):
    return [pl.BlockSpec((1, 1, rows, PAGE), (lambda b, s, pt, j=j: (layer, pt[b, s * P + j], 0, 0)))
            for j in range(n)]


def _idx_score_kernel(pt_ref, iqT_ref, iq3_ref, iw_ref, iwc_ref, iknew_ref, *refs, P):
    pages = refs[:P]
    key_ref, knew_ref = refs[P:]
    step = pl.program_id(1)
    wic = iwc_ref[0] * (H_IDX ** -0.5)
    iq3 = iq3_ref[0]
    for j in range(P):
        kt = pages[j][0, 0]
        hi = kt.astype(BF16)
        lo = (kt - hi.astype(F32)).astype(BF16)
        d = _dot(iq3, jnp.concatenate([hi, lo, hi], axis=0))
        sc = jnp.sum(wic * jnp.maximum(d, 0.0), axis=0, keepdims=True)
        key_ref[0, pl.ds(step * P + j, 1), :] = _orderable(sc + 0.0)

    @pl.when(step == pl.num_programs(1) - 1)
    def _():
        iqT = iqT_ref[0] * (D_IDX ** -0.5)
        wi = iw_ref[0] * (H_IDX ** -0.5)
        dn = jnp.sum(iqT * iknew_ref[0], axis=0, keepdims=True)
        sn = jnp.sum(wi * jnp.maximum(dn, 0.0), axis=-1, keepdims=True) + 0.0
        knew_ref[0] = jnp.broadcast_to(_orderable(sn), knew_ref.shape[1:])


def _idx_select_kernel(key_ref, knew_ref, tri_ref, low_ref, am_ref, amnew_ref, *, topk, n_samples):
    s = key_ref[...]
    npg = s.shape[0] // n_samples
    kn = knew_ref[...][:, 0:1]

    def per_sample(x):
        return jnp.sum(jnp.sum(x.reshape(n_samples, npg, x.shape[-1]), axis=1), axis=-1, keepdims=True)

    def count_ge(cand):
        t = jnp.where(s >= _expand_rows(cand, npg), 1.0, 0.0)
        return per_sample(t) + jnp.where(kn >= cand, 1.0, 0.0)

    thr = _kth_largest(count_ge, (n_samples, 1), topk)
    thr_r = _expand_rows(thr, npg)
    c_gt = per_sample(jnp.where(s > thr_r, 1.0, 0.0)) + jnp.where(kn > thr, 1.0, 0.0)
    need = topk - c_gt
    tie = s == thr_r
    tie_f = jnp.where(tie, 1.0, 0.0)
    in_row = _dot(tie_f.astype(BF16), tri_ref[...])
    row_tot = jnp.broadcast_to(jnp.sum(tie_f, axis=-1, keepdims=True), tie_f.shape)
    before = jnp.einsum('bpq,bql->bpl', low_ref[...],
                        row_tot.astype(BF16).reshape(n_samples, npg, PAGE),
                        preferred_element_type=F32).reshape(s.shape)
    keep_tie = jnp.where(in_row + before < _expand_rows(need, npg), 0.0, NEG)
    am_ref[...] = jnp.where(s > thr_r, 0.0, jnp.where(tie, keep_tie, NEG))
    keep_new = jnp.where(per_sample(tie_f) < need, 0.0, NEG)
    amn = jnp.where(kn > thr, 0.0, jnp.where(kn == thr, keep_new, NEG))
    amnew_ref[...] = jnp.broadcast_to(amn, amnew_ref.shape)


def idx_sample(page_table, layer, iq, iw, ik_new, poolT, topk):
    DB, npg = page_table.shape
    iqT = jnp.transpose(iq, (0, 2, 1))
    iq3 = _split3(iq * (D_IDX ** -0.5), (0, 0, 1))
    iwc = iw.reshape(DB, H_IDX, 1)
    iw = iw.reshape(DB, 1, H_IDX)
    ik_new = ik_new.reshape(DB, D_IDX, 1)
    P = min(IDX_PAGES_PER_STEP, npg)
    assert npg % P == 0
    per_b = lambda b, s, pt: (b, 0, 0)
    grid_spec = pltpu.PrefetchScalarGridSpec(
        num_scalar_prefetch=1,
        grid=(DB, npg // P),
        in_specs=[pl.BlockSpec((1, D_IDX, H_IDX), per_b),
                  pl.BlockSpec((1, H_IDX, IDX_SPLIT), per_b),
                  pl.BlockSpec((1, 1, H_IDX), per_b),
                  pl.BlockSpec((1, H_IDX, 1), per_b),
                  pl.BlockSpec((1, D_IDX, 1), per_b)]
                 + _page_specs(layer, P, D_IDX, P),
        out_specs=[pl.BlockSpec((1, npg, PAGE), per_b),
                   pl.BlockSpec((1, 1, PAGE), per_b)],
    )
    keys, knew = pl.pallas_call(
        functools.partial(_idx_score_kernel, P=P),
        grid_spec=grid_spec,
        out_shape=[jax.ShapeDtypeStruct((DB, npg, PAGE), jnp.int32),
                   jax.ShapeDtypeStruct((DB, 1, PAGE), jnp.int32)],
        compiler_params=_cparams(("arbitrary", "arbitrary")),
        name="idx_score",
    )(page_table, iqT, iq3, iw, iwc, ik_new, *([poolT] * P))
    tri = jnp.asarray(np.triu(np.ones((PAGE, PAGE), np.float32), 1), BF16)
    low = jnp.asarray(np.broadcast_to(np.tril(np.ones((npg, npg), np.float32), -1), (DB, npg, npg)), BF16)
    whole = lambda shape: pl.BlockSpec(shape, lambda i: (0,) * len(shape))
    am, amn = pl.pallas_call(
        functools.partial(_idx_select_kernel, topk=float(topk), n_samples=DB),
        grid=(1,),
        in_specs=[whole((DB * npg, PAGE)), whole((DB, PAGE)), whole((PAGE, PAGE)), whole((DB, npg, npg))],
        out_specs=[whole((DB * npg, PAGE)), whole((DB, PAGE))],
        out_shape=[jax.ShapeDtypeStruct((DB * npg, PAGE), F32), jax.ShapeDtypeStruct((DB, PAGE), F32)],
        compiler_params=_cparams(("arbitrary",)),
        name="idx_select",
    )(keys.reshape(DB * npg, PAGE), knew.reshape(DB, PAGE), tri, low)
    return am.reshape(DB, npg, PAGE), amn.reshape(DB, 1, PAGE)


def _expand_rows(x, rep):
    g, n = x.shape
    return jnp.broadcast_to(x[:, None, :], (g, rep, n)).reshape(g * rep, n)


def _group_logits(kt, qb):
    w, n = kt.shape
    return jnp.sum((kt * qb).reshape(N_GROUPS, w // N_GROUPS, n), axis=1)


def _dec_a_kernel(pt_ref, q_ref, am_ref, amnew_ref, knew_ref, vnew_ref, blast_ref, bnew_ref,
                  *refs, scale, P):
    kp = refs[:P]
    vp = refs[P:2 * P]
    o_ref, qb_ref, m_ref, l_ref, acc_ref = refs[2 * P:]
    step = pl.program_id(1)
    is_last = step == pl.num_programs(1) - 1
    W = qb_ref.shape[0]
    rep = W // N_GROUPS

    @pl.when(step == 0)
    def _():
        qb_ref[...] = jnp.broadcast_to(q_ref[0] * scale, qb_ref.shape)
        m_ref[...] = jnp.full_like(m_ref, NEG)
        l_ref[...] = jnp.zeros_like(l_ref)
        acc_ref[...] = jnp.zeros_like(acc_ref)

    last_f = jnp.where(is_last, 1.0, 0.0)
    for j in range(P):
        lg = _group_logits(kp[j][0, 0], qb_ref[...]) + am_ref[0, j:j + 1, :]
        if j == P - 1:
            lg = lg + last_f * blast_ref[...]
        m_old = m_ref[...]
        m_new = jnp.maximum(m_old, lg)
        alpha = jnp.exp(m_old - m_new)
        p = jnp.exp(lg - m_new)
        m_ref[...] = m_new
        l_ref[...] = l_ref[...] * alpha + p
        acc_ref[...] = acc_ref[...] * _expand_rows(alpha, rep) + _expand_rows(p, rep) * vp[j][0, 0]

    @pl.when(is_last)
    def _():
        q = q_ref[0] * scale
        lgn = (jnp.sum((knew_ref[0] * q).reshape(N_GROUPS, rep, 1), axis=1)
               + bnew_ref[...] + amnew_ref[0][:, 0:1])
        m = m_ref[...]
        mx = jnp.maximum(jnp.max(m, axis=-1, keepdims=True), lgn)
        w = jnp.exp(m - mx)
        pn = jnp.exp(lgn - mx)
        l = jnp.sum(l_ref[...] * w, axis=-1, keepdims=True) + pn
        o = (jnp.sum(acc_ref[...] * _expand_rows(w, rep), axis=-1, keepdims=True)
             + _expand_rows(pn, rep) * vnew_ref[0])
        o_ref[0] = o / _expand_rows(l, rep)


def decode_a(page_table, layer, qcol, am, amnew, knew, vnew, blast, bnew, kT, vT, scale):
    DB, npg = page_table.shape
    W = qcol.shape[1]
    P = min(ATTN_PAGES_PER_STEP, npg)
    assert npg % P == 0
    per_b = lambda b, s, pt: (b, 0, 0)
    c2 = lambda b, s, pt: (0, 0)
    grid_spec = pltpu.PrefetchScalarGridSpec(
        num_scalar_prefetch=1,
        grid=(DB, npg // P),
        in_specs=[pl.BlockSpec((1, W, 1), per_b),
                  pl.BlockSpec((1, P, PAGE), lambda b, s, pt: (b, s, 0)),
                  pl.BlockSpec((1, 1, PAGE), per_b),
                  pl.BlockSpec((1, W, 1), per_b),
                  pl.BlockSpec((1, W, 1), per_b),
                  pl.BlockSpec((N_GROUPS, PAGE), c2),
                  pl.BlockSpec((N_GROUPS, 1), c2)]
                 + _page_specs(layer, P, W, P) + _page_specs(layer, P, W, P),
        out_specs=pl.BlockSpec((1, W, 1), per_b),
        scratch_shapes=[pltpu.VMEM((W, PAGE), F32), pltpu.VMEM((N_GROUPS, PAGE), F32),
                        pltpu.VMEM((N_GROUPS, PAGE), F32), pltpu.VMEM((W, PAGE), F32)],
    )
    return pl.pallas_call(
        functools.partial(_dec_a_kernel, scale=scale, P=P),
        grid_spec=grid_spec,
        out_shape=jax.ShapeDtypeStruct((DB, W, 1), F32),
        compiler_params=_cparams(("arbitrary", "arbitrary")),
        name="decode_a",
    )(page_table, qcol, am, amnew, knew, vnew, blast, bnew, *([kT] * P), *([vT] * P))


def _dec_b_kernel(pt_ref, q_ref, knew_ref, vnew_ref, blast_ref, bnew_ref, r_ref, pm_ref,
                  lam_ref, g_ref, *refs, scale, lam_init, P):
    kp = refs[:P]
    vp = refs[P:2 * P]
    o_ref, qb_ref, m_ref, l_ref, acc_ref = refs[2 * P:]
    step = pl.program_id(1)
    is_last = step == pl.num_programs(1) - 1
    W = qb_ref.shape[0]
    rep = W // N_GROUPS

    @pl.when(step == 0)
    def _():
        qb_ref[...] = jnp.broadcast_to(q_ref[0] * scale, qb_ref.shape)
        m_ref[...] = jnp.full_like(m_ref, NEG)
        l_ref[...] = jnp.zeros_like(l_ref)
        acc_ref[...] = jnp.zeros_like(acc_ref)

    last_f = jnp.where(is_last, 1.0, 0.0)
    lgs = [_group_logits(kp[j][0, 0], qb_ref[...]) for j in range(P)]
    lgs[P - 1] = lgs[P - 1] + last_f * blast_ref[...]
    m_old = m_ref[...]
    m_new = m_old
    for lg in lgs:
        m_new = jnp.maximum(m_new, jnp.max(lg, axis=-1, keepdims=True))
    alpha = jnp.exp(m_old - m_new)
    ps = [jnp.exp(lg - m_new) for lg in lgs]
    m_ref[...] = m_new
    l_ref[...] = l_ref[...] * alpha + sum(jnp.sum(p, axis=-1, keepdims=True) for p in ps)
    pr = _dot(jnp.concatenate(ps, axis=0).astype(BF16), r_ref[...])
    o = jnp.zeros(acc_ref.shape, F32)
    for j in range(P):
        prj = (pr[j * N_GROUPS:(j + 1) * N_GROUPS] * pm_ref[...]).astype(BF16)
        o = o + _dot(prj, vp[j][0, 0].astype(BF16))
    acc_ref[...] = acc_ref[...] * alpha + o

    @pl.when(is_last)
    def _():
        q = q_ref[0] * scale
        lgn = jnp.sum((knew_ref[0] * q).reshape(N_GROUPS, rep, 1), axis=1) + bnew_ref[...]
        m = m_ref[...]
        mx = jnp.maximum(m, lgn)
        a2 = jnp.exp(m - mx)
        pn = jnp.exp(lgn - mx)
        l = l_ref[...] * a2 + pn
        on = (acc_ref[...] * a2 + pn * vnew_ref[0]) / l
        lam = _lambda(lam_ref[...], lam_init)
        r = lax.broadcasted_iota(jnp.int32, (N_GROUPS, N_GROUPS), 0)
        c = lax.broadcasted_iota(jnp.int32, (N_GROUPS, N_GROUPS), 1)
        comb = jnp.where(c == 2 * r, 1.0, 0.0) - lam * jnp.where(c == 2 * r + 1, 1.0, 0.0)
        o = _dot(comb, on, True)
        o = o * lax.rsqrt(jnp.mean(o * o, axis=-1, keepdims=True) + SUBLN_EPS)
        o_ref[0] = o * g_ref[...] * (1.0 - lam_init)


def decode_b(page_table, layer, qcol, knew, vnew_g, blast, bnew, rmat, pmask, lamv, g, kT, vrows,
             scale, lam_init):
    DB, npg = page_table.shape
    W = qcol.shape[1]
    P = min(ATTN_PAGES_PER_STEP, npg)
    dv = 2 * D_B
    assert npg % P == 0
    per_b = lambda b, s, pt: (b, 0, 0)
    c2 = lambda b, s, pt: (0, 0)
    grid_spec = pltpu.PrefetchScalarGridSpec(
        num_scalar_prefetch=1,
        grid=(DB, npg // P),
        in_specs=[pl.BlockSpec((1, W, 1), per_b),
                  pl.BlockSpec((1, W, 1), per_b),
                  pl.BlockSpec((1, N_GROUPS, dv), per_b),
                  pl.BlockSpec((N_GROUPS, PAGE), c2),
                  pl.BlockSpec((N_GROUPS, 1), c2),
                  pl.BlockSpec((PAGE, W), c2),
                  pl.BlockSpec((N_GROUPS, W), c2),
                  pl.BlockSpec((4, D_B), c2),
                  pl.BlockSpec((1, dv), c2)]
                 + _page_specs(layer, P, W, P) + _page_specs(layer, P, W, P),
        out_specs=pl.BlockSpec((1, N_GROUPS, dv), per_b),
        scratch_shapes=[pltpu.VMEM((W, PAGE), F32), pltpu.VMEM((N_GROUPS, 1), F32),
                        pltpu.VMEM((N_GROUPS, 1), F32), pltpu.VMEM((N_GROUPS, dv), F32)],
    )
    return pl.pallas_call(
        functools.partial(_dec_b_kernel, scale=scale, lam_init=lam_init, P=P),
        grid_spec=grid_spec,
        out_shape=jax.ShapeDtypeStruct((DB, N_GROUPS, dv), F32),
        compiler_params=_cparams(("arbitrary", "arbitrary")),
        name="decode_b",
    )(page_table, qcol, knew, vnew_g, blast, bnew, rmat, pmask, lamv, g,
      *([kT] * P), *([vrows] * P))


def _dec_m_kernel(q_ref, k_ref, v_ref, o_ref):
    n = k_ref.shape[2] // 8
    q = q_ref[0]
    k3 = k_ref[0, 0].reshape(n, 8, D_M)
    lg = jnp.sum(k3 * q[None], axis=-1, keepdims=True)
    m8 = jnp.max(lg, axis=0)
    m4 = jnp.maximum(m8[:H_M], m8[H_M:])
    p = jnp.exp(lg - jnp.concatenate([m4, m4], axis=0)[None])
    l8 = jnp.sum(p, axis=0)
    o8 = jnp.sum(p * v_ref[0, 0].reshape(n, 8, D_M), axis=0)
    o_ref[0] = (o8[:H_M] + o8[H_M:]) / (l8[:H_M] + l8[H_M:])


def decode_m(layer, q8, mem_k, mem_v):
    DB = q8.shape[0]
    rows = mem_k.shape[2]
    return pl.pallas_call(
        _dec_m_kernel,
        grid=(DB,),
        in_specs=[pl.BlockSpec((1, 8, D_M), lambda b: (b, 0, 0)),
                  pl.BlockSpec((1, 1, rows, D_M), lambda b: (layer, b, 0, 0)),
                  pl.BlockSpec((1, 1, rows, D_M), lambda b: (layer, b, 0, 0))],
        out_specs=pl.BlockSpec((1, H_M, D_M), lambda b: (b, 0, 0)),
        out_shape=jax.ShapeDtypeStruct((DB, H_M, D_M), F32),
        compiler_params=_cparams(("arbitrary",)),
        name="decode_m",
    )(q8, mem_k, mem_v)


def _t5_bucket_np(rel):
    n = np.maximum(rel, 0)
    max_exact = NUM_BUCKETS // 2
    nf = np.maximum(n, 1).astype(np.float32)
    large = max_exact + (np.log(nf / np.float32(max_exact))
                         / np.float32(math.log(MAX_DISTANCE / max_exact))
                         * np.float32(NUM_BUCKETS - max_exact)).astype(np.int32)
    large = np.minimum(large, NUM_BUCKETS - 1)
    return np.where(n < max_exact, n, large).astype(np.int32)


def _far_bucket(first_far, last_far):
    b = _t5_bucket_np(np.arange(first_far, last_far + 1))
    assert (b == b[0]).all(), "relative-position bias must be constant beyond one tile"
    return int(b[0])


def _bias_lookup(rel_bias, buckets):
    onehot = jax.nn.one_hot(jnp.asarray(buckets, jnp.int32), NUM_BUCKETS, dtype=F32)
    return jnp.einsum('...b,bh->...h', onehot, rel_bias, precision=HIGHEST)


def _band_tables(rel_bias, heads, t_max):
    far = _far_bucket(TQ + 1, max(t_max, TQ + 1))
    r = np.arange(TQ)[None, :] - np.arange(TQ)[:, None]
    tabs = []
    for d in range(2):
        rel = r + d * TQ
        vals = _bias_lookup(rel_bias, _t5_bucket_np(rel))[..., heads] - rel_bias[far][heads]
        vals = jnp.where((rel >= 0)[..., None], vals, NEG)
        tabs.append(jnp.transpose(vals, (2, 0, 1)))
    return jnp.stack(tabs, axis=1).astype(F32)


def _split3(x, order):
    hi = x.astype(BF16)
    lo = (x - hi.astype(F32)).astype(BF16)
    return jnp.concatenate([(hi, lo)[o] for o in order], axis=-1)


def kernel(x_prompt, x_sample, cache_a_k, cache_a_v, cache_idx_k, cache_b_k, cache_b_v, cache_mem_k, cache_mem_v, page_table, mem_prompt, w_in, w_mem_kv, lambda_q1, lambda_k1, lambda_q2, lambda_k2, subln_g, w_branch_a, w_branch_b, w_branch_m, w_out, rel_bias, ln1_g, ln1_b, ln2_g, ln2_b, ffn_w_gate, ffn_w_up, ffn_w_down, router_w, expert_w_gate, expert_w_up, expert_w_down):
    B, T, D = x_prompt.shape
    DB, TS, _ = x_sample.shape
    depth = w_in.shape[0]
    n_pool = cache_a_k.shape[1]
    npg = page_table.shape[1]
    past = npg * PAGE
    n_mem = mem_prompt.shape[1]
    assert TS == 1 and T % TQ == 0 and cache_a_k.shape[2] == PAGE
    alpha = (2 * depth) ** 0.25
    topk_p = min(TOPK_MAX, T // 4)
    topk_s = min(TOPK_MAX, (past + TS) // 4)
    W = HEAD_COLS
    nq = T // TQ

    c_ik = 3 * A_WIDTH + H_IDX * D_IDX
    c_b = c_ik + D_IDX + H_IDX
    n_main = 8 * W + 3 * D
    tail_pad = 128 - D_IDX - H_IDX

    low = jnp.asarray(np.tril(np.ones((TQ, TQ), np.float32), -1), BF16)
    band_a = _band_tables(rel_bias, np.arange(H_A), T)
    band_b = _band_tables(rel_bias, H_A + np.arange(H_B), T)

    far_s = _far_bucket(PAGE + 1, past)
    b_last = (_bias_lookup(rel_bias, _t5_bucket_np(PAGE - np.arange(PAGE))) - rel_bias[far_s]).T
    b_new = (rel_bias[0] - rel_bias[far_s])[:, None]
    blast_a, bnew_a = b_last[:H_A], b_new[:H_A]
    blast_b, bnew_b = jnp.repeat(b_last[H_A:], 2, axis=0), jnp.repeat(b_new[H_A:], 2, axis=0)
    lane = np.arange(W)
    rmat = jnp.asarray(lane[None, :] // H_B == np.arange(PAGE)[:, None], BF16)
    pmask = jnp.asarray(lane[None, :] % H_B == np.arange(N_GROUPS)[:, None] // 2, F32)

    idxT = jnp.transpose(cache_idx_k, (0, 1, 3, 2))
    akT = jnp.transpose(cache_a_k, (0, 1, 3, 4, 2)).reshape(depth, n_pool, W, PAGE)
    avT = jnp.transpose(cache_a_v, (0, 1, 3, 4, 2)).reshape(depth, n_pool, W, PAGE)
    bkT = jnp.transpose(cache_b_k, (0, 1, 3, 4, 5, 2)).reshape(depth, n_pool, W, PAGE)
    bv_rows = cache_b_v.reshape(depth, n_pool, PAGE * H_B, 2 * D_B)
    memk_rows = cache_mem_k.reshape(depth, DB, n_mem * H_M, D_M)
    memv_rows = cache_mem_v.reshape(depth, DB, n_mem * H_M, D_M)

    outs = {n: [] for n in ('akp', 'avp', 'ikp', 'bkp', 'bvp', 'mkp', 'mvp',
                            'aks', 'avs', 'iks', 'bks', 'bvs')}
    x_p = x_prompt.reshape(B * T, D)
    x_s = x_sample.reshape(DB, D)
    for l in range(depth):
        lam_init = 0.8 - 0.6 * math.exp(-0.3 * l)
        lamv = jnp.stack([lambda_q1[l], lambda_k1[l], lambda_q2[l], lambda_k2[l]])
        wl = w_in[l]
        w_perm = jnp.concatenate([wl[:, :c_ik], wl[:, c_b:], wl[:, c_ik:c_b],
                                  jnp.zeros((D, tail_pad), F32)], axis=1)
        w_main_bf = w_perm[:, :n_main].astype(BF16)
        w_tail_bf = w_perm[:, n_main:].astype(BF16)
        wa_bf, wb_bf, wm_bf, wo_bf = (w[l].astype(BF16) for w in (w_branch_a, w_branch_b, w_branch_m, w_out))
        lg1, lb1 = ln1_g[l][None], ln1_b[l][None]
        lg2, lb2 = ln2_g[l][None], ln2_b[l][None]
        g_sub = subln_g[l][None]
        j = l // 2
        if l % 2 == 0:
            rw = jnp.zeros((D, 128), BF16)
            wg, wu, wd = ffn_w_gate[j][None], ffn_w_up[j][None], ffn_w_down[j][None]
            routed = False
        else:
            rw = jnp.pad(router_w[j], ((0, 0), (0, 128 - N_EXPERTS))).astype(BF16)
            wg, wu, wd = expert_w_gate[j], expert_w_up[j], expert_w_down[j]
            routed = True

        h = matmul(x_p, w_main_bf, 1024 if B * T % 1024 == 0 else 512, 1024,
                   name="proj_prompt")
        tail = matmul(x_p, w_tail_bf, 512, 128, name="proj_tail")
        h3 = h.reshape(B, T, n_main)
        col = lambda c: h3[:, :, c * W:(c + 1) * W]

        def heads(c, nh, dh, dt):
            return col(c).reshape(B, T, nh, dh).transpose(0, 2, 1, 3).astype(dt)

        def q_t(c, scale):
            return jnp.transpose(col(c) * scale, (0, 2, 1)).astype(BF16)

        def v_t(c):
            return col(c).reshape(B, nq, TQ, W).transpose(0, 1, 3, 2).astype(BF16)

        a_k, a_v, b_k, b_v = col(1), col(2), col(5), col(6)
        i_k = tail[:, :D_IDX].reshape(B, T, D_IDX)
        i_w = tail[:, D_IDX:D_IDX + H_IDX].reshape(B, T, H_IDX)
        iq = col(3).reshape(B, T, H_IDX, D_IDX) * (D_IDX ** -0.5)
        iqT = jnp.transpose(_split3(iq, (0, 0, 1)), (0, 2, 3, 1))
        ik3 = _split3(i_k, (0, 1, 0))
        wT = jnp.transpose(i_w * (H_IDX ** -0.5), (0, 2, 1))
        o_a = attn_a_prompt(q_t(0, D_A ** -0.5), heads(1, H_A, D_A, BF16), v_t(2), iqT, ik3, wT,
                            band_a, low, topk_p)
        o_b = attn_b_prompt(q_t(4, D_B ** -0.5), heads(5, 2 * H_B, D_B, BF16), v_t(6),
                            band_b, lamv, g_sub.T, lam_init)
        kv = matmul(mem_prompt.reshape(B * n_mem, D), w_mem_kv[l].astype(BF16), 512, 1024,
                    name="proj_mem")
        mk = kv[:, :M_WIDTH].reshape(B, n_mem, H_M, D_M)
        mv = kv[:, M_WIDTH:].reshape(B, n_mem, H_M, D_M)
        qm = (heads(7, H_M, D_M, F32) * (D_M ** -0.5)).astype(BF16)
        o_m = attn_m_prompt(qm, mk.transpose(0, 2, 1, 3).astype(BF16),
                            mv.transpose(0, 2, 1, 3).astype(BF16))
        x_p = merge(x_p, o_a.reshape(B * T, W), o_b.reshape(B * T, W), o_m.reshape(B * T, W),
                    h, 8 * W // D, wa_bf, wb_bf, wm_bf, wo_bf, lg1, lb1, alpha, 512, name="merge_prompt")
        x_p_mid = x_p
        if routed:
            xs, tile_expert, n_used, idx_s, w12 = moe_dispatch(x_p_mid, rw, N_EXPERTS)
        outs['akp'].append(a_k.reshape(B, T, H_A, D_A))
        outs['avp'].append(a_v.reshape(B, T, H_A, D_A))
        outs['ikp'].append(i_k)
        outs['bkp'].append(b_k.reshape(B, T, H_B, 2, D_B))
        outs['bvp'].append(b_v.reshape(B, T, H_B, 2 * D_B))
        outs['mkp'].append(mk)
        outs['mvp'].append(mv)

        hs = matmul(x_s, w_main_bf, DB, 1024, name="proj_sample")
        ts = matmul(x_s, w_tail_bf, DB, 128, name="proj_tail_sample")
        sa_q, sa_k, sa_v, s_iq = (hs[:, c * W:(c + 1) * W] for c in range(4))
        sb_q, sb_k, sb_v, sm_q = (hs[:, c * W:(c + 1) * W] for c in range(4, 8))
        s_ik = ts[:, :D_IDX]
        s_iw = ts[:, D_IDX:D_IDX + H_IDX]
        colv = lambda t: t.reshape(DB, -1, 1)
        am, amn = idx_sample(page_table, l, s_iq.reshape(DB, H_IDX, D_IDX), s_iw, s_ik, idxT, topk_s)
        o_a = decode_a(page_table, l, colv(sa_q), am, amn, colv(sa_k), colv(sa_v),
                       blast_a, bnew_a, akT, avT, D_A ** -0.5).reshape(DB, W)
        vnew_g = jnp.repeat(sb_v.reshape(DB, H_B, 2 * D_B), 2, axis=1)
        o_b = decode_b(page_table, l, colv(sb_q), colv(sb_k), vnew_g, blast_b, bnew_b, rmat, pmask,
                       lamv, g_sub, bkT, bv_rows, D_B ** -0.5, lam_init)[:, :H_B].reshape(DB, W)
        q8 = jnp.tile(sm_q.reshape(DB, H_M, D_M) * (D_M ** -0.5), (1, 2, 1))
        o_m = decode_m(l, q8, memk_rows, memv_rows).reshape(DB, W)
        wg_bf, wu_bf, wd_bf = wg.astype(BF16), wu.astype(BF16), wd.astype(BF16)
        if routed:
            o_sorted = moe_gmm(tile_expert, n_used, xs, wg_bf, wu_bf, wd_bf, _ff_tile(wg.shape[2]))
            og = moe_gather(o_sorted, idx_s, B * T)
        x_s = merge(x_s, o_a, o_b, o_m, hs, 8 * W // D, wa_bf, wb_bf, wm_bf, wo_bf, lg1, lb1, alpha,
                    DB, name="merge_sample")
        x_s = ffn(x_s, rw, wg_bf, wu_bf, wd_bf, lg2, lb2, alpha, DB, _ff_tile(wg.shape[2]), routed,
                  name="ffn_sample")
        if routed:
            x_p = moe_combine(x_p_mid, og, w12, lg2, lb2, alpha)
        else:
            x_p = ffn(x_p_mid, rw, wg_bf, wu_bf, wd_bf, lg2, lb2, alpha,
                      512, _ff_tile(wg.shape[2]), False, name="ffn_prompt")
        outs['aks'].append(sa_k.reshape(DB, TS, H_A, D_A))
        outs['avs'].append(sa_v.reshape(DB, TS, H_A, D_A))
        outs['iks'].append(s_ik.reshape(DB, TS, D_IDX))
        outs['bks'].append(sb_k.reshape(DB, TS, H_B, 2, D_B))
        outs['bvs'].append(sb_v.reshape(DB, TS, H_B, 2 * D_B))

    st = {n: jnp.stack(v) for n, v in outs.items()}
    return (x_p.reshape(B, T, D), x_s.reshape(DB, TS, D),
            st['akp'], st['avp'], st['ikp'], st['bkp'], st['bvp'], st['mkp'], st['mvp'],
            st['aks'], st['avs'], st['iks'], st['bks'], st['bvs'])
```
